```python
import jax
import jax.numpy as jnp
from jax import lax
import numpy as np

D_MODEL = 1024
BATCH = 4
SEQ = 4096
DEPTH = 1
DEC_BATCH = 16
DEC_SEQ = 16
PAST_LEN = 4096

CHUNK = 64
EPS = 1e-6
GLA_HEADS = 4
GLA_DK = D_MODEL // 2
GLA_DV = D_MODEL
GLA_DK_HEAD = GLA_DK // GLA_HEADS
GLA_DV_HEAD = GLA_DV // GLA_HEADS
GLA_GATE_RANK = 16
GLA_GATE_NORM = 16.0
HGRN_EXPAND = 128
HGRN_HEADS = D_MODEL // HGRN_EXPAND
HGRN_DK = D_MODEL
HGRN_DV = D_MODEL
HGRN_DK_HEAD = HGRN_EXPAND
HGRN_DV_HEAD = HGRN_DV // HGRN_HEADS
N_EXPERTS = 32
TOP_K = 4
D_FF = D_MODEL
SWIGLU_LIMIT = 7.0
SWIGLU_ALPHA = 1.702
MOE_BLOCK = 128
IN_SPLITS = (GLA_DK, GLA_DK, GLA_DV, GLA_DV, GLA_GATE_RANK,
             HGRN_DK, HGRN_DK, HGRN_DV, HGRN_DV, D_MODEL, D_MODEL)
IN_WIDTH = sum(IN_SPLITS)

kernel_name = 'gla_hgrn2_parallel_moe_stream_step'


def rms_norm(x, g):
    xf = x.astype(jnp.float32)
    y = xf * lax.rsqrt(jnp.mean(xf * xf, axis=-1, keepdims=True) + EPS)
    return (y * g.astype(jnp.float32)).astype(x.dtype)


def split_cols(t, sizes):
    out, start = [], 0
    for s in sizes:
        out.append(t[..., start:start + s])
        start += s
    return out


def gated_linear_attention(q, k, v, log_a, s0):
    f32 = jnp.float32
    b, l, h, _ = q.shape
    dv = v.shape[-1]
    pad = (-l) % CHUNK
    n = (l + pad) // CHUNK

    def blocks(t):
        t = jnp.pad(t.astype(f32), ((0, 0), (0, pad), (0, 0), (0, 0)))
        return t.reshape(b, n, CHUNK, h, t.shape[-1]).transpose(1, 0, 3, 2, 4)

    causal = jnp.tril(jnp.ones((CHUNK, CHUNK), dtype=bool))[:, :, None]

    def step(s, inp):
        qc, kc, vc, ac = inp
        cum = jnp.cumsum(ac, axis=2)
        o_past = jnp.einsum('bhtk,bhkv->bhtv', qc * jnp.exp(cum), s)
        rel = jnp.where(causal, cum[:, :, :, None, :] - cum[:, :, None, :, :], -jnp.inf)
        scores = jnp.einsum('bhtk,bhsk,bhtsk->bhts', qc, kc, jnp.exp(rel))
        o = o_past + jnp.einsum('bhts,bhsv->bhtv', scores, vc)
        tot = cum[:, :, -1:, :]
        s_new = (jnp.exp(tot[:, :, 0, :])[..., None] * s
                 + jnp.einsum('bhsk,bhsv->bhkv', kc * jnp.exp(tot - cum), vc))
        return s_new, o

    s_fin, o = lax.scan(step, s0.astype(f32), (blocks(q), blocks(k), blocks(v), blocks(log_a)))
    o = o.transpose(1, 0, 3, 2, 4).reshape(b, n * CHUNK, h, dv)[:, :l]
    return o, s_fin


def moe_ffn(h, w_router, b_router, w_gate_up, b_gate_up, w_down, b_down):
    n, d = h.shape
    logits = (h @ w_router).astype(jnp.float32) + b_router.astype(jnp.float32)
    top_v, top_e = lax.top_k(logits, TOP_K)
    top_w = jax.nn.softmax(top_v, axis=-1)
    nk = n * TOP_K
    flat_e = top_e.reshape(nk)
    flat_t = jnp.repeat(jnp.arange(n, dtype=jnp.int32), TOP_K)
    flat_w = top_w.reshape(nk)
    order = jnp.argsort(flat_e)
    se = flat_e[order]
    counts = jnp.bincount(flat_e, length=N_EXPERTS)
    padded = (counts + MOE_BLOCK - 1) // MOE_BLOCK * MOE_BLOCK
    start = jnp.cumsum(counts) - counts
    pend = jnp.cumsum(padded)
    pstart = pend - padded
    dest = pstart[se] + jnp.arange(nk) - start[se]
    n_blocks = (nk + N_EXPERTS * (MOE_BLOCK - 1) + MOE_BLOCK - 1) // MOE_BLOCK
    cap = n_blocks * MOE_BLOCK
    slot_tok = jnp.zeros((cap,), jnp.int32).at[dest].set(flat_t[order])
    slot_w = jnp.zeros((cap,), jnp.float32).at[dest].set(flat_w[order])
    block_e = jnp.clip(jnp.searchsorted(pend, jnp.arange(n_blocks) * MOE_BLOCK, side='right'),
                       0, N_EXPERTS - 1)

    def expert_block(args):
        e, toks = args
        xb = h[toks]
        gu = xb @ w_gate_up[e] + b_gate_up[e]
        gate, up = gu[:, :D_FF], gu[:, D_FF:]
        gate = jnp.minimum(gate, SWIGLU_LIMIT)
        up = jnp.clip(up, -SWIGLU_LIMIT, SWIGLU_LIMIT)
        act = (up + 1.0) * (gate * jax.nn.sigmoid(SWIGLU_ALPHA * gate))
        return act @ w_down[e] + b_down[e]

    out = lax.map(expert_block, (block_e, slot_tok.reshape(n_blocks, MOE_BLOCK)))
    y = jnp.zeros((n, d), jnp.float32).at[slot_tok].add(
        out.reshape(cap, d).astype(jnp.float32) * slot_w[:, None])
    return y.astype(h.dtype)


def trunk_layer(x, c, s_gla, s_hgrn, lb, w_ada, b_ada, g_mix, g_ffn, w_in, w_gla_gate2, b_gla_gate,
                g_gla_onorm, g_hgrn_onorm, w_branch_a, w_branch_b, w_out,
                w_router, b_router, w_gate_up, b_gate_up, w_down, b_down):
    f32 = jnp.float32
    b, l, d = x.shape
    mod = jax.nn.silu(c) @ w_ada + b_ada
    sh1, sc1, gt1, sh2, sc2, gt2 = [m[:, None, :] for m in jnp.split(mod, 6, axis=-1)]
    h = rms_norm(x, g_mix) * (1.0 + sc1) + sh1
    (q_a, k_a, v_a, r_a, lr_a, q_b, f_b, i_b, r_b, u_a, u_b) = split_cols(h @ w_in, IN_SPLITS)
    log_a = jax.nn.log_sigmoid((lr_a @ w_gla_gate2 + b_gla_gate).astype(f32)) / GLA_GATE_NORM
    o_a, s_gla_new = gated_linear_attention(
        q_a.reshape(b, l, GLA_HEADS, GLA_DK_HEAD) * GLA_DK_HEAD ** -0.5,
        k_a.reshape(b, l, GLA_HEADS, GLA_DK_HEAD),
        v_a.reshape(b, l, GLA_HEADS, GLA_DV_HEAD),
        log_a.reshape(b, l, GLA_HEADS, GLA_DK_HEAD), s_gla)
    o_a = rms_norm(o_a, g_gla_onorm).reshape(b, l, GLA_DV).astype(x.dtype) * jax.nn.silu(r_a)
    fgate = lb + (1.0 - lb) * jax.nn.sigmoid(f_b.astype(f32))
    o_b, s_hgrn_new = gated_linear_attention(
        (jax.nn.silu(q_b) * HGRN_DK_HEAD ** -0.5).reshape(b, l, HGRN_HEADS, HGRN_DK_HEAD),
        (1.0 - fgate).reshape(b, l, HGRN_HEADS, HGRN_DK_HEAD),
        i_b.reshape(b, l, HGRN_HEADS, HGRN_DV_HEAD),
        jnp.log(fgate).reshape(b, l, HGRN_HEADS, HGRN_DK_HEAD), s_hgrn)
    o_b = rms_norm(o_b, g_hgrn_onorm).reshape(b, l, HGRN_DV).astype(x.dtype) * jax.nn.silu(r_b)
    merged = jax.nn.sigmoid(u_a) * (o_a @ w_branch_a) + jax.nn.sigmoid(u_b) * (o_b @ w_branch_b)
    x = x + gt1 * (merged @ w_out)
    h2 = rms_norm(x, g_ffn) * (1.0 + sc2) + sh2
    y = moe_ffn(h2.reshape(b * l, d), w_router, b_router, w_gate_up, b_gate_up,
                w_down, b_down).reshape(b, l, d)
    x = x + gt2 * y
    return x, s_gla_new, s_hgrn_new


def setup_inputs(seed: int = 0) -> dict:
    key = jax.random.key(seed)
    ks = jax.random.split(key, 26)
    f32 = jnp.float32
    D = D_MODEL

    def nrm(k, shape, scale):
        return jax.random.normal(k, shape, f32) * scale

    return {
        'x_prompt': nrm(ks[0], (BATCH, SEQ, D), 1.0),
        'x_sample': nrm(ks[1], (DEC_BATCH, DEC_SEQ, D), 1.0),
        'c_prompt': nrm(ks[2], (BATCH, D), 1.0),
        'c_sample': nrm(ks[3], (DEC_BATCH, D), 1.0),
        'state_gla': nrm(ks[4], (DEPTH, DEC_BATCH, GLA_HEADS, GLA_DK_HEAD, GLA_DV_HEAD), 0.5),
        'state_hgrn': nrm(ks[5], (DEPTH, DEC_BATCH, HGRN_HEADS, HGRN_DK_HEAD, HGRN_DV_HEAD), 0.5),
        'w_ada': nrm(ks[6], (DEPTH, D, 6 * D), 0.5 * D ** -0.5),
        'b_ada': nrm(ks[7], (DEPTH, 6 * D), 0.02),
        'g_norm_mix': 1.0 + nrm(ks[8], (DEPTH, D), 0.02),
        'g_norm_ffn': 1.0 + nrm(ks[9], (DEPTH, D), 0.02),
        'w_in': nrm(ks[10], (DEPTH, D, IN_WIDTH), D ** -0.5),
        'w_gla_gate2': nrm(ks[11], (DEPTH, GLA_GATE_RANK, GLA_DK), GLA_GATE_RANK ** -0.5),
        'b_gla_gate': nrm(ks[12], (DEPTH, GLA_DK), 0.1),
        'g_gla_onorm': 1.0 + nrm(ks[13], (DEPTH, GLA_DV_HEAD), 0.02),
        'hgrn_lb_logits': nrm(ks[14], (DEPTH + 1, HGRN_DK), 0.5),
        'g_hgrn_onorm': 1.0 + nrm(ks[15], (DEPTH, HGRN_DV_HEAD), 0.02),
        'w_branch_a': nrm(ks[16], (DEPTH, GLA_DV, D), GLA_DV ** -0.5),
        'w_branch_b': nrm(ks[17], (DEPTH, HGRN_DV, D), HGRN_DV ** -0.5),
        'w_out': nrm(ks[18], (DEPTH, D, D), D ** -0.5),
        'w_router': nrm(ks[19], (DEPTH, D, N_EXPERTS), D ** -0.5),
        'b_router': nrm(ks[20], (DEPTH, N_EXPERTS), 0.01),
        'w_gate_up': nrm(ks[21], (DEPTH, N_EXPERTS, D, 2 * D_FF), D ** -0.5),
        'b_gate_up': nrm(ks[22], (DEPTH, N_EXPERTS, 2 * D_FF), 0.01),
        'w_down': nrm(ks[23], (DEPTH, N_EXPERTS, D_FF, D), D_FF ** -0.5),
        'b_down': nrm(ks[24], (DEPTH, N_EXPERTS, D), 0.01),
        'g_final': 1.0 + nrm(ks[25], (D,), 0.02),
    }


def reference(x_prompt, x_sample, c_prompt, c_sample, state_gla, state_hgrn,
              w_ada, b_ada, g_norm_mix, g_norm_ffn, w_in, w_gla_gate2, b_gla_gate, g_gla_onorm,
              hgrn_lb_logits, g_hgrn_onorm, w_branch_a, w_branch_b, w_out,
              w_router, b_router, w_gate_up, b_gate_up, w_down, b_down, g_final):
    f32 = jnp.float32
    lbs = jnp.cumsum(jax.nn.softmax(hgrn_lb_logits.astype(f32), axis=0), axis=0)
    bp = x_prompt.shape[0]
    yp, ys = x_prompt, x_sample
    sg_p, sh_p, sg_s, sh_s = [], [], [], []
    for layer in range(DEPTH):
        p = (w_ada[layer], b_ada[layer], g_norm_mix[layer], g_norm_ffn[layer], w_in[layer],
             w_gla_gate2[layer], b_gla_gate[layer], g_gla_onorm[layer], g_hgrn_onorm[layer],
             w_branch_a[layer], w_branch_b[layer], w_out[layer],
             w_router[layer], b_router[layer], w_gate_up[layer], b_gate_up[layer],
             w_down[layer], b_down[layer])
        zg = jnp.zeros((bp, GLA_HEADS, GLA_DK_HEAD, GLA_DV_HEAD), f32)
        zh = jnp.zeros((bp, HGRN_HEADS, HGRN_DK_HEAD, HGRN_DV_HEAD), f32)
        yp, gp, hp = trunk_layer(yp, c_prompt, zg, zh, lbs[layer], *p)
        ys, gs, hs = trunk_layer(ys, c_sample, state_gla[layer], state_hgrn[layer], lbs[layer], *p)
        sg_p.append(gp)
        sh_p.append(hp)
        sg_s.append(gs)
        sh_s.append(hs)
    y_prompt = rms_norm(yp, g_final)
    y_sample = rms_norm(ys, g_final)
    return (y_prompt, y_sample, jnp.stack(sg_p), jnp.stack(sh_p), jnp.stack(sg_s), jnp.stack(sh_s))
```

```python
import functools

import jax
import jax.numpy as jnp
from jax import lax
from jax.experimental import pallas as pl
from jax.experimental.pallas import tpu as pltpu

F32 = jnp.float32
BF16 = jnp.bfloat16
I32 = jnp.int32

EPS = 1e-6
D = 1024
GLA_H, GLA_DK, GLA_DV = 4, 128, 256
HG_H, HG_DK, HG_DV = 8, 128, 128
GATE_RANK = 16
GATE_NORM = 16.0
N_EXP = 32
TOP_K = 4
SWIGLU_LIMIT = 7.0
SWIGLU_ALPHA = 1.702

LANES = 128
TM = 256
CHUNK = 128
SUB = 32
NSUB = CHUNK // SUB
EXP_CLAMP = 80.0
BM = 256
NEG = -1e30

C_QA, C_KA, C_VA, C_RA, C_QB, C_FB, C_IB, C_RB, C_UA, C_UB, C_END = (
    0, 512, 1024, 2048, 3072, 4096, 5120, 6144, 7168, 8192, 9216)


def _sigmoid(x):
    return 1.0 / (1.0 + jnp.exp(-x))


def _dot(a, b):
    return jnp.dot(a, b, preferred_element_type=F32)


def _split_bf16(x):
    hi = x.astype(BF16)
    lo = (x - hi.astype(F32)).astype(BF16)
    return hi, lo


def _log2(n):
    assert n > 0 and n & (n - 1) == 0, "power of two expected"
    return n.bit_length() - 1


def _cparams(vmem_mb):
    return pltpu.CompilerParams(dimension_semantics=("arbitrary",),
                                vmem_limit_bytes=vmem_mb * 1024 * 1024)


def _mod_kernel(c_ref, w_ref, b_ref, o_ref):
    c = c_ref[...]
    s = (c * _sigmoid(c)).astype(BF16)
    o_ref[...] = _dot(s, w_ref[...].astype(BF16)) + b_ref[...]


def _mod_call(c_all, w_ada, b_ada):
    rows = c_all.shape[0]
    return pl.pallas_call(
        _mod_kernel,
        grid=(6,),
        in_specs=[pl.BlockSpec((rows, D), lambda j: (0, 0)),
                  pl.BlockSpec((D, D), lambda j: (0, j)),
                  pl.BlockSpec((1, D), lambda j: (0, j))],
        out_specs=pl.BlockSpec((rows, D), lambda j: (0, j)),
        out_shape=jax.ShapeDtypeStruct((rows, 6 * D), F32),
        compiler_params=_cparams(32),
        name="mod",
    )(c_all, w_ada, b_ada)


def _row_mod(i, n_prompt_rows, prompt_len, sample_len, n_prompt_seq, mod_hi, mod_lo):
    row = i * TM + lax.broadcasted_iota(I32, (TM, mod_hi.shape[0]), 0)
    seq = jnp.where(row < n_prompt_rows, row >> _log2(prompt_len),
                    n_prompt_seq + ((row - n_prompt_rows) >> _log2(sample_len)))
    onehot = (lax.broadcasted_iota(I32, (TM, mod_hi.shape[0]), 1) == seq).astype(BF16)
    return _dot(onehot, mod_hi) + _dot(onehot, mod_lo)


def _proj_kernel(geom, xp_ref, xs_ref, mhi_ref, mlo_ref, g_ref, lbl_ref, wm_ref, wlr_ref,
                 w2_ref, bg_ref,
                 qa_ref, ka_ref, va_ref, ra_ref, ga_ref, qb_ref, kb_ref, gb_ref, ib_ref,
                 rb_ref, ua_ref, ub_ref):
    i = pl.program_id(0)
    n_prompt_tiles = geom[0] // TM
    x = jnp.where(i < n_prompt_tiles, xp_ref[...], xs_ref[...])
    mod = _row_mod(i, *geom, mhi_ref[...], mlo_ref[...])
    sh, sc = mod[:, :D], mod[:, D:]
    ms = jnp.mean(x * x, axis=-1, keepdims=True)
    h = x * lax.rsqrt(ms + EPS) * g_ref[...]
    hb = (h * (1.0 + sc) + sh).astype(BF16)

    def proj(a, b):
        return _dot(hb, wm_ref[:, a:b])

    qa_ref[...] = (proj(C_QA, C_KA) * GLA_DK ** -0.5).astype(BF16)
    ka_ref[...] = proj(C_KA, C_VA).astype(BF16)
    va_ref[...] = proj(C_VA, C_RA).astype(BF16)
    r = proj(C_RA, C_QB)
    ra_ref[...] = (r * _sigmoid(r)).astype(BF16)
    lr_hi, lr_lo = _split_bf16(_dot(hb, wlr_ref[...]))
    xg = _dot(lr_hi, w2_ref[...]) + _dot(lr_lo, w2_ref[...]) + bg_ref[...]
    ga_ref[...] = (jnp.minimum(xg, 0.0) - jnp.log1p(jnp.exp(-jnp.abs(xg)))) * (1.0 / GATE_NORM)
    q = proj(C_QB, C_FB)
    qb_ref[...] = (q * _sigmoid(q) * HG_DK ** -0.5).astype(BF16)
    lbl = lbl_ref[...]
    e = jnp.exp(lbl - jnp.max(lbl, axis=0, keepdims=True))
    lb = e[0:1, :] / jnp.sum(e, axis=0, keepdims=True)
    fb = proj(C_FB, C_IB)
    kb = (1.0 - lb) * _sigmoid(-fb)
    kb_ref[...] = kb.astype(BF16)
    gb_ref[...] = jnp.log(lb + (1.0 - lb) * _sigmoid(fb))
    ib_ref[...] = proj(C_IB, C_RB).astype(BF16)
    r = proj(C_RB, C_UA)
    rb_ref[...] = (r * _sigmoid(r)).astype(BF16)
    ua_ref[...] = _sigmoid(proj(C_UA, C_UB)).astype(BF16)
    ub_ref[...] = _sigmoid(proj(C_UB, C_END)).astype(BF16)


def _proj_call(geom, n_rows, xp, xs, mhi, mlo, g_mix, lbl, wm, wlr, w2, bg):
    n_tiles = n_rows // TM
    last_p = geom[0] // TM - 1
    const = lambda i: (0, 0)
    row = lambda i: (i, 0)
    widths = [(512, BF16), (512, BF16), (D, BF16), (D, BF16), (512, F32), (D, BF16), (D, BF16),
              (D, F32), (D, BF16), (D, BF16), (D, BF16), (D, BF16)]
    return pl.pallas_call(
        functools.partial(_proj_kernel, geom),
        grid=(n_tiles,),
        in_specs=[pl.BlockSpec((TM, D), lambda i: (jnp.minimum(i, last_p), 0)),
                  pl.BlockSpec(xs.shape, const),
                  pl.BlockSpec(mhi.shape, const), pl.BlockSpec(mlo.shape, const),
                  pl.BlockSpec((1, D), const), pl.BlockSpec(lbl.shape, const),
                  pl.BlockSpec(wm.shape, const, pipeline_mode=pl.Buffered(1)),
                  pl.BlockSpec(wlr.shape, const), pl.BlockSpec(w2.shape, const),
                  pl.BlockSpec(bg.shape, const)],
        out_specs=[pl.BlockSpec((TM, w), row) for w, _ in widths],
        out_shape=[jax.ShapeDtypeStruct((n_rows, w), dt) for w, dt in widths],
        compiler_params=_cparams(56),
        name="proj",
    )(xp, xs, mhi, mlo, g_mix, lbl, wm, wlr, w2, bg)


def _rec_chunk(tri, causal, g, q, k, v, r, n_heads, dk, dv, s_ref, gn, o_ref, r0, rows_out):
    g_hi, g_lo = _split_bf16(g)
    cum = _dot(tri, g_hi) + _dot(tri, g_lo)
    width = cum.shape[1]
    tot = cum[CHUNK - 1:CHUNK, :]
    refs = [jnp.zeros((1, width), F32)] + [cum[j * SUB - 1:j * SUB, :] for j in range(1, NSUB)]
    d = cum - jnp.concatenate([jnp.broadcast_to(b, (SUB, width)) for b in refs], axis=0)
    qn = q.astype(F32) * jnp.exp(d)
    kn = k.astype(F32) * jnp.exp(jnp.minimum(-d, EXP_CLAMP))
    sub = lambda x, j: x[j * SUB:(j + 1) * SUB, :]
    qs = jnp.concatenate([sub(qn, j) * jnp.exp(refs[j]) for j in range(NSUB)],
                         axis=0).astype(BF16)
    kd = jnp.concatenate([sub(kn, j) * jnp.exp(tot - refs[j]) for j in range(NSUB)], axis=0)
    et = jnp.exp(tot)
    qnb = qn.astype(BF16)
    k_seen = []
    for i in range(NSUB):
        parts = [sub(kn, j) * jnp.exp(refs[i] - refs[j]) for j in range(i)] + [sub(kn, i)]
        if i + 1 < NSUB:
            parts.append(jnp.zeros((CHUNK - (i + 1) * SUB, width), F32))
        k_seen.append(jnp.concatenate(parts, axis=0).astype(BF16))
    for h in range(n_heads):
        ks = slice(h * dk, (h + 1) * dk)
        vs = slice(h * dv, (h + 1) * dv)
        a = jnp.concatenate(
            [lax.dot_general(sub(qnb, i)[:, ks], k_seen[i][:, ks], (((1,), (1,)), ((), ())),
                             preferred_element_type=F32) for i in range(NSUB)], axis=0)
        a = jnp.where(causal, a, 0.0).astype(BF16)
        s = s_ref[0, h]
        kd_t = kd[:, ks].T.astype(BF16)
        av = _dot(jnp.concatenate([a, kd_t], axis=0), v[:, vs])
        o = _dot(qs[:, ks], s.astype(BF16)) + av[:CHUNK]
        et_col = jnp.broadcast_to(et[:, ks], (dk, dk)).T
        s_ref[0, h] = s * jnp.tile(et_col, (1, dv // dk)) + av[CHUNK:]
        ms = jnp.mean(o * o, axis=-1, keepdims=True)
        og = o * lax.rsqrt(ms + EPS) * gn * r[:, vs].astype(F32)
        o_ref[r0:r0 + rows_out, vs] = og[:rows_out].astype(BF16)


def _rec_kernel(rows, has_s0, *refs):
    (qa_ref, ka_ref, va_ref, ra_ref, ga_ref, qb_ref, kb_ref, gb_ref, ib_ref, rb_ref,
     gna_ref, gnb_ref) = refs[:12]
    refs = refs[12:]
    if has_s0:
        s0a_ref, s0b_ref = refs[:2]
        refs = refs[2:]
    oga_ref, ogb_ref, sa_ref, sb_ref = refs

    @pl.when(pl.program_id(1) == 0)
    def _():
        if has_s0:
            sa_ref[...] = s0a_ref[...]
            sb_ref[...] = s0b_ref[...]
        else:
            sa_ref[...] = jnp.zeros_like(sa_ref)
            sb_ref[...] = jnp.zeros_like(sb_ref)

    ti = lax.broadcasted_iota(I32, (CHUNK, CHUNK), 0)
    si = lax.broadcasted_iota(I32, (CHUNK, CHUNK), 1)
    causal = ti >= si
    tri = causal.astype(BF16)
    n_chunks = max(rows // CHUNK, 1)
    rows_out = min(rows, CHUNK)

    def load(ref, c):
        if rows >= CHUNK:
            return ref[c * CHUNK:(c + 1) * CHUNK, :]
        x = ref[...]
        return jnp.concatenate([x, jnp.zeros((CHUNK - rows, x.shape[1]), x.dtype)], axis=0)

    for c in range(n_chunks):
        _rec_chunk(tri, causal, load(ga_ref, c), load(qa_ref, c), load(ka_ref, c), load(va_ref, c),
                   load(ra_ref, c), GLA_H, GLA_DK, GLA_DV, sa_ref, gna_ref[...], oga_ref,
                   c * CHUNK, rows_out)
        _rec_chunk(tri, causal, load(gb_ref, c), load(qb_ref, c), load(kb_ref, c), load(ib_ref, c),
                   load(rb_ref, c), HG_H, HG_DK, HG_DV, sb_ref, gnb_ref[...], ogb_ref,
                   c * CHUNK, rows_out)


def _rec_call(arrs, gna, gnb, n_seq, seq_len, rows, row0, s0=None):
    steps = seq_len // rows
    blk0 = row0 // rows
    row_in = lambda b, t: (blk0 + b * steps + t, 0)
    row_out = lambda b, t: (b * steps + t, 0)
    const = lambda b, t: (0, 0)
    st = lambda b, t: (b, 0, 0, 0)
    in_specs = [pl.BlockSpec((rows, a.shape[1]), row_in) for a in arrs]
    in_specs += [pl.BlockSpec(gna.shape, const), pl.BlockSpec(gnb.shape, const)]
    args = list(arrs) + [gna, gnb]
    if s0 is not None:
        in_specs += [pl.BlockSpec((1, GLA_H, GLA_DK, GLA_DV), st),
                     pl.BlockSpec((1, HG_H, HG_DK, HG_DV), st)]
        args += [s0[0], s0[1]]
    n_rows = n_seq * seq_len
    return pl.pallas_call(
        functools.partial(_rec_kernel, rows, s0 is not None),
        grid=(n_seq, steps),
        in_specs=in_specs,
        out_specs=[pl.BlockSpec((rows, D), row_out), pl.BlockSpec((rows, D), row_out),
                   pl.BlockSpec((1, GLA_H, GLA_DK, GLA_DV), st),
                   pl.BlockSpec((1, HG_H, HG_DK, HG_DV), st)],
        out_shape=[jax.ShapeDtypeStruct((n_rows, D), BF16), jax.ShapeDtypeStruct((n_rows, D), BF16),
                   jax.ShapeDtypeStruct((n_seq, GLA_H, GLA_DK, GLA_DV), F32),
                   jax.ShapeDtypeStruct((n_seq, HG_H, HG_DK, HG_DV), F32)],
        compiler_params=pltpu.CompilerParams(dimension_semantics=("arbitrary", "arbitrary"),
                                             vmem_limit_bytes=48 * 1024 * 1024),
        name="rec_s0" if s0 is not None else "rec",
    )(*args)


def _post_kernel(geom, xp_ref, xs_ref, oap_ref, oas_ref, obp_ref, obs_ref, ua_ref, ub_ref,
                 mhi_ref, mlo_ref, gf_ref, wba_ref, wbb_ref, wo_ref, wrh_ref, wrl_ref, br_ref,
                 x1_ref, h2_ref, te_ref, tw_ref):
    i = pl.program_id(0)
    is_prompt = i < geom[0] // TM
    x = jnp.where(is_prompt, xp_ref[...], xs_ref[...])
    oga = jnp.where(is_prompt, oap_ref[...], oas_ref[...])
    ogb = jnp.where(is_prompt, obp_ref[...], obs_ref[...])
    mod = _row_mod(i, *geom, mhi_ref[...], mlo_ref[...])
    gt1, sh2, sc2 = mod[:, :D], mod[:, D:2 * D], mod[:, 2 * D:]
    merged = (ua_ref[...].astype(F32) * _dot(oga, wba_ref[...])
              + ub_ref[...].astype(F32) * _dot(ogb, wbb_ref[...]))
    x1 = x + gt1 * _dot(merged.astype(BF16), wo_ref[...])
    x1_ref[...] = x1
    ms = jnp.mean(x1 * x1, axis=-1, keepdims=True)
    h2 = x1 * lax.rsqrt(ms + EPS) * gf_ref[...] * (1.0 + sc2) + sh2
    h2_ref[...] = h2
    h_hi, h_lo = _split_bf16(h2)
    logit = (_dot(h_hi, wrh_ref[...]) + _dot(h_lo, wrh_ref[...]) + _dot(h_hi, wrl_ref[...])
             + br_ref[...])
    lane = lax.broadcasted_iota(I32, (TM, LANES), 1)
    lane_f = lane.astype(F32)
    vals, idxs = [], []
    for _ in range(TOP_K):
        m = jnp.max(logit, axis=-1, keepdims=True)
        idx = jnp.min(jnp.where(logit == m, lane_f, float(LANES)), axis=-1, keepdims=True)
        vals.append(m)
        idxs.append(idx)
        logit = jnp.where(lane_f == idx, -jnp.inf, logit)
    es = [jnp.exp(v - vals[0]) for v in vals]
    den = es[0] + es[1] + es[2] + es[3]
    te = jnp.zeros((TM, LANES), F32)
    tw = jnp.zeros((TM, LANES), F32)
    for k in range(TOP_K):
        te = jnp.where(lane == k, idxs[k], te)
        tw = jnp.where(lane == k, es[k] / den, tw)
    te_ref[...] = te.astype(I32)
    tw_ref[...] = tw


def _post_call(geom, n_rows, xp, xs, oap, oas, obp, obs, ua, ub, mhi, mlo, g_ffn, wba, wbb, wo,
               wrh, wrl, br):
    n_tiles = n_rows // TM
    last_p = geom[0] // TM - 1
    const = lambda i: (0, 0)
    row = lambda i: (i, 0)
    prow = pl.BlockSpec((TM, D), lambda i: (jnp.minimum(i, last_p), 0))
    full = lambda a: pl.BlockSpec(a.shape, const)
    return pl.pallas_call(
        functools.partial(_post_kernel, geom),
        grid=(n_tiles,),
        in_specs=[prow, full(xs), prow, full(oas), prow, full(obs),
                  pl.BlockSpec((TM, D), row), pl.BlockSpec((TM, D), row),
                  full(mhi), full(mlo), full(g_ffn), full(wba), full(wbb), full(wo),
                  full(wrh), full(wrl), full(br)],
        out_specs=[pl.BlockSpec((TM, D), row), pl.BlockSpec((TM, D), row),
                   pl.BlockSpec((TM, LANES), row), pl.BlockSpec((TM, LANES), row)],
        out_shape=[jax.ShapeDtypeStruct((n_rows, D), F32), jax.ShapeDtypeStruct((n_rows, D), F32),
                   jax.ShapeDtypeStruct((n_rows, LANES), I32),
                   jax.ShapeDtypeStruct((n_rows, LANES), F32)],
        compiler_params=_cparams(48),
        name="post",
    )(xp, xs, oap, oas, obp, obs, ua, ub, mhi, mlo, g_ffn, wba, wbb, wo, wrh, wrl, br)


def _multi_hot(te):
    lane = lax.broadcasted_iota(I32, (TM, LANES), 1)
    m = jnp.zeros((TM, LANES), F32)
    for k in range(TOP_K):
        m = m + (lane == te[:, k:k + 1]).astype(F32)
    return m


def _count_kernel(te_ref, cnt_ref):
    @pl.when(pl.program_id(0) == 0)
    def _():
        cnt_ref[...] = jnp.zeros_like(cnt_ref)

    cnt_ref[...] += jnp.sum(_multi_hot(te_ref[...]), axis=0, keepdims=True)


def _count_call(te, n_rows):
    return pl.pallas_call(
        _count_kernel,
        grid=(n_rows // TM,),
        in_specs=[pl.BlockSpec((TM, LANES), lambda i: (i, 0))],
        out_specs=pl.BlockSpec((1, LANES), lambda i: (0, 0)),
        out_shape=jax.ShapeDtypeStruct((1, LANES), F32),
        compiler_params=_cparams(32),
        name="count",
    )(te)


def _dest_kernel(te_ref, cnt_ref, dest_ref, base_ref):
    lane8 = lax.broadcasted_iota(I32, (8, LANES), 1)

    @pl.when(pl.program_id(0) == 0)
    def _():
        cnt = jnp.broadcast_to(cnt_ref[...], (8, LANES))
        padded = jnp.floor((cnt + (BM - 1)) * (1.0 / BM)) * BM
        inc = padded
        for s in (1, 2, 4, 8, 16, 32, 64):
            inc = inc + jnp.where(lane8 >= s, pltpu.roll(inc, s, 1), 0.0)
        base_ref[...] = inc - padded

    te = te_ref[...]
    m = _multi_hot(te)
    ri = lax.broadcasted_iota(I32, (TM, TM), 0)
    ci = lax.broadcasted_iota(I32, (TM, TM), 1)
    rank = _dot((ri > ci).astype(BF16), m.astype(BF16)) + base_ref[0:1, :]
    lane = lax.broadcasted_iota(I32, (TM, LANES), 1)
    dest = jnp.zeros((TM, LANES), F32)
    for k in range(TOP_K):
        d = jnp.sum(jnp.where(lane == te[:, k:k + 1], rank, 0.0), axis=-1, keepdims=True)
        dest = jnp.where(lane == k, d, dest)
    dest_ref[...] = dest.astype(I32)
    base_ref[...] += jnp.sum(m, axis=0, keepdims=True)


def _dest_call(te, cnt, n_rows):
    return pl.pallas_call(
        _dest_kernel,
        grid=(n_rows // TM,),
        in_specs=[pl.BlockSpec((TM, LANES), lambda i: (i, 0)),
                  pl.BlockSpec((1, LANES), lambda i: (0, 0))],
        out_specs=pl.BlockSpec((TM, LANES), lambda i: (i, 0)),
        out_shape=jax.ShapeDtypeStruct((n_rows, LANES), I32),
        scratch_shapes=[pltpu.VMEM((8, LANES), F32)],
        compiler_params=_cparams(32),
        name="dest",
    )(te, cnt)


def _dispatch_kernel(dest_ref, h2_ref, xs_in_ref, xs_ref, sem):
    del xs_in_ref
    i = pl.program_id(0)

    def copy(p):
        return pltpu.make_async_copy(h2_ref.at[pl.ds(i * TM + p // TOP_K, 1)],
                                     xs_ref.at[pl.ds(dest_ref[0, 0, p], 1)], sem)

    def start(p, c):
        copy(p).start()
        return c

    def wait(p, c):
        copy(p).wait()
        return c

    lax.fori_loop(0, TM * TOP_K, start, 0)
    lax.fori_loop(0, TM * TOP_K, wait, 0)


def _dispatch_call(dest3, h2, xs0):
    n_tiles = dest3.shape[0]
    return pl.pallas_call(
        _dispatch_kernel,
        grid=(n_tiles,),
        in_specs=[pl.BlockSpec((1, 1, TM * TOP_K), lambda i: (i, 0, 0), memory_space=pltpu.SMEM),
                  pl.BlockSpec(memory_space=pl.ANY), pl.BlockSpec(memory_space=pl.ANY)],
        out_specs=pl.BlockSpec(memory_space=pl.ANY),
        out_shape=jax.ShapeDtypeStruct(xs0.shape, xs0.dtype),
        scratch_shapes=[pltpu.SemaphoreType.DMA],
        input_output_aliases={2: 0},
        compiler_params=_cparams(32),
        name="dispatch",
    )(dest3, h2, xs0)


def _expert_kernel(be_ref, nv_ref, x_ref, wgu_ref, bgu_ref, wd_ref, bd_ref, o_ref, wgu_s, wd_s):
    j = pl.program_id(0)
    jc = jnp.minimum(j, nv_ref[0] - 1)
    prev = be_ref[jnp.maximum(jc - 1, 0)]

    @pl.when((j == 0) | ((j < nv_ref[0]) & (be_ref[jc] != prev)))
    def _():
        wgu_s[...] = wgu_ref[0].astype(BF16)
        wd_s[...] = wd_ref[0].astype(BF16)

    @pl.when(j < nv_ref[0])
    def _():
        gu = _dot(x_ref[...].astype(BF16), wgu_s[...]) + bgu_ref[0]
        gate = jnp.minimum(gu[:, :D], SWIGLU_LIMIT)
        up = jnp.clip(gu[:, D:], -SWIGLU_LIMIT, SWIGLU_LIMIT)
        act = (up + 1.0) * (gate * _sigmoid(SWIGLU_ALPHA * gate))
        o_ref[...] = _dot(act.astype(BF16), wd_s[...]) + bd_ref[0]

    @pl.when(j >= nv_ref[0])
    def _():
        o_ref[...] = jnp.zeros_like(o_ref)


def _expert_call(block_e, n_valid, xs, wgu, bgu, wd, bd):
    n_blocks = xs.shape[0] // BM
    blk = lambda j, be, nv: (jnp.minimum(j, nv[0] - 1), 0)
    blk_out = lambda j, be, nv: (j, 0)
    exp = lambda j, be, nv: (be[jnp.minimum(j, nv[0] - 1)], 0, 0)
    return pl.pallas_call(
        _expert_kernel,
        grid_spec=pltpu.PrefetchScalarGridSpec(
            num_scalar_prefetch=2,
            grid=(n_blocks,),
            in_specs=[pl.BlockSpec((BM, D), blk),
                      pl.BlockSpec((1, D, 2 * D), exp), pl.BlockSpec((1, 1, 2 * D), exp),
                      pl.BlockSpec((1, D, D), exp), pl.BlockSpec((1, 1, D), exp)],
            out_specs=pl.BlockSpec((BM, D), blk_out),
            scratch_shapes=[pltpu.VMEM((D, 2 * D), BF16), pltpu.VMEM((D, D), BF16)]),
        out_shape=jax.ShapeDtypeStruct(xs.shape, F32),
        compiler_params=_cparams(56),
        name="expert",
    )(block_e, n_valid, xs, wgu, bgu, wd, bd)


def _combine_kernel(geom, dest_ref, eo_ref, x1_ref, tw_ref, mhi_ref, mlo_ref, gfin_ref,
                    yp_ref, ys_ref, buf, sem):
    i = pl.program_id(0)
    n_prompt_tiles = geom[0] // TM

    def copy(p):
        return pltpu.make_async_copy(eo_ref.at[pl.ds(dest_ref[0, 0, p], 1)],
                                     buf.at[p % TOP_K, pl.ds(p // TOP_K, 1)], sem)

    def start(p, c):
        copy(p).start()
        return c

    def wait(p, c):
        copy(p).wait()
        return c

    lax.fori_loop(0, TM * TOP_K, start, 0)
    lax.fori_loop(0, TM * TOP_K, wait, 0)
    gt2 = _row_mod(i, *geom, mhi_ref[...], mlo_ref[...])
    tw = tw_ref[...]
    y = tw[:, 0:1] * buf[0]
    for k in range(1, TOP_K):
        y = y + tw[:, k:k + 1] * buf[k]
    x2 = x1_ref[...] + gt2 * y
    ms = jnp.mean(x2 * x2, axis=-1, keepdims=True)
    out = x2 * lax.rsqrt(ms + EPS) * gfin_ref[...]

    @pl.when(i < n_prompt_tiles)
    def _():
        yp_ref[...] = out

    @pl.when(i >= n_prompt_tiles)
    def _():
        ys_ref[...] = out


def _combine_call(geom, n_rows, dest3, eo, x1, tw, mhi, mlo, g_final):
    n_tiles = n_rows // TM
    n_prompt = geom[0]
    last_p = n_prompt // TM - 1
    const = lambda i: (0, 0)
    row = lambda i: (i, 0)
    return pl.pallas_call(
        functools.partial(_combine_kernel, geom),
        grid=(n_tiles,),
        in_specs=[pl.BlockSpec((1, 1, TM * TOP_K), lambda i: (i, 0, 0), memory_space=pltpu.SMEM),
                  pl.BlockSpec(memory_space=pl.ANY),
                  pl.BlockSpec((TM, D), row), pl.BlockSpec((TM, LANES), row),
                  pl.BlockSpec(mhi.shape, const), pl.BlockSpec(mlo.shape, const),
                  pl.BlockSpec((1, D), const)],
        out_specs=[pl.BlockSpec((TM, D), lambda i: (jnp.minimum(i, last_p), 0)),
                   pl.BlockSpec((n_rows - n_prompt, D), const)],
        out_shape=[jax.ShapeDtypeStruct((n_prompt, D), F32),
                   jax.ShapeDtypeStruct((n_rows - n_prompt, D), F32)],
        scratch_shapes=[pltpu.VMEM((TOP_K, TM, D), F32), pltpu.SemaphoreType.DMA],
        compiler_params=_cparams(40),
        name="combine",
    )(dest3, eo, x1, tw, mhi, mlo, g_final)


def kernel(x_prompt, x_sample, c_prompt, c_sample, state_gla, state_hgrn, w_ada, b_ada, g_norm_mix,
           g_norm_ffn, w_in, w_gla_gate2, b_gla_gate, g_gla_onorm, hgrn_lb_logits, g_hgrn_onorm,
           w_branch_a, w_branch_b, w_out, w_router, b_router, w_gate_up, b_gate_up, w_down, b_down,
           g_final):
    assert w_ada.shape[0] == 1, "single-layer trunk only"
    bp, lp, _ = x_prompt.shape
    bs, ls, _ = x_sample.shape
    n_p, n_s = bp * lp, bs * ls
    n = n_p + n_s
    assert n_p % TM == 0 and n_s == TM and lp % (2 * CHUNK) == 0 and ls <= CHUNK and bp + bs <= 32
    geom = (n_p, lp, ls, bp)
    xp = x_prompt.reshape(n_p, D)
    xs = x_sample.reshape(n_s, D)

    c_all = jnp.zeros((32, D), F32).at[:bp].set(c_prompt).at[bp:bp + bs].set(c_sample)
    mod = _mod_call(c_all, w_ada[0], b_ada[0].reshape(1, 6 * D))
    sh1, sc1, gt1, sh2, sc2, gt2 = [mod[:, j * D:(j + 1) * D] for j in range(6)]
    m1_hi, m1_lo = _split_bf16(jnp.concatenate([sh1, sc1], axis=1))
    m2_hi, m2_lo = _split_bf16(jnp.concatenate([gt1, sh2, sc2], axis=1))
    m3_hi, m3_lo = _split_bf16(gt2)

    wi = w_in[0]
    wm = jnp.concatenate([wi[:, :3072], wi[:, 3072 + GATE_RANK:]], axis=1).astype(BF16)
    wlr = jnp.pad(wi[:, 3072:3072 + GATE_RANK], ((0, 0), (0, LANES - GATE_RANK))).astype(BF16)
    w2 = jnp.pad(w_gla_gate2[0], ((0, LANES - GATE_RANK), (0, 0))).astype(BF16)
    arrs = _proj_call(geom, n, xp, xs, m1_hi, m1_lo, g_norm_mix[0].reshape(1, D), hgrn_lb_logits,
                      wm, wlr, w2, b_gla_gate[0].reshape(1, -1))
    qa, ka, va, ra, ga, qb, kb, gb, ib, rb, ua, ub = arrs
    rec_in = (qa, ka, va, ra, ga, qb, kb, gb, ib, rb)

    gna = g_gla_onorm[0].reshape(1, GLA_DV)
    gnb = g_hgrn_onorm[0].reshape(1, HG_DV)
    oap, obp, sa_p, sb_p = _rec_call(rec_in, gna, gnb, bp, lp, 2 * CHUNK, 0)
    oas, obs, sa_s, sb_s = _rec_call(rec_in, gna, gnb, bs, ls, ls, n_p,
                                     s0=(state_gla[0], state_hgrn[0]))

    wr = jnp.pad(w_router[0], ((0, 0), (0, LANES - N_EXP)))
    wr_hi, wr_lo = _split_bf16(wr)
    br = jnp.pad(b_router[0], (0, LANES - N_EXP), constant_values=NEG).reshape(1, LANES)
    x1, h2, te, tw = _post_call(geom, n, xp, xs, oap, oas, obp, obs, ua, ub, m2_hi, m2_lo,
                                g_norm_ffn[0].reshape(1, D), w_branch_a[0].astype(BF16),
                                w_branch_b[0].astype(BF16), w_out[0].astype(BF16), wr_hi, wr_lo, br)

    cnt = _count_call(te, n)
    dest = _dest_call(te, cnt, n)
    dest3 = dest[:, :TOP_K].reshape(n // TM, 1, TM * TOP_K)
    cap = (n * TOP_K + N_EXP * (BM - 1) + BM - 1) // BM * BM
    xs_sorted = _dispatch_call(dest3, h2, jnp.zeros((cap, D), F32))

    counts = cnt[0, :N_EXP].astype(I32)
    pend = jnp.cumsum((counts + BM - 1) // BM * BM)
    block_e = jnp.clip(jnp.searchsorted(pend, jnp.arange(cap // BM, dtype=I32) * BM, side='right'),
                       0, N_EXP - 1).astype(I32)
    n_valid = (pend[-1:] // BM).astype(I32)
    eo = _expert_call(block_e, n_valid, xs_sorted, w_gate_up[0], b_gate_up[0].reshape(N_EXP, 1, -1),
                      w_down[0], b_down[0].reshape(N_EXP, 1, -1))

    yp, ys = _combine_call(geom, n, dest3, eo, x1, tw, m3_hi, m3_lo, g_final.reshape(1, D))
    return (yp.reshape(bp, lp, D), ys.reshape(bs, ls, D),
            sa_p[None], sb_p[None], sa_s[None], sb_s[None])
```

```python
import functools

import jax
import jax.numpy as jnp
from jax import lax
from jax.experimental import pallas as pl
from jax.experimental.pallas import tpu as pltpu

F32 = jnp.float32
BF16 = jnp.bfloat16
I32 = jnp.int32

EPS = 1e-6
D = 1024
GLA_H, GLA_DK, GLA_DV = 4, 128, 256
HG_H, HG_DK, HG_DV = 8, 128, 128
GATE_RANK = 16
GATE_NORM = 16.0
N_EXP = 32
TOP_K = 4
SWIGLU_LIMIT = 7.0
SWIGLU_ALPHA = 1.702

LANES = 128
TM = 256
CHUNK = 128
SUB = 32
NSUB = CHUNK // SUB
EXP_CLAMP = 80.0
BM = 256
DISP_ROWS = 1280
WAIT_ROWS = 256
NEG = -1e30

C_QA, C_KA, C_VA, C_RA, C_QB, C_FB, C_IB, C_RB, C_UA, C_UB, C_END = (
    0, 512, 1024, 2048, 3072, 4096, 5120, 6144, 7168, 8192, 9216)


def _sigmoid(x):
    return 1.0 / (1.0 + jnp.exp(-x))


def _dot(a, b):
    return jnp.dot(a, b, preferred_element_type=F32)


def _split_bf16(x):
    hi = x.astype(BF16)
    lo = (x - hi.astype(F32)).astype(BF16)
    return hi, lo


def _log2(n):
    assert n > 0 and n & (n - 1) == 0, "power of two expected"
    return n.bit_length() - 1


def _cparams(vmem_mb):
    return pltpu.CompilerParams(dimension_semantics=("arbitrary",),
                                vmem_limit_bytes=vmem_mb * 1024 * 1024)


def _mod_kernel(c_ref, w_ref, b_ref, o_ref):
    c = c_ref[...]
    s = (c * _sigmoid(c)).astype(BF16)
    o_ref[...] = _dot(s, w_ref[...].astype(BF16)) + b_ref[...]


def _mod_call(c_all, w_ada, b_ada):
    rows = c_all.shape[0]
    return pl.pallas_call(
        _mod_kernel,
        grid=(6,),
        in_specs=[pl.BlockSpec((rows, D), lambda j: (0, 0)),
                  pl.BlockSpec((D, D), lambda j: (0, j)),
                  pl.BlockSpec((1, D), lambda j: (0, j))],
        out_specs=pl.BlockSpec((rows, D), lambda j: (0, j)),
        out_shape=jax.ShapeDtypeStruct((rows, 6 * D), F32),
        compiler_params=_cparams(32),
        name="mod",
    )(c_all, w_ada, b_ada)


def _row_mod(i, n_prompt_rows, prompt_len, sample_len, n_prompt_seq, mod_hi, mod_lo):
    row = i * TM + lax.broadcasted_iota(I32, (TM, mod_hi.shape[0]), 0)
    seq = jnp.where(row < n_prompt_rows, row >> _log2(prompt_len),
                    n_prompt_seq + ((row - n_prompt_rows) >> _log2(sample_len)))
    onehot = (lax.broadcasted_iota(I32, (TM, mod_hi.shape[0]), 1) == seq).astype(BF16)
    return _dot(onehot, mod_hi) + _dot(onehot, mod_lo)


def _proj_kernel(geom, xp_ref, xs_ref, mhi_ref, mlo_ref, g_ref, lbl_ref, wm_ref, wlr_ref,
                 w2_ref, bg_ref,
                 qa_ref, ka_ref, va_ref, ra_ref, ga_ref, qb_ref, kb_ref, gb_ref, ib_ref,
                 rb_ref, ua_ref, ub_ref):
    i = pl.program_id(0)
    n_prompt_tiles = geom[0] // TM
    x = jnp.where(i < n_prompt_tiles, xp_ref[...], xs_ref[...])
    mod = _row_mod(i, *geom, mhi_ref[...], mlo_ref[...])
    sh, sc = mod[:, :D], mod[:, D:]
    ms = jnp.mean(x * x, axis=-1, keepdims=True)
    h = x * lax.rsqrt(ms + EPS) * g_ref[...]
    hb = (h * (1.0 + sc) + sh).astype(BF16)

    def proj(a, b):
        return _dot(hb, wm_ref[:, a:b])

    qa_ref[...] = (proj(C_QA, C_KA) * GLA_DK ** -0.5).astype(BF16)
    ka_ref[...] = proj(C_KA, C_VA).astype(BF16)
    va_ref[...] = proj(C_VA, C_RA).astype(BF16)
    r = proj(C_RA, C_QB)
    ra_ref[...] = (r * _sigmoid(r)).astype(BF16)
    lr_hi, lr_lo = _split_bf16(_dot(hb, wlr_ref[...]))
    xg = _dot(lr_hi, w2_ref[...]) + _dot(lr_lo, w2_ref[...]) + bg_ref[...]
    ga_ref[...] = (jnp.minimum(xg, 0.0) - jnp.log1p(jnp.exp(-jnp.abs(xg)))) * (1.0 / GATE_NORM)
    q = proj(C_QB, C_FB)
    qb_ref[...] = (q * _sigmoid(q) * HG_DK ** -0.5).astype(BF16)
    lbl = lbl_ref[...]
    e = jnp.exp(lbl - jnp.max(lbl, axis=0, keepdims=True))
    lb = e[0:1, :] / jnp.sum(e, axis=0, keepdims=True)
    fb = proj(C_FB, C_IB)
    kb = (1.0 - lb) * _sigmoid(-fb)
    kb_ref[...] = kb.astype(BF16)
    gb_ref[...] = jnp.log(lb + (1.0 - lb) * _sigmoid(fb))
    ib_ref[...] = proj(C_IB, C_RB).astype(BF16)
    r = proj(C_RB, C_UA)
    rb_ref[...] = (r * _sigmoid(r)).astype(BF16)
    ua_ref[...] = _sigmoid(proj(C_UA, C_UB)).astype(BF16)
    ub_ref[...] = _sigmoid(proj(C_UB, C_END)).astype(BF16)


def _proj_call(geom, n_rows, xp, xs, mhi, mlo, g_mix, lbl, wm, wlr, w2, bg):
    n_tiles = n_rows // TM
    last_p = geom[0] // TM - 1
    const = lambda i: (0, 0)
    row = lambda i: (i, 0)
    widths = [(512, BF16), (512, BF16), (D, BF16), (D, BF16), (512, F32), (D, BF16), (D, BF16),
              (D, F32), (D, BF16), (D, BF16), (D, BF16), (D, BF16)]
    return pl.pallas_call(
        functools.partial(_proj_kernel, geom),
        grid=(n_tiles,),
        in_specs=[pl.BlockSpec((TM, D), lambda i: (jnp.minimum(i, last_p), 0)),
                  pl.BlockSpec(xs.shape, const),
                  pl.BlockSpec(mhi.shape, const), pl.BlockSpec(mlo.shape, const),
                  pl.BlockSpec((1, D), const), pl.BlockSpec(lbl.shape, const),
                  pl.BlockSpec(wm.shape, const, pipeline_mode=pl.Buffered(1)),
                  pl.BlockSpec(wlr.shape, const), pl.BlockSpec(w2.shape, const),
                  pl.BlockSpec(bg.shape, const)],
        out_specs=[pl.BlockSpec((TM, w), row) for w, _ in widths],
        out_shape=[jax.ShapeDtypeStruct((n_rows, w), dt) for w, dt in widths],
        compiler_params=_cparams(56),
        name="proj",
    )(xp, xs, mhi, mlo, g_mix, lbl, wm, wlr, w2, bg)


def _rec_chunk(tri, causal, g, q, k, v, r, n_heads, dk, dv, s_ref, gn, o_ref, r0, rows_out):
    g_hi, g_lo = _split_bf16(g)
    cum = _dot(tri, g_hi) + _dot(tri, g_lo)
    width = cum.shape[1]
    tot = cum[CHUNK - 1:CHUNK, :]
    refs = [jnp.zeros((1, width), F32)] + [cum[j * SUB - 1:j * SUB, :] for j in range(1, NSUB)]
    d = cum - jnp.concatenate([jnp.broadcast_to(b, (SUB, width)) for b in refs], axis=0)
    qn = q.astype(F32) * jnp.exp(d)
    kn = k.astype(F32) * jnp.exp(jnp.minimum(-d, EXP_CLAMP))
    sub = lambda x, j: x[j * SUB:(j + 1) * SUB, :]
    qs = jnp.concatenate([sub(qn, j) * jnp.exp(refs[j]) for j in range(NSUB)],
                         axis=0).astype(BF16)
    kd = jnp.concatenate([sub(kn, j) * jnp.exp(tot - refs[j]) for j in range(NSUB)], axis=0)
    et = jnp.exp(tot)
    qnb = qn.astype(BF16)
    k_seen = []
    for i in range(NSUB):
        parts = [sub(kn, j) * jnp.exp(refs[i] - refs[j]) for j in range(i)] + [sub(kn, i)]
        if i + 1 < NSUB:
            parts.append(jnp.zeros((CHUNK - (i + 1) * SUB, width), F32))
        k_seen.append(jnp.concatenate(parts, axis=0).astype(BF16))
    for h in range(n_heads):
        ks = slice(h * dk, (h + 1) * dk)
        vs = slice(h * dv, (h + 1) * dv)
        a = jnp.concatenate(
            [lax.dot_general(sub(qnb, i)[:, ks], k_seen[i][:, ks], (((1,), (1,)), ((), ())),
                             preferred_element_type=F32) for i in range(NSUB)], axis=0)
        a = jnp.where(causal, a, 0.0).astype(BF16)
        s = s_ref[0, h]
        kd_t = kd[:, ks].T.astype(BF16)
        av = _dot(jnp.concatenate([a, kd_t], axis=0), v[:, vs])
        o = _dot(qs[:, ks], s.astype(BF16)) + av[:CHUNK]
        et_col = jnp.broadcast_to(et[:, ks], (dk, dk)).T
        s_ref[0, h] = s * jnp.tile(et_col, (1, dv // dk)) + av[CHUNK:]
        ms = jnp.mean(o * o, axis=-1, keepdims=True)
        og = o * lax.rsqrt(ms + EPS) * gn * r[:, vs].astype(F32)
        o_ref[r0:r0 + rows_out, vs] = og[:rows_out].astype(BF16)


def _rec_kernel(rows, has_s0, *refs):
    (qa_ref, ka_ref, va_ref, ra_ref, ga_ref, qb_ref, kb_ref, gb_ref, ib_ref, rb_ref,
     gna_ref, gnb_ref) = refs[:12]
    refs = refs[12:]
    if has_s0:
        s0a_ref, s0b_ref = refs[:2]
        refs = refs[2:]
    oga_ref, ogb_ref, sa_ref, sb_ref = refs

    @pl.when(pl.program_id(1) == 0)
    def _():
        if has_s0:
            sa_ref[...] = s0a_ref[...]
            sb_ref[...] = s0b_ref[...]
        else:
            sa_ref[...] = jnp.zeros_like(sa_ref)
            sb_ref[...] = jnp.zeros_like(sb_ref)

    ti = lax.broadcasted_iota(I32, (CHUNK, CHUNK), 0)
    si = lax.broadcasted_iota(I32, (CHUNK, CHUNK), 1)
    causal = ti >= si
    tri = causal.astype(BF16)
    n_chunks = max(rows // CHUNK, 1)
    rows_out = min(rows, CHUNK)

    def load(ref, c):
        if rows >= CHUNK:
            return ref[c * CHUNK:(c + 1) * CHUNK, :]
        x = ref[...]
        return jnp.concatenate([x, jnp.zeros((CHUNK - rows, x.shape[1]), x.dtype)], axis=0)

    for c in range(n_chunks):
        _rec_chunk(tri, causal, load(ga_ref, c), load(qa_ref, c), load(ka_ref, c), load(va_ref, c),
                   load(ra_ref, c), GLA_H, GLA_DK, GLA_DV, sa_ref, gna_ref[...], oga_ref,
                   c * CHUNK, rows_out)
        _rec_chunk(tri, causal, load(gb_ref, c), load(qb_ref, c), load(kb_ref, c), load(ib_ref, c),
                   load(rb_ref, c), HG_H, HG_DK, HG_DV, sb_ref, gnb_ref[...], ogb_ref,
                   c * CHUNK, rows_out)


def _rec_call(arrs, gna, gnb, n_seq, seq_len, rows, row0, s0=None):
    steps = seq_len // rows
    blk0 = row0 // rows
    row_in = lambda b, t: (blk0 + b * steps + t, 0)
    row_out = lambda b, t: (b * steps + t, 0)
    const = lambda b, t: (0, 0)
    st = lambda b, t: (b, 0, 0, 0)
    in_specs = [pl.BlockSpec((rows, a.shape[1]), row_in) for a in arrs]
    in_specs += [pl.BlockSpec(gna.shape, const), pl.BlockSpec(gnb.shape, const)]
    args = list(arrs) + [gna, gnb]
    if s0 is not None:
        in_specs += [pl.BlockSpec((1, GLA_H, GLA_DK, GLA_DV), st),
                     pl.BlockSpec((1, HG_H, HG_DK, HG_DV), st)]
        args += [s0[0], s0[1]]
    n_rows = n_seq * seq_len
    return pl.pallas_call(
        functools.partial(_rec_kernel, rows, s0 is not None),
        grid=(n_seq, steps),
        in_specs=in_specs,
        out_specs=[pl.BlockSpec((rows, D), row_out), pl.BlockSpec((rows, D), row_out),
                   pl.BlockSpec((1, GLA_H, GLA_DK, GLA_DV), st),
                   pl.BlockSpec((1, HG_H, HG_DK, HG_DV), st)],
        out_shape=[jax.ShapeDtypeStruct((n_rows, D), BF16), jax.ShapeDtypeStruct((n_rows, D), BF16),
                   jax.ShapeDtypeStruct((n_seq, GLA_H, GLA_DK, GLA_DV), F32),
                   jax.ShapeDtypeStruct((n_seq, HG_H, HG_DK, HG_DV), F32)],
        compiler_params=pltpu.CompilerParams(dimension_semantics=("arbitrary", "arbitrary"),
                                             vmem_limit_bytes=48 * 1024 * 1024),
        name="rec_s0" if s0 is not None else "rec",
    )(*args)


def _post_kernel(geom, xp_ref, xs_ref, oap_ref, oas_ref, obp_ref, obs_ref, ua_ref, ub_ref,
                 mhi_ref, mlo_ref, gf_ref, wba_ref, wbb_ref, wo_ref, wrh_ref, wrl_ref, br_ref,
                 x1_ref, h2_ref, te_ref, tw_ref):
    i = pl.program_id(0)
    is_prompt = i < geom[0] // TM
    x = jnp.where(is_prompt, xp_ref[...], xs_ref[...])
    oga = jnp.where(is_prompt, oap_ref[...], oas_ref[...])
    ogb = jnp.where(is_prompt, obp_ref[...], obs_ref[...])
    mod = _row_mod(i, *geom, mhi_ref[...], mlo_ref[...])
    gt1, sh2, sc2 = mod[:, :D], mod[:, D:2 * D], mod[:, 2 * D:]
    merged = (ua_ref[...].astype(F32) * _dot(oga, wba_ref[...])
              + ub_ref[...].astype(F32) * _dot(ogb, wbb_ref[...]))
    x1 = x + gt1 * _dot(merged.astype(BF16), wo_ref[...])
    x1_ref[...] = x1
    ms = jnp.mean(x1 * x1, axis=-1, keepdims=True)
    h2 = x1 * lax.rsqrt(ms + EPS) * gf_ref[...] * (1.0 + sc2) + sh2
    h2_ref[...] = h2
    h_hi, h_lo = _split_bf16(h2)
    logit = (_dot(h_hi, wrh_ref[...]) + _dot(h_lo, wrh_ref[...]) + _dot(h_hi, wrl_ref[...])
             + br_ref[...])
    lane = lax.broadcasted_iota(I32, (TM, LANES), 1)
    lane_f = lane.astype(F32)
    vals, idxs = [], []
    for _ in range(TOP_K):
        m = jnp.max(logit, axis=-1, keepdims=True)
        idx = jnp.min(jnp.where(logit == m, lane_f, float(LANES)), axis=-1, keepdims=True)
        vals.append(m)
        idxs.append(idx)
        logit = jnp.where(lane_f == idx, -jnp.inf, logit)
    es = [jnp.exp(v - vals[0]) for v in vals]
    den = es[0] + es[1] + es[2] + es[3]
    te = jnp.zeros((TM, LANES), F32)
    tw = jnp.zeros((TM, LANES), F32)
    for k in range(TOP_K):
        te = jnp.where(lane == k, idxs[k], te)
        tw = jnp.where(lane == k, es[k] / den, tw)
    te_ref[...] = te.astype(I32)
    tw_ref[...] = tw


def _post_call(geom, n_rows, xp, xs, oap, oas, obp, obs, ua, ub, mhi, mlo, g_ffn, wba, wbb, wo,
               wrh, wrl, br):
    n_tiles = n_rows // TM
    last_p = geom[0] // TM - 1
    const = lambda i: (0, 0)
    row = lambda i: (i, 0)
    prow = pl.BlockSpec((TM, D), lambda i: (jnp.minimum(i, last_p), 0))
    full = lambda a: pl.BlockSpec(a.shape, const)
    return pl.pallas_call(
        functools.partial(_post_kernel, geom),
        grid=(n_tiles,),
        in_specs=[prow, full(xs), prow, full(oas), prow, full(obs),
                  pl.BlockSpec((TM, D), row), pl.BlockSpec((TM, D), row),
                  full(mhi), full(mlo), full(g_ffn), full(wba), full(wbb), full(wo),
                  full(wrh), full(wrl), full(br)],
        out_specs=[pl.BlockSpec((TM, D), row), pl.BlockSpec((TM, D), row),
                   pl.BlockSpec((TM, LANES), row), pl.BlockSpec((TM, LANES), row)],
        out_shape=[jax.ShapeDtypeStruct((n_rows, D), F32), jax.ShapeDtypeStruct((n_rows, D), F32),
                   jax.ShapeDtypeStruct((n_rows, LANES), I32),
                   jax.ShapeDtypeStruct((n_rows, LANES), F32)],
        compiler_params=_cparams(48),
        name="post",
    )(xp, xs, oap, oas, obp, obs, ua, ub, mhi, mlo, g_ffn, wba, wbb, wo, wrh, wrl, br)


def _multi_hot(te):
    lane = lax.broadcasted_iota(I32, (TM, LANES), 1)
    m = jnp.zeros((TM, LANES), F32)
    for k in range(TOP_K):
        m = m + (lane == te[:, k:k + 1]).astype(F32)
    return m


def _count_kernel(te_ref, cnt_ref):
    @pl.when(pl.program_id(0) == 0)
    def _():
        cnt_ref[...] = jnp.zeros_like(cnt_ref)

    cnt_ref[...] += jnp.sum(_multi_hot(te_ref[...]), axis=0, keepdims=True)


def _count_call(te, n_rows):
    return pl.pallas_call(
        _count_kernel,
        grid=(n_rows // TM,),
        in_specs=[pl.BlockSpec((TM, LANES), lambda i: (i, 0))],
        out_specs=pl.BlockSpec((1, LANES), lambda i: (0, 0)),
        out_shape=jax.ShapeDtypeStruct((1, LANES), F32),
        compiler_params=_cparams(32),
        name="count",
    )(te)


def _dest_kernel(te_ref, cnt_ref, dest_ref, base_ref):
    lane8 = lax.broadcasted_iota(I32, (8, LANES), 1)

    @pl.when(pl.program_id(0) == 0)
    def _():
        cnt = jnp.broadcast_to(cnt_ref[...], (8, LANES))
        padded = jnp.floor((cnt + (BM - 1)) * (1.0 / BM)) * BM
        inc = padded
        for s in (1, 2, 4, 8, 16, 32, 64):
            inc = inc + jnp.where(lane8 >= s, pltpu.roll(inc, s, 1), 0.0)
        base_ref[...] = inc - padded

    te = te_ref[...]
    m = _multi_hot(te)
    ri = lax.broadcasted_iota(I32, (TM, TM), 0)
    ci = lax.broadcasted_iota(I32, (TM, TM), 1)
    rank = _dot((ri > ci).astype(BF16), m.astype(BF16)) + base_ref[0:1, :]
    lane = lax.broadcasted_iota(I32, (TM, LANES), 1)
    dest = jnp.zeros((TM, LANES), F32)
    for k in range(TOP_K):
        d = jnp.sum(jnp.where(lane == te[:, k:k + 1], rank, 0.0), axis=-1, keepdims=True)
        dest = jnp.where(lane == k, d, dest)
    dest_ref[...] = dest.astype(I32)
    base_ref[...] += jnp.sum(m, axis=0, keepdims=True)


def _dest_call(te, cnt, n_rows):
    return pl.pallas_call(
        _dest_kernel,
        grid=(n_rows // TM,),
        in_specs=[pl.BlockSpec((TM, LANES), lambda i: (i, 0)),
                  pl.BlockSpec((1, LANES), lambda i: (0, 0))],
        out_specs=pl.BlockSpec((TM, LANES), lambda i: (i, 0)),
        out_shape=jax.ShapeDtypeStruct((n_rows, LANES), I32),
        scratch_shapes=[pltpu.VMEM((8, LANES), F32)],
        compiler_params=_cparams(32),
        name="dest",
    )(te, cnt)


def _dispatch_kernel(pend_ref, dest_ref, h2_ref, xs_ref, zeros, sem, zsem):
    @pl.when(pl.program_id(0) == 0)
    def _():
        zeros[...] = jnp.zeros_like(zeros)

        def zero_copy(e):
            start = pl.multiple_of(pend_ref[e] - BM, BM)
            return pltpu.make_async_copy(zeros, xs_ref.at[pl.ds(start, BM)], zsem)

        def nonempty(e):
            return pend_ref[e] > (pend_ref[e - 1] if e > 0 else 0)

        for e in range(N_EXP):
            pl.when(nonempty(e))(lambda e=e: zero_copy(e).start())
        for e in range(N_EXP):
            pl.when(nonempty(e))(lambda e=e: zero_copy(e).wait())

        def tail_copy(j):
            return pltpu.make_async_copy(zeros, xs_ref.at[pl.ds(pl.multiple_of(j * BM, BM), BM)],
                                         zsem)

        first, last = pend_ref[N_EXP - 1] // BM, xs_ref.shape[0] // BM
        lax.fori_loop(first, last, lambda j, c: (tail_copy(j).start(), c)[1], 0)
        lax.fori_loop(first, last, lambda j, c: (tail_copy(j).wait(), c)[1], 0)

    def body(r, c):
        src = h2_ref.at[pl.ds(r, 1)]
        for k in range(TOP_K):
            pltpu.make_async_copy(src, xs_ref.at[pl.ds(dest_ref[0, 0, r * TOP_K + k], 1)],
                                  sem).start()
        return c

    lax.fori_loop(0, DISP_ROWS, body, 0, unroll=4)

    def wait(j, c):
        pltpu.make_async_copy(xs_ref.at[pl.ds(0, WAIT_ROWS)], xs_ref.at[pl.ds(0, WAIT_ROWS)],
                              sem).wait()
        return c

    lax.fori_loop(0, DISP_ROWS * TOP_K // WAIT_ROWS, wait, 0)


def _dispatch_call(pend, dest3, h2, cap):
    n_steps = dest3.shape[0]
    return pl.pallas_call(
        _dispatch_kernel,
        grid_spec=pltpu.PrefetchScalarGridSpec(
            num_scalar_prefetch=1,
            grid=(n_steps,),
            in_specs=[pl.BlockSpec((1, 1, DISP_ROWS * TOP_K), lambda i, pe: (i, 0, 0),
                                   memory_space=pltpu.SMEM),
                      pl.BlockSpec((DISP_ROWS, D), lambda i, pe: (i, 0))],
            out_specs=pl.BlockSpec(memory_space=pl.ANY),
            scratch_shapes=[pltpu.VMEM((BM, D), F32), pltpu.SemaphoreType.DMA,
                            pltpu.SemaphoreType.DMA]),
        out_shape=jax.ShapeDtypeStruct((cap, D), F32),
        compiler_params=_cparams(32),
        name="dispatch",
    )(pend, dest3, h2)


def _expert_kernel(be_ref, nv_ref, x_ref, wgu_ref, bgu_ref, wd_ref, bd_ref, o_ref, wgu_s, wd_s):
    j = pl.program_id(0)
    jc = jnp.minimum(j, nv_ref[0] - 1)
    prev = be_ref[jnp.maximum(jc - 1, 0)]

    @pl.when((j == 0) | ((j < nv_ref[0]) & (be_ref[jc] != prev)))
    def _():
        wgu_s[...] = wgu_ref[0].astype(BF16)
        wd_s[...] = wd_ref[0].astype(BF16)

    @pl.when(j < nv_ref[0])
    def _():
        gu = _dot(x_ref[...].astype(BF16), wgu_s[...]) + bgu_ref[0]
        gate = jnp.minimum(gu[:, :D], SWIGLU_LIMIT)
        up = jnp.clip(gu[:, D:], -SWIGLU_LIMIT, SWIGLU_LIMIT)
        act = (up + 1.0) * (gate * _sigmoid(SWIGLU_ALPHA * gate))
        o_ref[...] = _dot(act.astype(BF16), wd_s[...]) + bd_ref[0]

    @pl.when(j >= nv_ref[0])
    def _():
        o_ref[...] = jnp.zeros_like(o_ref)


def _expert_call(block_e, n_valid, xs, wgu, bgu, wd, bd):
    n_blocks = xs.shape[0] // BM
    blk = lambda j, be, nv: (jnp.minimum(j, nv[0] - 1), 0)
    blk_out = lambda j, be, nv: (j, 0)
    exp = lambda j, be, nv: (be[jnp.minimum(j, nv[0] - 1)], 0, 0)
    return pl.pallas_call(
        _expert_kernel,
        grid_spec=pltpu.PrefetchScalarGridSpec(
            num_scalar_prefetch=2,
            grid=(n_blocks,),
            in_specs=[pl.BlockSpec((BM, D), blk),
                      pl.BlockSpec((1, D, 2 * D), exp), pl.BlockSpec((1, 1, 2 * D), exp),
                      pl.BlockSpec((1, D, D), exp), pl.BlockSpec((1, 1, D), exp)],
            out_specs=pl.BlockSpec((BM, D), blk_out),
            scratch_shapes=[pltpu.VMEM((D, 2 * D), BF16), pltpu.VMEM((D, D), BF16)]),
        out_shape=jax.ShapeDtypeStruct(xs.shape, F32),
        compiler_params=_cparams(56),
        name="expert",
    )(block_e, n_valid, xs, wgu, bgu, wd, bd)


def _combine_kernel(geom, dcur_ref, dnxt_ref, eo_ref, x1_ref, tw_ref, mhi_ref, mlo_ref, gfin_ref,
                    yp_ref, ys_ref, buf, sem):
    i = pl.program_id(0)
    n_prompt_tiles = geom[0] // TM
    slot = lax.rem(i, 2)

    def gather(dest_ref, into):
        def body(r, c):
            for k in range(TOP_K):
                pltpu.make_async_copy(eo_ref.at[pl.ds(dest_ref[0, 0, r * TOP_K + k], 1)],
                                      buf.at[into, k, pl.ds(r, 1)], sem.at[into]).start()
            return c

        lax.fori_loop(0, TM, body, 0, unroll=4)

    @pl.when(i == 0)
    def _():
        gather(dcur_ref, 0)

    @pl.when(i + 1 < pl.num_programs(0))
    def _():
        gather(dnxt_ref, 1 - slot)

    for k in range(TOP_K):
        pltpu.make_async_copy(eo_ref.at[pl.ds(0, TM)], buf.at[slot, k], sem.at[slot]).wait()
    gt2 = _row_mod(i, *geom, mhi_ref[...], mlo_ref[...])
    tw = tw_ref[...]
    y = tw[:, 0:1] * buf[slot, 0]
    for k in range(1, TOP_K):
        y = y + tw[:, k:k + 1] * buf[slot, k]
    x2 = x1_ref[...] + gt2 * y
    ms = jnp.mean(x2 * x2, axis=-1, keepdims=True)
    out = x2 * lax.rsqrt(ms + EPS) * gfin_ref[...]

    @pl.when(i < n_prompt_tiles)
    def _():
        yp_ref[...] = out

    @pl.when(i >= n_prompt_tiles)
    def _():
        ys_ref[...] = out


def _combine_call(geom, n_rows, dest3, eo, x1, tw, mhi, mlo, g_final):
    n_tiles = n_rows // TM
    n_prompt = geom[0]
    last_p = n_prompt // TM - 1
    const = lambda i: (0, 0)
    row = lambda i: (i, 0)
    return pl.pallas_call(
        functools.partial(_combine_kernel, geom),
        grid=(n_tiles,),
        in_specs=[pl.BlockSpec((1, 1, TM * TOP_K), lambda i: (i, 0, 0), memory_space=pltpu.SMEM),
                  pl.BlockSpec((1, 1, TM * TOP_K), lambda i: (jnp.minimum(i + 1, n_tiles - 1), 0, 0),
                               memory_space=pltpu.SMEM),
                  pl.BlockSpec(memory_space=pl.ANY),
                  pl.BlockSpec((TM, D), row), pl.BlockSpec((TM, LANES), row),
                  pl.BlockSpec(mhi.shape, const), pl.BlockSpec(mlo.shape, const),
                  pl.BlockSpec((1, D), const)],
        out_specs=[pl.BlockSpec((TM, D), lambda i: (jnp.minimum(i, last_p), 0)),
                   pl.BlockSpec((n_rows - n_prompt, D), const)],
        out_shape=[jax.ShapeDtypeStruct((n_prompt, D), F32),
                   jax.ShapeDtypeStruct((n_rows - n_prompt, D), F32)],
        scratch_shapes=[pltpu.VMEM((2, TOP_K, TM, D), F32), pltpu.SemaphoreType.DMA((2,))],
        compiler_params=_cparams(40),
        name="combine",
    )(dest3, dest3, eo, x1, tw, mhi, mlo, g_final)


def kernel(x_prompt, x_sample, c_prompt, c_sample, state_gla, state_hgrn, w_ada, b_ada, g_norm_mix,
           g_norm_ffn, w_in, w_gla_gate2, b_gla_gate, g_gla_onorm, hgrn_lb_logits, g_hgrn_onorm,
           w_branch_a, w_branch_b, w_out, w_router, b_router, w_gate_up, b_gate_up, w_down, b_down,
           g_final):
    assert w_ada.shape[0] == 1, "single-layer trunk only"
    bp, lp, _ = x_prompt.shape
    bs, ls, _ = x_sample.shape
    n_p, n_s = bp * lp, bs * ls
    n = n_p + n_s
    assert n_p % TM == 0 and n_s == TM and lp % (2 * CHUNK) == 0 and ls <= CHUNK and bp + bs <= 32
    assert n % DISP_ROWS == 0
    geom = (n_p, lp, ls, bp)
    xp = x_prompt.reshape(n_p, D)
    xs = x_sample.reshape(n_s, D)

    c_all = jnp.zeros((32, D), F32).at[:bp].set(c_prompt).at[bp:bp + bs].set(c_sample)
    mod = _mod_call(c_all, w_ada[0], b_ada[0].reshape(1, 6 * D))
    sh1, sc1, gt1, sh2, sc2, gt2 = [mod[:, j * D:(j + 1) * D] for j in range(6)]
    m1_hi, m1_lo = _split_bf16(jnp.concatenate([sh1, sc1], axis=1))
    m2_hi, m2_lo = _split_bf16(jnp.concatenate([gt1, sh2, sc2], axis=1))
    m3_hi, m3_lo = _split_bf16(gt2)

    wi = w_in[0]
    wm = jnp.concatenate([wi[:, :3072], wi[:, 3072 + GATE_RANK:]], axis=1).astype(BF16)
    wlr = jnp.pad(wi[:, 3072:3072 + GATE_RANK], ((0, 0), (0, LANES - GATE_RANK))).astype(BF16)
    w2 = jnp.pad(w_gla_gate2[0], ((0, LANES - GATE_RANK), (0, 0))).astype(BF16)
    arrs = _proj_call(geom, n, xp, xs, m1_hi, m1_lo, g_norm_mix[0].reshape(1, D), hgrn_lb_logits,
                      wm, wlr, w2, b_gla_gate[0].reshape(1, -1))
    qa, ka, va, ra, ga, qb, kb, gb, ib, rb, ua, ub = arrs
    rec_in = (qa, ka, va, ra, ga, qb, kb, gb, ib, rb)

    gna = g_gla_onorm[0].reshape(1, GLA_DV)
    gnb = g_hgrn_onorm[0].reshape(1, HG_DV)
    oap, obp, sa_p, sb_p = _rec_call(rec_in, gna, gnb, bp, lp, 2 * CHUNK, 0)
    oas, obs, sa_s, sb_s = _rec_call(rec_in, gna, gnb, bs, ls, ls, n_p,
                                     s0=(state_gla[0], state_hgrn[0]))

    wr = jnp.pad(w_router[0], ((0, 0), (0, LANES - N_EXP)))
    wr_hi, wr_lo = _split_bf16(wr)
    br = jnp.pad(b_router[0], (0, LANES - N_EXP), constant_values=NEG).reshape(1, LANES)
    x1, h2, te, tw = _post_call(geom, n, xp, xs, oap, oas, obp, obs, ua, ub, m2_hi, m2_lo,
                                g_norm_ffn[0].reshape(1, D), w_branch_a[0].astype(BF16),
                                w_branch_b[0].astype(BF16), w_out[0].astype(BF16), wr_hi, wr_lo, br)

    cnt = _count_call(te, n)
    dest = _dest_call(te, cnt, n)
    dest4 = dest[:, :TOP_K]
    cap = (n * TOP_K + N_EXP * (BM - 1) + BM - 1) // BM * BM
    counts = cnt[0, :N_EXP].astype(I32)
    pend = jnp.cumsum((counts + BM - 1) // BM * BM).astype(I32)
    block_start = jnp.arange(cap // BM, dtype=I32) * BM
    block_e = jnp.minimum(jnp.sum((pend[None, :] <= block_start[:, None]).astype(I32), axis=1),
                          N_EXP - 1)
    n_valid = pend[-1:] // BM
    xs_sorted = _dispatch_call(pend, dest4.reshape(n // DISP_ROWS, 1, DISP_ROWS * TOP_K), h2, cap)
    dest3 = dest4.reshape(n // TM, 1, TM * TOP_K)
    eo = _expert_call(block_e, n_valid, xs_sorted, w_gate_up[0], b_gate_up[0].reshape(N_EXP, 1, -1),
                      w_down[0], b_down[0].reshape(N_EXP, 1, -1))

    yp, ys = _combine_call(geom, n, dest3, eo, x1, tw, m3_hi, m3_lo, g_final.reshape(1, D))
    return (yp.reshape(bp, lp, D), ys.reshape(bs, ls, D),
            sa_p[None], sb_p[None], sa_s[None], sb_s[None])
```

```python
import functools

import jax
import jax.numpy as jnp
from jax import lax
from jax.experimental import pallas as pl
from jax.experimental.pallas import tpu as pltpu

F32 = jnp.float32
BF16 = jnp.bfloat16
I32 = jnp.int32

EPS = 1e-6
D = 1024
GLA_H, GLA_DK, GLA_DV = 4, 128, 256
HG_H, HG_DK, HG_DV = 8, 128, 128
GATE_RANK = 16
GATE_NORM = 16.0
N_EXP = 32
TOP_K = 4
SWIGLU_LIMIT = 7.0
SWIGLU_ALPHA = 1.702

LANES = 128
TM = 256
CHUNK = 128
SUB = 32
NSUB = CHUNK // SUB
EXP_CLAMP = 80.0
BM = 256
DISP_ROWS = 1280
WAIT_ROWS = 256
NEG = -1e30

C_QA, C_KA, C_VA, C_RA, C_QB, C_FB, C_IB, C_RB, C_UA, C_UB, C_END = (
    0, 512, 1024, 2048, 3072, 4096, 5120, 6144, 7168, 8192, 9216)


def _sigmoid(x):
    return 1.0 / (1.0 + jnp.exp(-x))


def _dot(a, b):
    return jnp.dot(a, b, preferred_element_type=F32)


def _split_bf16(x):
    hi = x.astype(BF16)
    lo = (x - hi.astype(F32)).astype(BF16)
    return hi, lo


def _log2(n):
    assert n > 0 and n & (n - 1) == 0, "power of two expected"
    return n.bit_length() - 1


def _cparams(vmem_mb):
    return pltpu.CompilerParams(dimension_semantics=("arbitrary",),
                                vmem_limit_bytes=vmem_mb * 1024 * 1024)


def _mod_kernel(c_ref, w_ref, b_ref, o_ref):
    c = c_ref[...]
    s = (c * _sigmoid(c)).astype(BF16)
    o_ref[...] = _dot(s, w_ref[...].astype(BF16)) + b_ref[...]


def _mod_call(c_all, w_ada, b_ada):
    rows = c_all.shape[0]
    return pl.pallas_call(
        _mod_kernel,
        grid=(6,),
        in_specs=[pl.BlockSpec((rows, D), lambda j: (0, 0)),
                  pl.BlockSpec((D, D), lambda j: (0, j)),
                  pl.BlockSpec((1, D), lambda j: (0, j))],
        out_specs=pl.BlockSpec((rows, D), lambda j: (0, j)),
        out_shape=jax.ShapeDtypeStruct((rows, 6 * D), F32),
        compiler_params=_cparams(32),
        name="mod",
    )(c_all, w_ada, b_ada)


def _row_mod(i, geom, table_ref, sample_ref):
    n_prompt_rows, prompt_len, _, n_prompt_seq = geom
    seq = jnp.minimum((i * TM) >> _log2(prompt_len), n_prompt_seq - 1)
    return jnp.where(i < n_prompt_rows // TM, table_ref[pl.ds(seq, 1), :], sample_ref[...])


def _proj_kernel(geom, xp_ref, xs_ref, mhi_ref, mlo_ref, g_ref, lbl_ref, wm_ref, wlr_ref,
                 w2_ref, bg_ref,
                 qa_ref, ka_ref, va_ref, ra_ref, ga_ref, qb_ref, kb_ref, gb_ref, ib_ref,
                 rb_ref, ua_ref, ub_ref):
    i = pl.program_id(0)
    n_prompt_tiles = geom[0] // TM
    x = jnp.where(i < n_prompt_tiles, xp_ref[...], xs_ref[...])
    mod = _row_mod(i, geom, mhi_ref, mlo_ref)
    sh, sc = mod[:, :D], mod[:, D:]
    ms = jnp.mean(x * x, axis=-1, keepdims=True)
    h = x * lax.rsqrt(ms + EPS) * g_ref[...]
    hb = (h * (1.0 + sc) + sh).astype(BF16)

    def proj(a, b):
        return _dot(hb, wm_ref[:, a:b])

    qa_ref[...] = (proj(C_QA, C_KA) * GLA_DK ** -0.5).astype(BF16)
    ka_ref[...] = proj(C_KA, C_VA).astype(BF16)
    va_ref[...] = proj(C_VA, C_RA).astype(BF16)
    r = proj(C_RA, C_QB)
    ra_ref[...] = (r * _sigmoid(r)).astype(BF16)
    lr_hi, lr_lo = _split_bf16(_dot(hb, wlr_ref[...]))
    xg = _dot(lr_hi, w2_ref[...]) + _dot(lr_lo, w2_ref[...]) + bg_ref[...]
    ga_ref[...] = (jnp.minimum(xg, 0.0) - jnp.log1p(jnp.exp(-jnp.abs(xg)))) * (1.0 / GATE_NORM)
    q = proj(C_QB, C_FB)
    qb_ref[...] = (q * _sigmoid(q) * HG_DK ** -0.5).astype(BF16)
    lbl = lbl_ref[...]
    e = jnp.exp(lbl - jnp.max(lbl, axis=0, keepdims=True))
    lb = e[0:1, :] / jnp.sum(e, axis=0, keepdims=True)
    fb = proj(C_FB, C_IB)
    kb = (1.0 - lb) * _sigmoid(-fb)
    kb_ref[...] = kb.astype(BF16)
    gb_ref[...] = jnp.log(lb + (1.0 - lb) * _sigmoid(fb))
    ib_ref[...] = proj(C_IB, C_RB).astype(BF16)
    r = proj(C_RB, C_UA)
    rb_ref[...] = (r * _sigmoid(r)).astype(BF16)
    ua_ref[...] = _sigmoid(proj(C_UA, C_UB)).astype(BF16)
    ub_ref[...] = _sigmoid(proj(C_UB, C_END)).astype(BF16)


def _proj_call(geom, n_rows, xp, xs, mhi, mlo, g_mix, lbl, wm, wlr, w2, bg):
    n_tiles = n_rows // TM
    last_p = geom[0] // TM - 1
    const = lambda i: (0, 0)
    row = lambda i: (i, 0)
    widths = [(512, BF16), (512, BF16), (D, BF16), (D, BF16), (512, F32), (D, BF16), (D, BF16),
              (D, F32), (D, BF16), (D, BF16), (D, BF16), (D, BF16)]
    return pl.pallas_call(
        functools.partial(_proj_kernel, geom),
        grid=(n_tiles,),
        in_specs=[pl.BlockSpec((TM, D), lambda i: (jnp.minimum(i, last_p), 0)),
                  pl.BlockSpec(xs.shape, const),
                  pl.BlockSpec(mhi.shape, const), pl.BlockSpec(mlo.shape, const),
                  pl.BlockSpec((1, D), const), pl.BlockSpec(lbl.shape, const),
                  pl.BlockSpec(wm.shape, const, pipeline_mode=pl.Buffered(1)),
                  pl.BlockSpec(wlr.shape, const), pl.BlockSpec(w2.shape, const),
                  pl.BlockSpec(bg.shape, const)],
        out_specs=[pl.BlockSpec((TM, w), row) for w, _ in widths],
        out_shape=[jax.ShapeDtypeStruct((n_rows, w), dt) for w, dt in widths],
        compiler_params=_cparams(56),
        name="proj",
    )(xp, xs, mhi, mlo, g_mix, lbl, wm, wlr, w2, bg)


def _rec_chunk(tri, causal, g, q, k, v, r, n_heads, dk, dv, s_ref, gn, o_ref, r0, rows_out):
    g_hi, g_lo = _split_bf16(g)
    cum = _dot(tri, g_hi) + _dot(tri, g_lo)
    width = cum.shape[1]
    tot = cum[CHUNK - 1:CHUNK, :]
    refs = [jnp.zeros((1, width), F32)] + [cum[j * SUB - 1:j * SUB, :] for j in range(1, NSUB)]
    d = cum - jnp.concatenate([jnp.broadcast_to(b, (SUB, width)) for b in refs], axis=0)
    qn = q.astype(F32) * jnp.exp(d)
    kn = k.astype(F32) * jnp.exp(jnp.minimum(-d, EXP_CLAMP))
    sub = lambda x, j: x[j * SUB:(j + 1) * SUB, :]
    qs = jnp.concatenate([sub(qn, j) * jnp.exp(refs[j]) for j in range(NSUB)],
                         axis=0).astype(BF16)
    kd = jnp.concatenate([sub(kn, j) * jnp.exp(tot - refs[j]) for j in range(NSUB)], axis=0)
    et = jnp.exp(tot)
    qnb = qn.astype(BF16)
    k_seen = []
    for i in range(NSUB):
        parts = [sub(kn, j) * jnp.exp(refs[i] - refs[j]) for j in range(i)] + [sub(kn, i)]
        if i + 1 < NSUB:
            parts.append(jnp.zeros((CHUNK - (i + 1) * SUB, width), F32))
        k_seen.append(jnp.concatenate(parts, axis=0).astype(BF16))
    for h in range(n_heads):
        ks = slice(h * dk, (h + 1) * dk)
        vs = slice(h * dv, (h + 1) * dv)
        a = jnp.concatenate(
            [lax.dot_general(sub(qnb, i)[:, ks], k_seen[i][:, ks], (((1,), (1,)), ((), ())),
                             preferred_element_type=F32) for i in range(NSUB)], axis=0)
        a = jnp.where(causal, a, 0.0).astype(BF16)
        s = s_ref[0, h]
        kd_t = kd[:, ks].T.astype(BF16)
        av = _dot(jnp.concatenate([a, kd_t], axis=0), v[:, vs])
        o = _dot(qs[:, ks], s.astype(BF16)) + av[:CHUNK]
        et_col = jnp.broadcast_to(et[:, ks], (dk, dk)).T
        s_ref[0, h] = s * jnp.tile(et_col, (1, dv // dk)) + av[CHUNK:]
        ms = jnp.mean(o * o, axis=-1, keepdims=True)
        og = o * lax.rsqrt(ms + EPS) * gn * r[:, vs].astype(F32)
        o_ref[r0:r0 + rows_out, vs] = og[:rows_out].astype(BF16)


def _rec_kernel(rows, has_s0, *refs):
    (qa_ref, ka_ref, va_ref, ra_ref, ga_ref, qb_ref, kb_ref, gb_ref, ib_ref, rb_ref,
     gna_ref, gnb_ref) = refs[:12]
    refs = refs[12:]
    if has_s0:
        s0a_ref, s0b_ref = refs[:2]
        refs = refs[2:]
    oga_ref, ogb_ref, sa_ref, sb_ref = refs

    @pl.when(pl.program_id(1) == 0)
    def _():
        if has_s0:
            sa_ref[...] = s0a_ref[...]
            sb_ref[...] = s0b_ref[...]
        else:
            sa_ref[...] = jnp.zeros_like(sa_ref)
            sb_ref[...] = jnp.zeros_like(sb_ref)

    ti = lax.broadcasted_iota(I32, (CHUNK, CHUNK), 0)
    si = lax.broadcasted_iota(I32, (CHUNK, CHUNK), 1)
    causal = ti >= si
    tri = causal.astype(BF16)
    n_chunks = max(rows // CHUNK, 1)
    rows_out = min(rows, CHUNK)

    def load(ref, c):
        if rows >= CHUNK:
            return ref[c * CHUNK:(c + 1) * CHUNK, :]
        x = ref[...]
        return jnp.concatenate([x, jnp.zeros((CHUNK - rows, x.shape[1]), x.dtype)], axis=0)

    for c in range(n_chunks):
        _rec_chunk(tri, causal, load(ga_ref, c), load(qa_ref, c), load(ka_ref, c), load(va_ref, c),
                   load(ra_ref, c), GLA_H, GLA_DK, GLA_DV, sa_ref, gna_ref[...], oga_ref,
                   c * CHUNK, rows_out)
        _rec_chunk(tri, causal, load(gb_ref, c), load(qb_ref, c), load(kb_ref, c), load(ib_ref, c),
                   load(rb_ref, c), HG_H, HG_DK, HG_DV, sb_ref, gnb_ref[...], ogb_ref,
                   c * CHUNK, rows_out)


def _rec_call(arrs, gna, gnb, n_seq, seq_len, rows, row0, s0=None):
    steps = seq_len // rows
    blk0 = row0 // rows
    row_in = lambda b, t: (blk0 + b * steps + t, 0)
    row_out = lambda b, t: (b * steps + t, 0)
    const = lambda b, t: (0, 0)
    st = lambda b, t: (b, 0, 0, 0)
    in_specs = [pl.BlockSpec((rows, a.shape[1]), row_in) for a in arrs]
    in_specs += [pl.BlockSpec(gna.shape, const), pl.BlockSpec(gnb.shape, const)]
    args = list(arrs) + [gna, gnb]
    if s0 is not None:
        in_specs += [pl.BlockSpec((1, GLA_H, GLA_DK, GLA_DV), st),
                     pl.BlockSpec((1, HG_H, HG_DK, HG_DV), st)]
        args += [s0[0], s0[1]]
    n_rows = n_seq * seq_len
    return pl.pallas_call(
        functools.partial(_rec_kernel, rows, s0 is not None),
        grid=(n_seq, steps),
        in_specs=in_specs,
        out_specs=[pl.BlockSpec((rows, D), row_out), pl.BlockSpec((rows, D), row_out),
                   pl.BlockSpec((1, GLA_H, GLA_DK, GLA_DV), st),
                   pl.BlockSpec((1, HG_H, HG_DK, HG_DV), st)],
        out_shape=[jax.ShapeDtypeStruct((n_rows, D), BF16), jax.ShapeDtypeStruct((n_rows, D), BF16),
                   jax.ShapeDtypeStruct((n_seq, GLA_H, GLA_DK, GLA_DV), F32),
                   jax.ShapeDtypeStruct((n_seq, HG_H, HG_DK, HG_DV), F32)],
        compiler_params=pltpu.CompilerParams(dimension_semantics=("arbitrary", "arbitrary"),
                                             vmem_limit_bytes=48 * 1024 * 1024),
        name="rec_s0" if s0 is not None else "rec",
    )(*args)


def _post_kernel(geom, xp_ref, xs_ref, oap_ref, oas_ref, obp_ref, obs_ref, ua_ref, ub_ref,
                 mhi_ref, mlo_ref, gf_ref, wba_ref, wbb_ref, wo_ref, wrh_ref, wrl_ref, br_ref,
                 x1_ref, h2_ref, te_ref, tw_ref):
    i = pl.program_id(0)
    is_prompt = i < geom[0] // TM
    x = jnp.where(is_prompt, xp_ref[...], xs_ref[...])
    oga = jnp.where(is_prompt, oap_ref[...], oas_ref[...])
    ogb = jnp.where(is_prompt, obp_ref[...], obs_ref[...])
    mod = _row_mod(i, geom, mhi_ref, mlo_ref)
    gt1, sh2, sc2 = mod[:, :D], mod[:, D:2 * D], mod[:, 2 * D:]
    merged = (ua_ref[...].astype(F32) * _dot(oga, wba_ref[...])
              + ub_ref[...].astype(F32) * _dot(ogb, wbb_ref[...]))
    x1 = x + gt1 * _dot(merged.astype(BF16), wo_ref[...])
    x1_ref[...] = x1
    ms = jnp.mean(x1 * x1, axis=-1, keepdims=True)
    h2 = x1 * lax.rsqrt(ms + EPS) * gf_ref[...] * (1.0 + sc2) + sh2
    h2_ref[...] = h2
    h_hi, h_lo = _split_bf16(h2)
    logit = (_dot(h_hi, wrh_ref[...]) + _dot(h_lo, wrh_ref[...]) + _dot(h_hi, wrl_ref[...])
             + br_ref[...])
    lane = lax.broadcasted_iota(I32, (TM, LANES), 1)
    lane_f = lane.astype(F32)
    vals, idxs = [], []
    for _ in range(TOP_K):
        m = jnp.max(logit, axis=-1, keepdims=True)
        idx = jnp.min(jnp.where(logit == m, lane_f, float(LANES)), axis=-1, keepdims=True)
        vals.append(m)
        idxs.append(idx)
        logit = jnp.where(lane_f == idx, -jnp.inf, logit)
    es = [jnp.exp(v - vals[0]) for v in vals]
    den = es[0] + es[1] + es[2] + es[3]
    te = jnp.zeros((TM, LANES), F32)
    tw = jnp.zeros((TM, LANES), F32)
    for k in range(TOP_K):
        te = jnp.where(lane == k, idxs[k], te)
        tw = jnp.where(lane == k, es[k] / den, tw)
    te_ref[...] = te.astype(I32)
    tw_ref[...] = tw


def _post_call(geom, n_rows, xp, xs, oap, oas, obp, obs, ua, ub, mhi, mlo, g_ffn, wba, wbb, wo,
               wrh, wrl, br):
    n_tiles = n_rows // TM
    last_p = geom[0] // TM - 1
    const = lambda i: (0, 0)
    row = lambda i: (i, 0)
    prow = pl.BlockSpec((TM, D), lambda i: (jnp.minimum(i, last_p), 0))
    full = lambda a: pl.BlockSpec(a.shape, const)
    return pl.pallas_call(
        functools.partial(_post_kernel, geom),
        grid=(n_tiles,),
        in_specs=[prow, full(xs), prow, full(oas), prow, full(obs),
                  pl.BlockSpec((TM, D), row), pl.BlockSpec((TM, D), row),
                  full(mhi), full(mlo), full(g_ffn), full(wba), full(wbb), full(wo),
                  full(wrh), full(wrl), full(br)],
        out_specs=[pl.BlockSpec((TM, D), row), pl.BlockSpec((TM, D), row),
                   pl.BlockSpec((TM, LANES), row), pl.BlockSpec((TM, LANES), row)],
        out_shape=[jax.ShapeDtypeStruct((n_rows, D), F32), jax.ShapeDtypeStruct((n_rows, D), F32),
                   jax.ShapeDtypeStruct((n_rows, LANES), I32),
                   jax.ShapeDtypeStruct((n_rows, LANES), F32)],
        compiler_params=_cparams(48),
        name="post",
    )(xp, xs, oap, oas, obp, obs, ua, ub, mhi, mlo, g_ffn, wba, wbb, wo, wrh, wrl, br)


def _multi_hot(te):
    lane = lax.broadcasted_iota(I32, (TM, LANES), 1)
    m = jnp.zeros((TM, LANES), F32)
    for k in range(TOP_K):
        m = m + (lane == te[:, k:k + 1]).astype(F32)
    return m


def _count_kernel(te_ref, cnt_ref):
    @pl.when(pl.program_id(0) == 0)
    def _():
        cnt_ref[...] = jnp.zeros_like(cnt_ref)

    cnt_ref[...] += jnp.sum(_multi_hot(te_ref[...]), axis=0, keepdims=True)


def _count_call(te, n_rows):
    return pl.pallas_call(
        _count_kernel,
        grid=(n_rows // TM,),
        in_specs=[pl.BlockSpec((TM, LANES), lambda i: (i, 0))],
        out_specs=pl.BlockSpec((1, LANES), lambda i: (0, 0)),
        out_shape=jax.ShapeDtypeStruct((1, LANES), F32),
        compiler_params=_cparams(32),
        name="count",
    )(te)


def _dest_kernel(te_ref, cnt_ref, dest_ref, base_ref):
    lane8 = lax.broadcasted_iota(I32, (8, LANES), 1)

    @pl.when(pl.program_id(0) == 0)
    def _():
        cnt = jnp.broadcast_to(cnt_ref[...], (8, LANES))
        padded = jnp.floor((cnt + (BM - 1)) * (1.0 / BM)) * BM
        inc = padded
        for s in (1, 2, 4, 8, 16, 32, 64):
            inc = inc + jnp.where(lane8 >= s, pltpu.roll(inc, s, 1), 0.0)
        base_ref[...] = inc - padded

    te = te_ref[...]
    m = _multi_hot(te)
    ri = lax.broadcasted_iota(I32, (TM, TM), 0)
    ci = lax.broadcasted_iota(I32, (TM, TM), 1)
    rank = _dot((ri > ci).astype(BF16), m.astype(BF16)) + base_ref[0:1, :]
    lane = lax.broadcasted_iota(I32, (TM, LANES), 1)
    dest = jnp.zeros((TM, LANES), F32)
    for k in range(TOP_K):
        d = jnp.sum(jnp.where(lane == te[:, k:k + 1], rank, 0.0), axis=-1, keepdims=True)
        dest = jnp.where(lane == k, d, dest)
    dest_ref[...] = dest.astype(I32)
    base_ref[...] += jnp.sum(m, axis=0, keepdims=True)


def _dest_call(te, cnt, n_rows):
    return pl.pallas_call(
        _dest_kernel,
        grid=(n_rows // TM,),
        in_specs=[pl.BlockSpec((TM, LANES), lambda i: (i, 0)),
                  pl.BlockSpec((1, LANES), lambda i: (0, 0))],
        out_specs=pl.BlockSpec((TM, LANES), lambda i: (i, 0)),
        out_shape=jax.ShapeDtypeStruct((n_rows, LANES), I32),
        scratch_shapes=[pltpu.VMEM((8, LANES), F32)],
        compiler_params=_cparams(32),
        name="dest",
    )(te, cnt)


def _dispatch_kernel(pend_ref, dest_ref, h2_ref, xs_ref, zeros, sem, zsem):
    @pl.when(pl.program_id(0) == 0)
    def _():
        zeros[...] = jnp.zeros_like(zeros)

        def zero_copy(e):
            start = pl.multiple_of(pend_ref[e] - BM, BM)
            return pltpu.make_async_copy(zeros, xs_ref.at[pl.ds(start, BM)], zsem)

        def nonempty(e):
            return pend_ref[e] > (pend_ref[e - 1] if e > 0 else 0)

        for e in range(N_EXP):
            pl.when(nonempty(e))(lambda e=e: zero_copy(e).start())
        for e in range(N_EXP):
            pl.when(nonempty(e))(lambda e=e: zero_copy(e).wait())

        def tail_copy(j):
            return pltpu.make_async_copy(zeros, xs_ref.at[pl.ds(pl.multiple_of(j * BM, BM), BM)],
                                         zsem)

        first, last = pend_ref[N_EXP - 1] // BM, xs_ref.shape[0] // BM
        lax.fori_loop(first, last, lambda j, c: (tail_copy(j).start(), c)[1], 0)
        lax.fori_loop(first, last, lambda j, c: (tail_copy(j).wait(), c)[1], 0)

    def body(r, c):
        src = h2_ref.at[pl.ds(r, 1)]
        for k in range(TOP_K):
            pltpu.make_async_copy(src, xs_ref.at[pl.ds(dest_ref[0, 0, r * TOP_K + k], 1)],
                                  sem).start(priority=k % 2)
        return c

    lax.fori_loop(0, DISP_ROWS, body, 0, unroll=4)

    def wait(j, c):
        pltpu.make_async_copy(xs_ref.at[pl.ds(0, WAIT_ROWS)], xs_ref.at[pl.ds(0, WAIT_ROWS)],
                              sem).wait()
        return c

    lax.fori_loop(0, DISP_ROWS * TOP_K // WAIT_ROWS, wait, 0)


def _dispatch_call(pend, dest3, h2, cap):
    n_steps = dest3.shape[0]
    return pl.pallas_call(
        _dispatch_kernel,
        grid_spec=pltpu.PrefetchScalarGridSpec(
            num_scalar_prefetch=1,
            grid=(n_steps,),
            in_specs=[pl.BlockSpec((1, 1, DISP_ROWS * TOP_K), lambda i, pe: (i, 0, 0),
                                   memory_space=pltpu.SMEM),
                      pl.BlockSpec((DISP_ROWS, D), lambda i, pe: (i, 0))],
            out_specs=pl.BlockSpec(memory_space=pl.ANY),
            scratch_shapes=[pltpu.VMEM((BM, D), F32), pltpu.SemaphoreType.DMA,
                            pltpu.SemaphoreType.DMA]),
        out_shape=jax.ShapeDtypeStruct((cap, D), F32),
        compiler_params=_cparams(32),
        name="dispatch",
    )(pend, dest3, h2)


def _expert_kernel(be_ref, nv_ref, seg_ref, nxt_ref, x_ref, wgu_hbm, bgu_ref, wd_hbm, bd_ref, o_ref,
                   wgu_f, wd_f, wgu_s, wd_s, sem):
    j = pl.program_id(0)
    jc = jnp.minimum(j, nv_ref[0] - 1)
    e = be_ref[jc]
    slot = lax.rem(seg_ref[jc], 2)
    first = (j == 0) | ((j < nv_ref[0]) & (e != be_ref[jnp.maximum(jc - 1, 0)]))

    def fetch(expert, into):
        return (pltpu.make_async_copy(wgu_hbm.at[expert], wgu_f.at[into], sem.at[0, into]),
                pltpu.make_async_copy(wd_hbm.at[expert], wd_f.at[into], sem.at[1, into]))

    @pl.when(j == 0)
    def _():
        for c in fetch(e, slot):
            c.start()

    @pl.when(first)
    def _():
        for c in fetch(e, slot):
            c.wait()

        @pl.when(nxt_ref[jc] >= 0)
        def _():
            for c in fetch(nxt_ref[jc], 1 - slot):
                c.start()

        wgu_s[...] = wgu_f[slot].astype(BF16)
        wd_s[...] = wd_f[slot].astype(BF16)

    @pl.when(j < nv_ref[0])
    def _():
        gu = _dot(x_ref[...].astype(BF16), wgu_s[...]) + bgu_ref[0]
        gate = jnp.minimum(gu[:, :D], SWIGLU_LIMIT)
        up = jnp.clip(gu[:, D:], -SWIGLU_LIMIT, SWIGLU_LIMIT)
        act = (up + 1.0) * (gate * _sigmoid(SWIGLU_ALPHA * gate))
        o_ref[...] = _dot(act.astype(BF16), wd_s[...]) + bd_ref[0]

    @pl.when(j >= nv_ref[0])
    def _():
        o_ref[...] = jnp.zeros_like(o_ref)


def _expert_call(block_e, n_valid, seg, nxt, xs, wgu, bgu, wd, bd):
    n_blocks = xs.shape[0] // BM
    blk = lambda j, be, nv, sg, nx: (jnp.minimum(j, nv[0] - 1), 0)
    blk_out = lambda j, be, nv, sg, nx: (j, 0)
    exp = lambda j, be, nv, sg, nx: (be[jnp.minimum(j, nv[0] - 1)], 0, 0)
    return pl.pallas_call(
        _expert_kernel,
        grid_spec=pltpu.PrefetchScalarGridSpec(
            num_scalar_prefetch=4,
            grid=(n_blocks,),
            in_specs=[pl.BlockSpec((BM, D), blk),
                      pl.BlockSpec(memory_space=pl.ANY), pl.BlockSpec((1, 1, 2 * D), exp),
                      pl.BlockSpec(memory_space=pl.ANY), pl.BlockSpec((1, 1, D), exp)],
            out_specs=pl.BlockSpec((BM, D), blk_out),
            scratch_shapes=[pltpu.VMEM((2, D, 2 * D), F32), pltpu.VMEM((2, D, D), F32),
                            pltpu.VMEM((D, 2 * D), BF16), pltpu.VMEM((D, D), BF16),
                            pltpu.SemaphoreType.DMA((2, 2))]),
        out_shape=jax.ShapeDtypeStruct(xs.shape, F32),
        compiler_params=_cparams(56),
        name="expert",
    )(block_e, n_valid, seg, nxt, xs, wgu, bgu, wd, bd)


def _combine_kernel(geom, dcur_ref, dnxt_ref, eo_ref, x1_ref, tw_ref, mhi_ref, mlo_ref, gfin_ref,
                    yp_ref, ys_ref, buf, sem):
    i = pl.program_id(0)
    n_prompt_tiles = geom[0] // TM
    slot = lax.rem(i, 2)

    def gather(dest_ref, into):
        def body(r, c):
            for k in range(TOP_K):
                pltpu.make_async_copy(eo_ref.at[pl.ds(dest_ref[0, 0, r * TOP_K + k], 1)],
                                      buf.at[into, k, pl.ds(r, 1)],
                                      sem.at[into]).start(priority=k % 2)
            return c

        lax.fori_loop(0, TM, body, 0, unroll=4)

    @pl.when(i == 0)
    def _():
        gather(dcur_ref, 0)

    @pl.when(i + 1 < pl.num_programs(0))
    def _():
        gather(dnxt_ref, 1 - slot)

    for k in range(TOP_K):
        pltpu.make_async_copy(eo_ref.at[pl.ds(0, TM)], buf.at[slot, k], sem.at[slot]).wait()
    gt2 = _row_mod(i, geom, mhi_ref, mlo_ref)
    tw = tw_ref[...]
    y = tw[:, 0:1] * buf[slot, 0]
    for k in range(1, TOP_K):
        y = y + tw[:, k:k + 1] * buf[slot, k]
    x2 = x1_ref[...] + gt2 * y
    ms = jnp.mean(x2 * x2, axis=-1, keepdims=True)
    out = x2 * lax.rsqrt(ms + EPS) * gfin_ref[...]

    @pl.when(i < n_prompt_tiles)
    def _():
        yp_ref[...] = out

    @pl.when(i >= n_prompt_tiles)
    def _():
        ys_ref[...] = out


def _combine_call(geom, n_rows, dest3, eo, x1, tw, mhi, mlo, g_final):
    n_tiles = n_rows // TM
    n_prompt = geom[0]
    last_p = n_prompt // TM - 1
    const = lambda i: (0, 0)
    row = lambda i: (i, 0)
    return pl.pallas_call(
        functools.partial(_combine_kernel, geom),
        grid=(n_tiles,),
        in_specs=[pl.BlockSpec((1, 1, TM * TOP_K), lambda i: (i, 0, 0), memory_space=pltpu.SMEM),
                  pl.BlockSpec((1, 1, TM * TOP_K), lambda i: (jnp.minimum(i + 1, n_tiles - 1), 0, 0),
                               memory_space=pltpu.SMEM),
                  pl.BlockSpec(memory_space=pl.ANY),
                  pl.BlockSpec((TM, D), row), pl.BlockSpec((TM, LANES), row),
                  pl.BlockSpec(mhi.shape, const), pl.BlockSpec(mlo.shape, const),
                  pl.BlockSpec((1, D), const)],
        out_specs=[pl.BlockSpec((TM, D), lambda i: (jnp.minimum(i, last_p), 0)),
                   pl.BlockSpec((n_rows - n_prompt, D), const)],
        out_shape=[jax.ShapeDtypeStruct((n_prompt, D), F32),
                   jax.ShapeDtypeStruct((n_rows - n_prompt, D), F32)],
        scratch_shapes=[pltpu.VMEM((2, TOP_K, TM, D), F32), pltpu.SemaphoreType.DMA((2,))],
        compiler_params=_cparams(40),
        name="combine",
    )(dest3, dest3, eo, x1, tw, mhi, mlo, g_final)


def kernel(x_prompt, x_sample, c_prompt, c_sample, state_gla, state_hgrn, w_ada, b_ada, g_norm_mix,
           g_norm_ffn, w_in, w_gla_gate2, b_gla_gate, g_gla_onorm, hgrn_lb_logits, g_hgrn_onorm,
           w_branch_a, w_branch_b, w_out, w_router, b_router, w_gate_up, b_gate_up, w_down, b_down,
           g_final):
    assert w_ada.shape[0] == 1, "single-layer trunk only"
    bp, lp, _ = x_prompt.shape
    bs, ls, _ = x_sample.shape
    n_p, n_s = bp * lp, bs * ls
    n = n_p + n_s
    assert n_p % TM == 0 and n_s == TM and lp % (2 * CHUNK) == 0 and ls <= CHUNK and bp + bs <= 32
    assert n % DISP_ROWS == 0
    geom = (n_p, lp, ls, bp)
    xp = x_prompt.reshape(n_p, D)
    xs = x_sample.reshape(n_s, D)

    c_all = jnp.zeros((32, D), F32).at[:bp].set(c_prompt).at[bp:bp + bs].set(c_sample)
    mod = _mod_call(c_all, w_ada[0], b_ada[0].reshape(1, 6 * D))
    sh1, sc1, gt1, sh2, sc2, gt2 = [mod[:, j * D:(j + 1) * D] for j in range(6)]
    def table_and_sample_rows(t):
        return t, jnp.repeat(t[bp:bp + bs], ls, axis=0)

    m1_hi, m1_lo = table_and_sample_rows(jnp.concatenate([sh1, sc1], axis=1))
    m2_hi, m2_lo = table_and_sample_rows(jnp.concatenate([gt1, sh2, sc2], axis=1))
    m3_hi, m3_lo = table_and_sample_rows(gt2)

    wi = w_in[0]
    wm = jnp.concatenate([wi[:, :3072], wi[:, 3072 + GATE_RANK:]], axis=1).astype(BF16)
    wlr = jnp.pad(wi[:, 3072:3072 + GATE_RANK], ((0, 0), (0, LANES - GATE_RANK))).astype(BF16)
    w2 = jnp.pad(w_gla_gate2[0], ((0, LANES - GATE_RANK), (0, 0))).astype(BF16)
    arrs = _proj_call(geom, n, xp, xs, m1_hi, m1_lo, g_norm_mix[0].reshape(1, D), hgrn_lb_logits,
                      wm, wlr, w2, b_gla_gate[0].reshape(1, -1))
    qa, ka, va, ra, ga, qb, kb, gb, ib, rb, ua, ub = arrs
    rec_in = (qa, ka, va, ra, ga, qb, kb, gb, ib, rb)

    gna = g_gla_onorm[0].reshape(1, GLA_DV)
    gnb = g_hgrn_onorm[0].reshape(1, HG_DV)
    oap, obp, sa_p, sb_p = _rec_call(rec_in, gna, gnb, bp, lp, 2 * CHUNK, 0)
    oas, obs, sa_s, sb_s = _rec_call(rec_in, gna, gnb, bs, ls, ls, n_p,
                                     s0=(state_gla[0], state_hgrn[0]))

    wr = jnp.pad(w_router[0], ((0, 0), (0, LANES - N_EXP)))
    wr_hi, wr_lo = _split_bf16(wr)
    br = jnp.pad(b_router[0], (0, LANES - N_EXP), constant_values=NEG).reshape(1, LANES)
    x1, h2, te, tw = _post_call(geom, n, xp, xs, oap, oas, obp, obs, ua, ub, m2_hi, m2_lo,
                                g_norm_ffn[0].reshape(1, D), w_branch_a[0].astype(BF16),
                                w_branch_b[0].astype(BF16), w_out[0].astype(BF16), wr_hi, wr_lo, br)

    cnt = _count_call(te, n)
    dest = _dest_call(te, cnt, n)
    dest4 = dest[:, :TOP_K]
    cap = (n * TOP_K + N_EXP * (BM - 1) + BM - 1) // BM * BM
    counts = cnt[0, :N_EXP].astype(I32)
    pend = jnp.cumsum((counts + BM - 1) // BM * BM).astype(I32)
    block_start = jnp.arange(cap // BM, dtype=I32) * BM
    block_e = jnp.minimum(jnp.sum((pend[None, :] <= block_start[:, None]).astype(I32), axis=1),
                          N_EXP - 1)
    n_valid = pend[-1:] // BM
    ids = jnp.arange(N_EXP, dtype=I32)
    nonempty = counts > 0
    later = jnp.where(nonempty[None, :] & (ids[None, :] > ids[:, None]), ids[None, :], N_EXP)
    next_e = jnp.min(later, axis=1)
    next_e = jnp.where(next_e == N_EXP, -1, next_e)
    seg = (jnp.cumsum(nonempty.astype(I32)) - 1)[block_e]
    nxt = next_e[block_e]
    xs_sorted = _dispatch_call(pend, dest4.reshape(n // DISP_ROWS, 1, DISP_ROWS * TOP_K), h2, cap)
    dest3 = dest4.reshape(n // TM, 1, TM * TOP_K)
    eo = _expert_call(block_e, n_valid, seg, nxt, xs_sorted, w_gate_up[0],
                      b_gate_up[0].reshape(N_EXP, 1, -1), w_down[0], b_down[0].reshape(N_EXP, 1, -1))

    yp, ys = _combine_call(geom, n, dest3, eo, x1, tw, m3_hi, m3_lo, g_final.reshape(1, D))
    return (yp.reshape(bp, lp, D), ys.reshape(bs, ls, D),
            sa_p[None], sb_p[None], sa_s[None], sb_s[None])
```

```python
import functools

import jax
import jax.numpy as jnp
from jax import lax
from jax.experimental import pallas as pl
from jax.experimental.pallas import tpu as pltpu

F32 = jnp.float32
BF16 = jnp.bfloat16
I32 = jnp.int32

EPS = 1e-6
D = 1024
GLA_H, GLA_DK, GLA_DV = 4, 128, 256
HG_H, HG_DK, HG_DV = 8, 128, 128
GATE_RANK = 16
GATE_NORM = 16.0
N_EXP = 32
TOP_K = 4
SWIGLU_LIMIT = 7.0
SWIGLU_ALPHA = 1.702

LANES = 128
TM = 256
CHUNK = 128
SUB = 32
NSUB = CHUNK // SUB
EXP_CLAMP = 80.0
BM = 256
RUN = 8
RUN_PIECES = (256, 128, 64, 32, 16, 8)
SORT_ROWS = TM * TOP_K + 256
TAB_ROWS = 3 * N_EXP
NEG = -1e30

C_QA, C_KA, C_VA, C_RA, C_QB, C_FB, C_IB, C_RB, C_UA, C_UB, C_END = (
    0, 512, 1024, 2048, 3072, 4096, 5120, 6144, 7168, 8192, 9216)


def _sigmoid(x):
    return 1.0 / (1.0 + jnp.exp(-x))


def _dot(a, b):
    return jnp.dot(a, b, preferred_element_type=F32)


def _split_bf16(x):
    hi = x.astype(BF16)
    lo = (x - hi.astype(F32)).astype(BF16)
    return hi, lo


def _log2(n):
    assert n > 0 and n & (n - 1) == 0, "power of two expected"
    return n.bit_length() - 1


def _cparams(vmem_mb):
    return pltpu.CompilerParams(dimension_semantics=("arbitrary",),
                                vmem_limit_bytes=vmem_mb * 1024 * 1024)


def _mod_kernel(c_ref, w_ref, b_ref, o_ref):
    c = c_ref[...]
    s = (c * _sigmoid(c)).astype(BF16)
    o_ref[...] = _dot(s, w_ref[...].astype(BF16)) + b_ref[...]


def _mod_call(c_all, w_ada, b_ada):
    rows = c_all.shape[0]
    return pl.pallas_call(
        _mod_kernel,
        grid=(6,),
        in_specs=[pl.BlockSpec((rows, D), lambda j: (0, 0)),
                  pl.BlockSpec((D, D), lambda j: (0, j)),
                  pl.BlockSpec((1, D), lambda j: (0, j))],
        out_specs=pl.BlockSpec((rows, D), lambda j: (0, j)),
        out_shape=jax.ShapeDtypeStruct((rows, 6 * D), F32),
        compiler_params=_cparams(32),
        name="mod",
    )(c_all, w_ada, b_ada)


def _row_mod(i, geom, table_ref, sample_ref):
    n_prompt_rows, prompt_len, _, n_prompt_seq = geom
    seq = jnp.minimum((i * TM) >> _log2(prompt_len), n_prompt_seq - 1)
    return jnp.where(i < n_prompt_rows // TM, table_ref[pl.ds(seq, 1), :], sample_ref[...])


def _proj_kernel(geom, xp_ref, xs_ref, mhi_ref, mlo_ref, g_ref, lbl_ref, wm_ref, wlr_ref,
                 w2_ref, bg_ref,
                 qa_ref, ka_ref, va_ref, ra_ref, ga_ref, qb_ref, kb_ref, gb_ref, ib_ref,
                 rb_ref, ua_ref, ub_ref):
    i = pl.program_id(0)
    n_prompt_tiles = geom[0] // TM
    x = jnp.where(i < n_prompt_tiles, xp_ref[...], xs_ref[...])
    mod = _row_mod(i, geom, mhi_ref, mlo_ref)
    sh, sc = mod[:, :D], mod[:, D:]
    ms = jnp.mean(x * x, axis=-1, keepdims=True)
    h = x * lax.rsqrt(ms + EPS) * g_ref[...]
    hb = (h * (1.0 + sc) + sh).astype(BF16)

    def proj(a, b):
        return _dot(hb, wm_ref[:, a:b])

    qa_ref[...] = (proj(C_QA, C_KA) * GLA_DK ** -0.5).astype(BF16)
    ka_ref[...] = proj(C_KA, C_VA).astype(BF16)
    va_ref[...] = proj(C_VA, C_RA).astype(BF16)
    r = proj(C_RA, C_QB)
    ra_ref[...] = (r * _sigmoid(r)).astype(BF16)
    lr_hi, lr_lo = _split_bf16(_dot(hb, wlr_ref[...]))
    xg = _dot(lr_hi, w2_ref[...]) + _dot(lr_lo, w2_ref[...]) + bg_ref[...]
    ga_ref[...] = (jnp.minimum(xg, 0.0) - jnp.log1p(jnp.exp(-jnp.abs(xg)))) * (1.0 / GATE_NORM)
    q = proj(C_QB, C_FB)
    qb_ref[...] = (q * _sigmoid(q) * HG_DK ** -0.5).astype(BF16)
    lbl = lbl_ref[...]
    e = jnp.exp(lbl - jnp.max(lbl, axis=0, keepdims=True))
    lb = e[0:1, :] / jnp.sum(e, axis=0, keepdims=True)
    fb = proj(C_FB, C_IB)
    kb = (1.0 - lb) * _sigmoid(-fb)
    kb_ref[...] = kb.astype(BF16)
    gb_ref[...] = jnp.log(lb + (1.0 - lb) * _sigmoid(fb))
    ib_ref[...] = proj(C_IB, C_RB).astype(BF16)
    r = proj(C_RB, C_UA)
    rb_ref[...] = (r * _sigmoid(r)).astype(BF16)
    ua_ref[...] = _sigmoid(proj(C_UA, C_UB)).astype(BF16)
    ub_ref[...] = _sigmoid(proj(C_UB, C_END)).astype(BF16)


def _proj_call(geom, n_rows, xp, xs, mhi, mlo, g_mix, lbl, wm, wlr, w2, bg):
    n_tiles = n_rows // TM
    last_p = geom[0] // TM - 1
    const = lambda i: (0, 0)
    row = lambda i: (i, 0)
    widths = [(512, BF16), (512, BF16), (D, BF16), (D, BF16), (512, F32), (D, BF16), (D, BF16),
              (D, F32), (D, BF16), (D, BF16), (D, BF16), (D, BF16)]
    return pl.pallas_call(
        functools.partial(_proj_kernel, geom),
        grid=(n_tiles,),
        in_specs=[pl.BlockSpec((TM, D), lambda i: (jnp.minimum(i, last_p), 0)),
                  pl.BlockSpec(xs.shape, const),
                  pl.BlockSpec(mhi.shape, const), pl.BlockSpec(mlo.shape, const),
                  pl.BlockSpec((1, D), const), pl.BlockSpec(lbl.shape, const),
                  pl.BlockSpec(wm.shape, const, pipeline_mode=pl.Buffered(1)),
                  pl.BlockSpec(wlr.shape, const), pl.BlockSpec(w2.shape, const),
                  pl.BlockSpec(bg.shape, const)],
        out_specs=[pl.BlockSpec((TM, w), row) for w, _ in widths],
        out_shape=[jax.ShapeDtypeStruct((n_rows, w), dt) for w, dt in widths],
        compiler_params=_cparams(56),
        name="proj",
    )(xp, xs, mhi, mlo, g_mix, lbl, wm, wlr, w2, bg)


def _rec_chunk(tri, causal, g, q, k, v, r, n_heads, dk, dv, s_ref, gn, o_ref, r0, rows_out):
    g_hi, g_lo = _split_bf16(g)
    cum = _dot(tri, g_hi) + _dot(tri, g_lo)
    width = cum.shape[1]
    tot = cum[CHUNK - 1:CHUNK, :]
    refs = [jnp.zeros((1, width), F32)] + [cum[j * SUB - 1:j * SUB, :] for j in range(1, NSUB)]
    d = cum - jnp.concatenate([jnp.broadcast_to(b, (SUB, width)) for b in refs], axis=0)
    qn = q.astype(F32) * jnp.exp(d)
    kn = k.astype(F32) * jnp.exp(jnp.minimum(-d, EXP_CLAMP))
    sub = lambda x, j: x[j * SUB:(j + 1) * SUB, :]
    qs = jnp.concatenate([sub(qn, j) * jnp.exp(refs[j]) for j in range(NSUB)],
                         axis=0).astype(BF16)
    kd = jnp.concatenate([sub(kn, j) * jnp.exp(tot - refs[j]) for j in range(NSUB)], axis=0)
    et = jnp.exp(tot)
    qnb = qn.astype(BF16)
    k_seen = []
    for i in range(NSUB):
        parts = [sub(kn, j) * jnp.exp(refs[i] - refs[j]) for j in range(i)] + [sub(kn, i)]
        if i + 1 < NSUB:
            parts.append(jnp.zeros((CHUNK - (i + 1) * SUB, width), F32))
        k_seen.append(jnp.concatenate(parts, axis=0).astype(BF16))
    for h in range(n_heads):
        ks = slice(h * dk, (h + 1) * dk)
        vs = slice(h * dv, (h + 1) * dv)
        a = jnp.concatenate(
            [lax.dot_general(sub(qnb, i)[:, ks], k_seen[i][:, ks], (((1,), (1,)), ((), ())),
                             preferred_element_type=F32) for i in range(NSUB)], axis=0)
        a = jnp.where(causal, a, 0.0).astype(BF16)
        s = s_ref[0, h]
        kd_t = kd[:, ks].T.astype(BF16)
        av = _dot(jnp.concatenate([a, kd_t], axis=0), v[:, vs])
        o = _dot(qs[:, ks], s.astype(BF16)) + av[:CHUNK]
        et_col = jnp.broadcast_to(et[:, ks], (dk, dk)).T
        s_ref[0, h] = s * jnp.tile(et_col, (1, dv // dk)) + av[CHUNK:]
        ms = jnp.mean(o * o, axis=-1, keepdims=True)
        og = o * lax.rsqrt(ms + EPS) * gn * r[:, vs].astype(F32)
        o_ref[r0:r0 + rows_out, vs] = og[:rows_out].astype(BF16)


def _rec_kernel(rows, has_s0, *refs):
    (qa_ref, ka_ref, va_ref, ra_ref, ga_ref, qb_ref, kb_ref, gb_ref, ib_ref, rb_ref,
     gna_ref, gnb_ref) = refs[:12]
    refs = refs[12:]
    if has_s0:
        s0a_ref, s0b_ref = refs[:2]
        refs = refs[2:]
    oga_ref, ogb_ref, sa_ref, sb_ref = refs

    @pl.when(pl.program_id(1) == 0)
    def _():
        if has_s0:
            sa_ref[...] = s0a_ref[...]
            sb_ref[...] = s0b_ref[...]
        else:
            sa_ref[...] = jnp.zeros_like(sa_ref)
            sb_ref[...] = jnp.zeros_like(sb_ref)

    ti = lax.broadcasted_iota(I32, (CHUNK, CHUNK), 0)
    si = lax.broadcasted_iota(I32, (CHUNK, CHUNK), 1)
    causal = ti >= si
    tri = causal.astype(BF16)
    n_chunks = max(rows // CHUNK, 1)
    rows_out = min(rows, CHUNK)

    def load(ref, c):
        if rows >= CHUNK:
            return ref[c * CHUNK:(c + 1) * CHUNK, :]
        x = ref[...]
        return jnp.concatenate([x, jnp.zeros((CHUNK - rows, x.shape[1]), x.dtype)], axis=0)

    for c in range(n_chunks):
        _rec_chunk(tri, causal, load(ga_ref, c), load(qa_ref, c), load(ka_ref, c), load(va_ref, c),
                   load(ra_ref, c), GLA_H, GLA_DK, GLA_DV, sa_ref, gna_ref[...], oga_ref,
                   c * CHUNK, rows_out)
        _rec_chunk(tri, causal, load(gb_ref, c), load(qb_ref, c), load(kb_ref, c), load(ib_ref, c),
                   load(rb_ref, c), HG_H, HG_DK, HG_DV, sb_ref, gnb_ref[...], ogb_ref,
                   c * CHUNK, rows_out)


def _rec_call(arrs, gna, gnb, n_seq, seq_len, rows, row0, s0=None):
    steps = seq_len // rows
    blk0 = row0 // rows
    row_in = lambda b, t: (blk0 + b * steps + t, 0)
    row_out = lambda b, t: (b * steps + t, 0)
    const = lambda b, t: (0, 0)
    st = lambda b, t: (b, 0, 0, 0)
    in_specs = [pl.BlockSpec((rows, a.shape[1]), row_in) for a in arrs]
    in_specs += [pl.BlockSpec(gna.shape, const), pl.BlockSpec(gnb.shape, const)]
    args = list(arrs) + [gna, gnb]
    if s0 is not None:
        in_specs += [pl.BlockSpec((1, GLA_H, GLA_DK, GLA_DV), st),
                     pl.BlockSpec((1, HG_H, HG_DK, HG_DV), st)]
        args += [s0[0], s0[1]]
    n_rows = n_seq * seq_len
    return pl.pallas_call(
        functools.partial(_rec_kernel, rows, s0 is not None),
        grid=(n_seq, steps),
        in_specs=in_specs,
        out_specs=[pl.BlockSpec((rows, D), row_out), pl.BlockSpec((rows, D), row_out),
                   pl.BlockSpec((1, GLA_H, GLA_DK, GLA_DV), st),
                   pl.BlockSpec((1, HG_H, HG_DK, HG_DV), st)],
        out_shape=[jax.ShapeDtypeStruct((n_rows, D), BF16), jax.ShapeDtypeStruct((n_rows, D), BF16),
                   jax.ShapeDtypeStruct((n_seq, GLA_H, GLA_DK, GLA_DV), F32),
                   jax.ShapeDtypeStruct((n_seq, HG_H, HG_DK, HG_DV), F32)],
        compiler_params=pltpu.CompilerParams(dimension_semantics=("arbitrary", "arbitrary"),
                                             vmem_limit_bytes=48 * 1024 * 1024),
        name="rec_s0" if s0 is not None else "rec",
    )(*args)


def _post_kernel(geom, xp_ref, xs_ref, oap_ref, oas_ref, obp_ref, obs_ref, ua_ref, ub_ref,
                 mhi_ref, mlo_ref, gf_ref, wba_ref, wbb_ref, wo_ref, wrh_ref, wrl_ref, br_ref,
                 x1_ref, h2_ref, te_ref, tw_ref):
    i = pl.program_id(0)
    is_prompt = i < geom[0] // TM
    x = jnp.where(is_prompt, xp_ref[...], xs_ref[...])
    oga = jnp.where(is_prompt, oap_ref[...], oas_ref[...])
    ogb = jnp.where(is_prompt, obp_ref[...], obs_ref[...])
    mod = _row_mod(i, geom, mhi_ref, mlo_ref)
    gt1, sh2, sc2 = mod[:, :D], mod[:, D:2 * D], mod[:, 2 * D:]
    merged = (ua_ref[...].astype(F32) * _dot(oga, wba_ref[...])
              + ub_ref[...].astype(F32) * _dot(ogb, wbb_ref[...]))
    x1 = x + gt1 * _dot(merged.astype(BF16), wo_ref[...])
    x1_ref[...] = x1
    ms = jnp.mean(x1 * x1, axis=-1, keepdims=True)
    h2 = x1 * lax.rsqrt(ms + EPS) * gf_ref[...] * (1.0 + sc2) + sh2
    h2_ref[...] = h2
    h_hi, h_lo = _split_bf16(h2)
    logit = (_dot(h_hi, wrh_ref[...]) + _dot(h_lo, wrh_ref[...]) + _dot(h_hi, wrl_ref[...])
             + br_ref[...])
    lane = lax.broadcasted_iota(I32, (TM, LANES), 1)
    lane_f = lane.astype(F32)
    vals, idxs = [], []
    for _ in range(TOP_K):
        m = jnp.max(logit, axis=-1, keepdims=True)
        idx = jnp.min(jnp.where(logit == m, lane_f, float(LANES)), axis=-1, keepdims=True)
        vals.append(m)
        idxs.append(idx)
        logit = jnp.where(lane_f == idx, -jnp.inf, logit)
    es = [jnp.exp(v - vals[0]) for v in vals]
    den = es[0] + es[1] + es[2] + es[3]
    te = jnp.zeros((TM, LANES), F32)
    tw = jnp.zeros((TM, LANES), F32)
    for k in range(TOP_K):
        te = jnp.where(lane == k, idxs[k], te)
        tw = jnp.where(lane == k, es[k] / den, tw)
    te_ref[...] = te.astype(I32)
    tw_ref[...] = tw


def _post_call(geom, n_rows, xp, xs, oap, oas, obp, obs, ua, ub, mhi, mlo, g_ffn, wba, wbb, wo,
               wrh, wrl, br):
    n_tiles = n_rows // TM
    last_p = geom[0] // TM - 1
    const = lambda i: (0, 0)
    row = lambda i: (i, 0)
    prow = pl.BlockSpec((TM, D), lambda i: (jnp.minimum(i, last_p), 0))
    full = lambda a: pl.BlockSpec(a.shape, const)
    return pl.pallas_call(
        functools.partial(_post_kernel, geom),
        grid=(n_tiles,),
        in_specs=[prow, full(xs), prow, full(oas), prow, full(obs),
                  pl.BlockSpec((TM, D), row), pl.BlockSpec((TM, D), row),
                  full(mhi), full(mlo), full(g_ffn), full(wba), full(wbb), full(wo),
                  full(wrh), full(wrl), full(br)],
        out_specs=[pl.BlockSpec((TM, D), row), pl.BlockSpec((TM, D), row),
                   pl.BlockSpec((TM, LANES), row), pl.BlockSpec((TM, LANES), row)],
        out_shape=[jax.ShapeDtypeStruct((n_rows, D), F32), jax.ShapeDtypeStruct((n_rows, D), F32),
                   jax.ShapeDtypeStruct((n_rows, LANES), I32),
                   jax.ShapeDtypeStruct((n_rows, LANES), F32)],
        compiler_params=_cparams(48),
        name="post",
    )(xp, xs, oap, oas, obp, obs, ua, ub, mhi, mlo, g_ffn, wba, wbb, wo, wrh, wrl, br)


def _multi_hot(te):
    lane = lax.broadcasted_iota(I32, (TM, LANES), 1)
    m = jnp.zeros((TM, LANES), F32)
    for k in range(TOP_K):
        m = m + (lane == te[:, k:k + 1]).astype(F32)
    return m


def _count_kernel(te_ref, cnt_ref):
    cnt_ref[...] = jnp.broadcast_to(jnp.sum(_multi_hot(te_ref[...]), axis=0, keepdims=True),
                                    (8, LANES))


def _count_call(te, n_rows):
    n_tiles = n_rows // TM
    return pl.pallas_call(
        _count_kernel,
        grid=(n_tiles,),
        in_specs=[pl.BlockSpec((TM, LANES), lambda i: (i, 0))],
        out_specs=pl.BlockSpec((8, LANES), lambda i: (i, 0)),
        out_shape=jax.ShapeDtypeStruct((n_tiles * 8, LANES), F32),
        compiler_params=_cparams(32),
        name="count",
    )(te)


def _for_each_run_piece(tab_ref, fn):
    for e in range(N_EXP):
        length = tab_ref[0, 0, e]
        src = tab_ref[0, 0, N_EXP + e]
        dst = tab_ref[0, 0, 2 * N_EXP + e]
        off = jnp.int32(0)
        for p in RUN_PIECES:
            has = (length & p) != 0
            pl.when(has)(lambda off=off, p=p, src=src, dst=dst: fn(
                pl.multiple_of(src + off, RUN), pl.multiple_of(dst + off, RUN), p))
            off = off + jnp.where(has, p, 0)


def _dispatch_kernel(pend_ref, tab_ref, te_ref, gsrc_ref, h2_ref, xs_ref, pos_ref, z_ref, zeros,
                     sem, zsem):
    @pl.when(pl.program_id(0) == 0)
    def _():
        zeros[...] = jnp.zeros_like(zeros)

        def zero_copy(e):
            start = pl.multiple_of(pend_ref[e] - BM, BM)
            return pltpu.make_async_copy(zeros, xs_ref.at[pl.ds(start, BM)], zsem)

        def nonempty(e):
            return pend_ref[e] > (pend_ref[e - 1] if e > 0 else 0)

        for e in range(N_EXP):
            pl.when(nonempty(e))(lambda e=e: zero_copy(e).start())
        for e in range(N_EXP):
            pl.when(nonempty(e))(lambda e=e: zero_copy(e).wait())

        def tail_copy(j):
            return pltpu.make_async_copy(zeros, xs_ref.at[pl.ds(pl.multiple_of(j * BM, BM), BM)],
                                         zsem)

        first, last = pend_ref[N_EXP - 1] // BM, xs_ref.shape[0] // BM
        lax.fori_loop(first, last, lambda j, c: (tail_copy(j).start(), c)[1], 0)
        lax.fori_loop(first, last, lambda j, c: (tail_copy(j).wait(), c)[1], 0)

    te = te_ref[...]
    ri = lax.broadcasted_iota(I32, (TM, TM), 0)
    ci = lax.broadcasted_iota(I32, (TM, TM), 1)
    rank = _dot((ri > ci).astype(BF16), _multi_hot(te).astype(BF16)) + gsrc_ref[0:1, :]
    lane = lax.broadcasted_iota(I32, (TM, LANES), 1)
    col = lax.broadcasted_iota(I32, (TM, SORT_ROWS), 1)
    pos = jnp.zeros((TM, LANES), F32)
    onehot = jnp.zeros((TM, SORT_ROWS), F32)
    for k in range(TOP_K):
        p = jnp.sum(jnp.where(lane == te[:, k:k + 1], rank, 0.0), axis=-1, keepdims=True)
        pos = jnp.where(lane == k, p, pos)
        onehot = jnp.where(col == p.astype(I32), 1.0, onehot)
    pos_ref[...] = pos.astype(I32)
    z_ref[...] = lax.dot_general(onehot.astype(BF16), h2_ref[...].astype(BF16),
                                 (((0,), (0,)), ((), ())), preferred_element_type=F32)

    _for_each_run_piece(tab_ref, lambda s, d, p: pltpu.make_async_copy(
        z_ref.at[pl.ds(s, p)], xs_ref.at[pl.ds(d, p)], sem).start())

    def wait(j, c):
        pltpu.make_async_copy(z_ref.at[pl.ds(0, RUN)], xs_ref.at[pl.ds(0, RUN)], sem).wait()
        return c

    lax.fori_loop(0, tab_ref[0, 0, TAB_ROWS] >> _log2(RUN), wait, 0)


def _dispatch_call(pend, tab, te, gsrc, h2, cap):
    n_tiles = tab.shape[0]
    row = lambda i, pe: (i, 0)
    return pl.pallas_call(
        _dispatch_kernel,
        grid_spec=pltpu.PrefetchScalarGridSpec(
            num_scalar_prefetch=1,
            grid=(n_tiles,),
            in_specs=[pl.BlockSpec((1, 1, LANES), lambda i, pe: (i, 0, 0), memory_space=pltpu.SMEM),
                      pl.BlockSpec((TM, LANES), row), pl.BlockSpec((8, LANES), row),
                      pl.BlockSpec((TM, D), row)],
            out_specs=[pl.BlockSpec(memory_space=pl.ANY), pl.BlockSpec((TM, LANES), row)],
            scratch_shapes=[pltpu.VMEM((SORT_ROWS, D), F32), pltpu.VMEM((BM, D), F32),
                            pltpu.SemaphoreType.DMA, pltpu.SemaphoreType.DMA]),
        out_shape=[jax.ShapeDtypeStruct((cap, D), F32),
                   jax.ShapeDtypeStruct((n_tiles * TM, LANES), I32)],
        compiler_params=_cparams(40),
        name="dispatch",
    )(pend, tab, te, gsrc, h2)


def _expert_kernel(be_ref, nv_ref, seg_ref, nxt_ref, x_ref, wgu_hbm, bgu_ref, wd_hbm, bd_ref, o_ref,
                   wgu_f, wd_f, wgu_s, wd_s, sem):
    j = pl.program_id(0)
    jc = jnp.minimum(j, nv_ref[0] - 1)
    e = be_ref[jc]
    slot = lax.rem(seg_ref[jc], 2)
    first = (j == 0) | ((j < nv_ref[0]) & (e != be_ref[jnp.maximum(jc - 1, 0)]))

    def fetch(expert, into):
        return (pltpu.make_async_copy(wgu_hbm.at[expert], wgu_f.at[into], sem.at[0, into]),
                pltpu.make_async_copy(wd_hbm.at[expert], wd_f.at[into], sem.at[1, into]))

    @pl.when(j == 0)
    def _():
        for c in fetch(e, slot):
            c.start()

    @pl.when(first)
    def _():
        for c in fetch(e, slot):
            c.wait()

        @pl.when(nxt_ref[jc] >= 0)
        def _():
            for c in fetch(nxt_ref[jc], 1 - slot):
                c.start()

        wgu_s[...] = wgu_f[slot].astype(BF16)
        wd_s[...] = wd_f[slot].astype(BF16)

    @pl.when(j < nv_ref[0])
    def _():
        gu = _dot(x_ref[...].astype(BF16), wgu_s[...]) + bgu_ref[0]
        gate = jnp.minimum(gu[:, :D], SWIGLU_LIMIT)
        up = jnp.clip(gu[:, D:], -SWIGLU_LIMIT, SWIGLU_LIMIT)
        act = (up + 1.0) * (gate * _sigmoid(SWIGLU_ALPHA * gate))
        o_ref[...] = _dot(act.astype(BF16), wd_s[...]) + bd_ref[0]

    @pl.when(j >= nv_ref[0])
    def _():
        o_ref[...] = jnp.zeros_like(o_ref)


def _expert_call(block_e, n_valid, seg, nxt, xs, wgu, bgu, wd, bd):
    n_blocks = xs.shape[0] // BM
    blk = lambda j, be, nv, sg, nx: (jnp.minimum(j, nv[0] - 1), 0)
    blk_out = lambda j, be, nv, sg, nx: (j, 0)
    exp = lambda j, be, nv, sg, nx: (be[jnp.minimum(j, nv[0] - 1)], 0, 0)
    return pl.pallas_call(
        _expert_kernel,
        grid_spec=pltpu.PrefetchScalarGridSpec(
            num_scalar_prefetch=4,
            grid=(n_blocks,),
            in_specs=[pl.BlockSpec((BM, D), blk),
                      pl.BlockSpec(memory_space=pl.ANY), pl.BlockSpec((1, 1, 2 * D), exp),
                      pl.BlockSpec(memory_space=pl.ANY), pl.BlockSpec((1, 1, D), exp)],
            out_specs=pl.BlockSpec((BM, D), blk_out),
            scratch_shapes=[pltpu.VMEM((2, D, 2 * D), F32), pltpu.VMEM((2, D, D), F32),
                            pltpu.VMEM((D, 2 * D), BF16), pltpu.VMEM((D, D), BF16),
                            pltpu.SemaphoreType.DMA((2, 2))]),
        out_shape=jax.ShapeDtypeStruct(xs.shape, F32),
        compiler_params=_cparams(56),
        name="expert",
    )(block_e, n_valid, seg, nxt, xs, wgu, bgu, wd, bd)


def _combine_kernel(geom, tcur_ref, tnxt_ref, eo_ref, pos_ref, x1_ref, tw_ref, mhi_ref, mlo_ref,
                    gfin_ref, yp_ref, ys_ref, buf, sem):
    i = pl.program_id(0)
    n_prompt_tiles = geom[0] // TM
    slot = lax.rem(i, 2)

    def gather(tab_ref, into):
        _for_each_run_piece(tab_ref, lambda s, d, p: pltpu.make_async_copy(
            eo_ref.at[pl.ds(d, p)], buf.at[into, pl.ds(s, p)], sem.at[into]).start())

    @pl.when(i == 0)
    def _():
        buf[...] = jnp.zeros_like(buf)
        gather(tcur_ref, 0)

    @pl.when(i + 1 < pl.num_programs(0))
    def _():
        gather(tnxt_ref, 1 - slot)

    def wait(j, c):
        pltpu.make_async_copy(eo_ref.at[pl.ds(0, RUN)], buf.at[slot, pl.ds(0, RUN)],
                              sem.at[slot]).wait()
        return c

    lax.fori_loop(0, tcur_ref[0, 0, TAB_ROWS] >> _log2(RUN), wait, 0)
    gt2 = _row_mod(i, geom, mhi_ref, mlo_ref)
    tw = tw_ref[...]
    pos = pos_ref[...]
    col = lax.broadcasted_iota(I32, (TM, SORT_ROWS), 1)
    wsel = jnp.zeros((TM, SORT_ROWS), F32)
    for k in range(TOP_K):
        wsel = jnp.where(col == pos[:, k:k + 1], tw[:, k:k + 1], wsel)
    y = _dot(wsel.astype(BF16), buf[slot].astype(BF16))
    x2 = x1_ref[...] + gt2 * y
    ms = jnp.mean(x2 * x2, axis=-1, keepdims=True)
    out = x2 * lax.rsqrt(ms + EPS) * gfin_ref[...]

    @pl.when(i < n_prompt_tiles)
    def _():
        yp_ref[...] = out

    @pl.when(i >= n_prompt_tiles)
    def _():
        ys_ref[...] = out


def _combine_call(geom, n_rows, tab, eo, pos, x1, tw, mhi, mlo, g_final):
    n_tiles = n_rows // TM
    n_prompt = geom[0]
    last_p = n_prompt // TM - 1
    const = lambda i: (0, 0)
    row = lambda i: (i, 0)
    return pl.pallas_call(
        functools.partial(_combine_kernel, geom),
        grid=(n_tiles,),
        in_specs=[pl.BlockSpec((1, 1, LANES), lambda i: (i, 0, 0), memory_space=pltpu.SMEM),
                  pl.BlockSpec((1, 1, LANES), lambda i: (jnp.minimum(i + 1, n_tiles - 1), 0, 0),
                               memory_space=pltpu.SMEM),
                  pl.BlockSpec(memory_space=pl.ANY), pl.BlockSpec((TM, LANES), row),
                  pl.BlockSpec((TM, D), row), pl.BlockSpec((TM, LANES), row),
                  pl.BlockSpec(mhi.shape, const), pl.BlockSpec(mlo.shape, const),
                  pl.BlockSpec((1, D), const)],
        out_specs=[pl.BlockSpec((TM, D), lambda i: (jnp.minimum(i, last_p), 0)),
                   pl.BlockSpec((n_rows - n_prompt, D), const)],
        out_shape=[jax.ShapeDtypeStruct((n_prompt, D), F32),
                   jax.ShapeDtypeStruct((n_rows - n_prompt, D), F32)],
        scratch_shapes=[pltpu.VMEM((2, SORT_ROWS, D), F32), pltpu.SemaphoreType.DMA((2,))],
        compiler_params=_cparams(48),
        name="combine",
    )(tab, tab, eo, pos, x1, tw, mhi, mlo, g_final)


def kernel(x_prompt, x_sample, c_prompt, c_sample, state_gla, state_hgrn, w_ada, b_ada, g_norm_mix,
           g_norm_ffn, w_in, w_gla_gate2, b_gla_gate, g_gla_onorm, hgrn_lb_logits, g_hgrn_onorm,
           w_branch_a, w_branch_b, w_out, w_router, b_router, w_gate_up, b_gate_up, w_down, b_down,
           g_final):
    assert w_ada.shape[0] == 1, "single-layer trunk only"
    bp, lp, _ = x_prompt.shape
    bs, ls, _ = x_sample.shape
    n_p, n_s = bp * lp, bs * ls
    n = n_p + n_s
    assert n_p % TM == 0 and n_s == TM and lp % (2 * CHUNK) == 0 and ls <= CHUNK and bp + bs <= 32
    assert SORT_ROWS >= TM * TOP_K + N_EXP * (RUN - 1) and RUN_PIECES[0] == TM
    geom = (n_p, lp, ls, bp)
    xp = x_prompt.reshape(n_p, D)
    xs = x_sample.reshape(n_s, D)

    c_all = jnp.zeros((32, D), F32).at[:bp].set(c_prompt).at[bp:bp + bs].set(c_sample)
    mod = _mod_call(c_all, w_ada[0], b_ada[0].reshape(1, 6 * D))
    sh1, sc1, gt1, sh2, sc2, gt2 = [mod[:, j * D:(j + 1) * D] for j in range(6)]
    def table_and_sample_rows(t):
        return t, jnp.repeat(t[bp:bp + bs], ls, axis=0)

    m1_hi, m1_lo = table_and_sample_rows(jnp.concatenate([sh1, sc1], axis=1))
    m2_hi, m2_lo = table_and_sample_rows(jnp.concatenate([gt1, sh2, sc2], axis=1))
    m3_hi, m3_lo = table_and_sample_rows(gt2)

    wi = w_in[0]
    wm = jnp.concatenate([wi[:, :3072], wi[:, 3072 + GATE_RANK:]], axis=1).astype(BF16)
    wlr = jnp.pad(wi[:, 3072:3072 + GATE_RANK], ((0, 0), (0, LANES - GATE_RANK))).astype(BF16)
    w2 = jnp.pad(w_gla_gate2[0], ((0, LANES - GATE_RANK), (0, 0))).astype(BF16)
    arrs = _proj_call(geom, n, xp, xs, m1_hi, m1_lo, g_norm_mix[0].reshape(1, D), hgrn_lb_logits,
                      wm, wlr, w2, b_gla_gate[0].reshape(1, -1))
    qa, ka, va, ra, ga, qb, kb, gb, ib, rb, ua, ub = arrs
    rec_in = (qa, ka, va, ra, ga, qb, kb, gb, ib, rb)

    gna = g_gla_onorm[0].reshape(1, GLA_DV)
    gnb = g_hgrn_onorm[0].reshape(1, HG_DV)
    oap, obp, sa_p, sb_p = _rec_call(rec_in, gna, gnb, bp, lp, 2 * CHUNK, 0)
    oas, obs, sa_s, sb_s = _rec_call(rec_in, gna, gnb, bs, ls, ls, n_p,
                                     s0=(state_gla[0], state_hgrn[0]))

    wr = jnp.pad(w_router[0], ((0, 0), (0, LANES - N_EXP)))
    wr_hi, wr_lo = _split_bf16(wr)
    br = jnp.pad(b_router[0], (0, LANES - N_EXP), constant_values=NEG).reshape(1, LANES)
    x1, h2, te, tw = _post_call(geom, n, xp, xs, oap, oas, obp, obs, ua, ub, m2_hi, m2_lo,
                                g_norm_ffn[0].reshape(1, D), w_branch_a[0].astype(BF16),
                                w_branch_b[0].astype(BF16), w_out[0].astype(BF16), wr_hi, wr_lo, br)

    n_tiles = n // TM
    run = (_count_call(te, n)[::8, :N_EXP].astype(I32) + RUN - 1) // RUN * RUN
    counts = jnp.sum(run, axis=0)
    seg_len = (counts + BM - 1) // BM * BM
    pend = jnp.cumsum(seg_len).astype(I32)
    run_dst = (pend - seg_len)[None, :] + jnp.cumsum(run, axis=0) - run
    run_src = jnp.cumsum(run, axis=1) - run
    tab = jnp.zeros((n_tiles, LANES), I32)
    tab = tab.at[:, :N_EXP].set(run).at[:, N_EXP:2 * N_EXP].set(run_src)
    tab = tab.at[:, 2 * N_EXP:3 * N_EXP].set(run_dst).at[:, TAB_ROWS].set(jnp.sum(run, axis=1))
    tab = tab.reshape(n_tiles, 1, LANES)
    gsrc = jnp.repeat(jnp.pad(run_src.astype(F32), ((0, 0), (0, LANES - N_EXP))), 8, axis=0)
    cap = (n * TOP_K + n_tiles * N_EXP * (RUN - 1) + N_EXP * (BM - 1) + BM - 1) // BM * BM
    block_start = jnp.arange(cap // BM, dtype=I32) * BM
    block_e = jnp.minimum(jnp.sum((pend[None, :] <= block_start[:, None]).astype(I32), axis=1),
                          N_EXP - 1)
    n_valid = pend[-1:] // BM
    ids = jnp.arange(N_EXP, dtype=I32)
    nonempty = counts > 0
    later = jnp.where(nonempty[None, :] & (ids[None, :] > ids[:, None]), ids[None, :], N_EXP)
    next_e = jnp.min(later, axis=1)
    next_e = jnp.where(next_e == N_EXP, -1, next_e)
    seg = (jnp.cumsum(nonempty.astype(I32)) - 1)[block_e]
    nxt = next_e[block_e]
    xs_sorted, pos = _dispatch_call(pend, tab, te, gsrc, h2, cap)
    eo = _expert_call(block_e, n_valid, seg, nxt, xs_sorted, w_gate_up[0],
                      b_gate_up[0].reshape(N_EXP, 1, -1), w_down[0], b_down[0].reshape(N_EXP, 1, -1))

    yp, ys = _combine_call(geom, n, tab, eo, pos, x1, tw, m3_hi, m3_lo, g_final.reshape(1, D))
    return (yp.reshape(bp, lp, D), ys.reshape(bs, ls, D),
            sa_p[None], sb_p[None], sa_s[None], sb_s[None])
```

```python
import functools

import jax
import jax.numpy as jnp
from jax import lax
from jax.experimental import pallas as pl
from jax.experimental.pallas import tpu as pltpu

F32 = jnp.float32
BF16 = jnp.bfloat16
I32 = jnp.int32

EPS = 1e-6
D = 1024
GLA_H, GLA_DK, GLA_DV = 4, 128, 256
HG_H, HG_DK, HG_DV = 8, 128, 128
GATE_RANK = 16
GATE_NORM = 16.0
N_EXP = 32
TOP_K = 4
SWIGLU_LIMIT = 7.0
SWIGLU_ALPHA = 1.702

LANES = 128
TM = 256
CHUNK = 128
SUB = 32
NSUB = CHUNK // SUB
EXP_CLAMP = 80.0
BM = 256
ROUTE_ROWS = 1280
RUN = 8
RUN_PIECES = (256, 128, 64, 32, 16, 8)
SORT_ROWS = TM * TOP_K + 256
TAB_ROWS = 3 * N_EXP
NEG = -1e30

C_QA, C_KA, C_VA, C_RA, C_QB, C_FB, C_IB, C_RB, C_UA, C_UB, C_END = (
    0, 512, 1024, 2048, 3072, 4096, 5120, 6144, 7168, 8192, 9216)


def _sigmoid(x):
    return 1.0 / (1.0 + jnp.exp(-x))


def _dot(a, b):
    return jnp.dot(a, b, preferred_element_type=F32)


def _split_bf16(x):
    hi = x.astype(BF16)
    lo = (x - hi.astype(F32)).astype(BF16)
    return hi, lo


def _log2(n):
    assert n > 0 and n & (n - 1) == 0, "power of two expected"
    return n.bit_length() - 1


def _cparams(vmem_mb):
    return pltpu.CompilerParams(dimension_semantics=("arbitrary",),
                                vmem_limit_bytes=vmem_mb * 1024 * 1024)


def _mod_kernel(c_ref, w_ref, b_ref, o_ref):
    c = c_ref[...]
    s = (c * _sigmoid(c)).astype(BF16)
    o_ref[...] = _dot(s, w_ref[...].astype(BF16)) + b_ref[...]


def _mod_call(c_all, w_ada, b_ada):
    rows = c_all.shape[0]
    return pl.pallas_call(
        _mod_kernel,
        grid=(6,),
        in_specs=[pl.BlockSpec((rows, D), lambda j: (0, 0)),
                  pl.BlockSpec((D, D), lambda j: (0, j)),
                  pl.BlockSpec((1, D), lambda j: (0, j))],
        out_specs=pl.BlockSpec((rows, D), lambda j: (0, j)),
        out_shape=jax.ShapeDtypeStruct((rows, 6 * D), F32),
        compiler_params=_cparams(32),
        name="mod",
    )(c_all, w_ada, b_ada)


def _row_mod(i, geom, table_ref, sample_ref):
    n_prompt_rows, prompt_len, _, n_prompt_seq = geom
    seq = jnp.minimum((i * TM) >> _log2(prompt_len), n_prompt_seq - 1)
    return jnp.where(i < n_prompt_rows // TM, table_ref[pl.ds(seq, 1), :], sample_ref[...])


def _proj_kernel(geom, xp_ref, xs_ref, mhi_ref, mlo_ref, g_ref, lbl_ref, wm_ref, wlr_ref,
                 w2_ref, bg_ref,
                 qa_ref, ka_ref, va_ref, ra_ref, ga_ref, qb_ref, kb_ref, gb_ref, ib_ref,
                 rb_ref, ua_ref, ub_ref):
    i = pl.program_id(0)
    n_prompt_tiles = geom[0] // TM
    x = jnp.where(i < n_prompt_tiles, xp_ref[...], xs_ref[...])
    mod = _row_mod(i, geom, mhi_ref, mlo_ref)
    sh, sc = mod[:, :D], mod[:, D:]
    ms = jnp.mean(x * x, axis=-1, keepdims=True)
    h = x * lax.rsqrt(ms + EPS) * g_ref[...]
    hb = (h * (1.0 + sc) + sh).astype(BF16)

    def proj(a, b):
        return _dot(hb, wm_ref[:, a:b])

    qa_ref[...] = (proj(C_QA, C_KA) * GLA_DK ** -0.5).astype(BF16)
    ka_ref[...] = proj(C_KA, C_VA).astype(BF16)
    va_ref[...] = proj(C_VA, C_RA).astype(BF16)
    r = proj(C_RA, C_QB)
    ra_ref[...] = (r * _sigmoid(r)).astype(BF16)
    lr_hi, lr_lo = _split_bf16(_dot(hb, wlr_ref[...]))
    xg = _dot(lr_hi, w2_ref[...]) + _dot(lr_lo, w2_ref[...]) + bg_ref[...]
    ga_ref[...] = (jnp.minimum(xg, 0.0) - jnp.log1p(jnp.exp(-jnp.abs(xg)))) * (1.0 / GATE_NORM)
    q = proj(C_QB, C_FB)
    qb_ref[...] = (q * _sigmoid(q) * HG_DK ** -0.5).astype(BF16)
    lbl = lbl_ref[...]
    e = jnp.exp(lbl - jnp.max(lbl, axis=0, keepdims=True))
    lb = e[0:1, :] / jnp.sum(e, axis=0, keepdims=True)
    fb = proj(C_FB, C_IB)
    kb = (1.0 - lb) * _sigmoid(-fb)
    kb_ref[...] = kb.astype(BF16)
    gb_ref[...] = jnp.log(lb + (1.0 - lb) * _sigmoid(fb))
    ib_ref[...] = proj(C_IB, C_RB).astype(BF16)
    r = proj(C_RB, C_UA)
    rb_ref[...] = (r * _sigmoid(r)).astype(BF16)
    ua_ref[...] = _sigmoid(proj(C_UA, C_UB)).astype(BF16)
    ub_ref[...] = _sigmoid(proj(C_UB, C_END)).astype(BF16)


def _proj_call(geom, n_rows, xp, xs, mhi, mlo, g_mix, lbl, wm, wlr, w2, bg):
    n_tiles = n_rows // TM
    last_p = geom[0] // TM - 1
    const = lambda i: (0, 0)
    row = lambda i: (i, 0)
    widths = [(512, BF16), (512, BF16), (D, BF16), (D, BF16), (512, F32), (D, BF16), (D, BF16),
              (D, F32), (D, BF16), (D, BF16), (D, BF16), (D, BF16)]
    return pl.pallas_call(
        functools.partial(_proj_kernel, geom),
        grid=(n_tiles,),
        in_specs=[pl.BlockSpec((TM, D), lambda i: (jnp.minimum(i, last_p), 0)),
                  pl.BlockSpec(xs.shape, const),
                  pl.BlockSpec(mhi.shape, const), pl.BlockSpec(mlo.shape, const),
                  pl.BlockSpec((1, D), const), pl.BlockSpec(lbl.shape, const),
                  pl.BlockSpec(wm.shape, const, pipeline_mode=pl.Buffered(1)),
                  pl.BlockSpec(wlr.shape, const), pl.BlockSpec(w2.shape, const),
                  pl.BlockSpec(bg.shape, const)],
        out_specs=[pl.BlockSpec((TM, w), row) for w, _ in widths],
        out_shape=[jax.ShapeDtypeStruct((n_rows, w), dt) for w, dt in widths],
        compiler_params=_cparams(56),
        name="proj",
    )(xp, xs, mhi, mlo, g_mix, lbl, wm, wlr, w2, bg)


def _rec_chunk(tri, causal, g, q, k, v, r, n_heads, dk, dv, s_ref, gn, o_ref, r0, rows_out):
    g_hi, g_lo = _split_bf16(g)
    cum = _dot(tri, g_hi) + _dot(tri, g_lo)
    width = cum.shape[1]
    tot = cum[CHUNK - 1:CHUNK, :]
    refs = [jnp.zeros((1, width), F32)] + [cum[j * SUB - 1:j * SUB, :] for j in range(1, NSUB)]
    d = cum - jnp.concatenate([jnp.broadcast_to(b, (SUB, width)) for b in refs], axis=0)
    qn = q.astype(F32) * jnp.exp(d)
    kn = k.astype(F32) * jnp.exp(jnp.minimum(-d, EXP_CLAMP))
    sub = lambda x, j: x[j * SUB:(j + 1) * SUB, :]
    qs = jnp.concatenate([sub(qn, j) * jnp.exp(refs[j]) for j in range(NSUB)],
                         axis=0).astype(BF16)
    kd = jnp.concatenate([sub(kn, j) * jnp.exp(tot - refs[j]) for j in range(NSUB)], axis=0)
    et = jnp.exp(tot)
    zero_rows = lambda n: jnp.zeros((n, width), F32)
    q_to, k_of = [], []
    for j in range(NSUB):
        parts = [zero_rows(j * SUB)] if j else []
        parts += [sub(qn, i) * jnp.exp(refs[i] - refs[j]) if i > j else sub(qn, i)
                  for i in range(j, NSUB)]
        q_to.append(jnp.concatenate(parts, axis=0).astype(BF16))
        parts = ([zero_rows(j * SUB)] if j else []) + [sub(kn, j)]
        if j + 1 < NSUB:
            parts.append(zero_rows(CHUNK - (j + 1) * SUB))
        k_of.append(jnp.concatenate(parts, axis=0).astype(BF16))
    for h in range(n_heads):
        ks = slice(h * dk, (h + 1) * dk)
        vs = slice(h * dv, (h + 1) * dv)
        a = lax.dot_general(jnp.concatenate([x[:, ks] for x in q_to], axis=1),
                            jnp.concatenate([x[:, ks] for x in k_of], axis=1),
                            (((1,), (1,)), ((), ())), preferred_element_type=F32)
        a = jnp.where(causal, a, 0.0).astype(BF16)
        s = s_ref[0, h]
        kd_t = kd[:, ks].T.astype(BF16)
        av = _dot(jnp.concatenate([a, kd_t], axis=0), v[:, vs])
        o = _dot(qs[:, ks], s.astype(BF16)) + av[:CHUNK]
        et_col = jnp.broadcast_to(et[:, ks], (dk, dk)).T
        s_ref[0, h] = s * jnp.tile(et_col, (1, dv // dk)) + av[CHUNK:]
        ms = jnp.mean(o * o, axis=-1, keepdims=True)
        og = o * lax.rsqrt(ms + EPS) * gn * r[:, vs].astype(F32)
        o_ref[r0:r0 + rows_out, vs] = og[:rows_out].astype(BF16)


def _rec_kernel(rows, has_s0, *refs):
    (qa_ref, ka_ref, va_ref, ra_ref, ga_ref, qb_ref, kb_ref, gb_ref, ib_ref, rb_ref,
     gna_ref, gnb_ref) = refs[:12]
    refs = refs[12:]
    if has_s0:
        s0a_ref, s0b_ref = refs[:2]
        refs = refs[2:]
    oga_ref, ogb_ref, sa_ref, sb_ref = refs

    @pl.when(pl.program_id(1) == 0)
    def _():
        if has_s0:
            sa_ref[...] = s0a_ref[...]
            sb_ref[...] = s0b_ref[...]
        else:
            sa_ref[...] = jnp.zeros_like(sa_ref)
            sb_ref[...] = jnp.zeros_like(sb_ref)

    ti = lax.broadcasted_iota(I32, (CHUNK, CHUNK), 0)
    si = lax.broadcasted_iota(I32, (CHUNK, CHUNK), 1)
    causal = ti >= si
    tri = causal.astype(BF16)
    n_chunks = max(rows // CHUNK, 1)
    rows_out = min(rows, CHUNK)

    def load(ref, c):
        if rows >= CHUNK:
            return ref[c * CHUNK:(c + 1) * CHUNK, :]
        x = ref[...]
        return jnp.concatenate([x, jnp.zeros((CHUNK - rows, x.shape[1]), x.dtype)], axis=0)

    for c in range(n_chunks):
        _rec_chunk(tri, causal, load(ga_ref, c), load(qa_ref, c), load(ka_ref, c), load(va_ref, c),
                   load(ra_ref, c), GLA_H, GLA_DK, GLA_DV, sa_ref, gna_ref[...], oga_ref,
                   c * CHUNK, rows_out)
        _rec_chunk(tri, causal, load(gb_ref, c), load(qb_ref, c), load(kb_ref, c), load(ib_ref, c),
                   load(rb_ref, c), HG_H, HG_DK, HG_DV, sb_ref, gnb_ref[...], ogb_ref,
                   c * CHUNK, rows_out)


def _rec_call(arrs, gna, gnb, n_seq, seq_len, rows, row0, s0=None):
    steps = seq_len // rows
    blk0 = row0 // rows
    row_in = lambda b, t: (blk0 + b * steps + t, 0)
    row_out = lambda b, t: (b * steps + t, 0)
    const = lambda b, t: (0, 0)
    st = lambda b, t: (b, 0, 0, 0)
    in_specs = [pl.BlockSpec((rows, a.shape[1]), row_in) for a in arrs]
    in_specs += [pl.BlockSpec(gna.shape, const), pl.BlockSpec(gnb.shape, const)]
    args = list(arrs) + [gna, gnb]
    if s0 is not None:
        in_specs += [pl.BlockSpec((1, GLA_H, GLA_DK, GLA_DV), st),
                     pl.BlockSpec((1, HG_H, HG_DK, HG_DV), st)]
        args += [s0[0], s0[1]]
    n_rows = n_seq * seq_len
    return pl.pallas_call(
        functools.partial(_rec_kernel, rows, s0 is not None),
        grid=(n_seq, steps),
        in_specs=in_specs,
        out_specs=[pl.BlockSpec((rows, D), row_out), pl.BlockSpec((rows, D), row_out),
                   pl.BlockSpec((1, GLA_H, GLA_DK, GLA_DV), st),
                   pl.BlockSpec((1, HG_H, HG_DK, HG_DV), st)],
        out_shape=[jax.ShapeDtypeStruct((n_rows, D), BF16), jax.ShapeDtypeStruct((n_rows, D), BF16),
                   jax.ShapeDtypeStruct((n_seq, GLA_H, GLA_DK, GLA_DV), F32),
                   jax.ShapeDtypeStruct((n_seq, HG_H, HG_DK, HG_DV), F32)],
        compiler_params=pltpu.CompilerParams(dimension_semantics=("arbitrary", "arbitrary"),
                                             vmem_limit_bytes=48 * 1024 * 1024),
        name="rec_s0" if s0 is not None else "rec",
    )(*args)


def _post_kernel(geom, xp_ref, xs_ref, oap_ref, oas_ref, obp_ref, obs_ref, ua_ref, ub_ref,
                 mhi_ref, mlo_ref, gf_ref, wba_ref, wbb_ref, wo_ref, wrh_ref, wrl_ref, br_ref,
                 x1_ref, h2_ref, lg_ref):
    i = pl.program_id(0)
    is_prompt = i < geom[0] // TM
    x = jnp.where(is_prompt, xp_ref[...], xs_ref[...])
    oga = jnp.where(is_prompt, oap_ref[...], oas_ref[...])
    ogb = jnp.where(is_prompt, obp_ref[...], obs_ref[...])
    mod = _row_mod(i, geom, mhi_ref, mlo_ref)
    gt1, sh2, sc2 = mod[:, :D], mod[:, D:2 * D], mod[:, 2 * D:]
    merged = (ua_ref[...].astype(F32) * _dot(oga, wba_ref[...])
              + ub_ref[...].astype(F32) * _dot(ogb, wbb_ref[...]))
    x1 = x + gt1 * _dot(merged.astype(BF16), wo_ref[...])
    x1_ref[...] = x1
    ms = jnp.mean(x1 * x1, axis=-1, keepdims=True)
    h2 = x1 * lax.rsqrt(ms + EPS) * gf_ref[...] * (1.0 + sc2) + sh2
    h2_ref[...] = h2
    h_hi, h_lo = _split_bf16(h2)
    lg_ref[...] = (_dot(h_hi, wrh_ref[...]) + _dot(h_lo, wrh_ref[...]) + _dot(h_hi, wrl_ref[...])
                   + br_ref[...])


def _post_call(geom, n_rows, xp, xs, oap, oas, obp, obs, ua, ub, mhi, mlo, g_ffn, wba, wbb, wo,
               wrh, wrl, br):
    n_tiles = n_rows // TM
    last_p = geom[0] // TM - 1
    const = lambda i: (0, 0)
    row = lambda i: (i, 0)
    prow = pl.BlockSpec((TM, D), lambda i: (jnp.minimum(i, last_p), 0))
    full = lambda a: pl.BlockSpec(a.shape, const)
    return pl.pallas_call(
        functools.partial(_post_kernel, geom),
        grid=(n_tiles,),
        in_specs=[prow, full(xs), prow, full(oas), prow, full(obs),
                  pl.BlockSpec((TM, D), row), pl.BlockSpec((TM, D), row),
                  full(mhi), full(mlo), full(g_ffn), full(wba), full(wbb), full(wo),
                  full(wrh), full(wrl), full(br)],
        out_specs=[pl.BlockSpec((TM, D), row), pl.BlockSpec((TM, D), row),
                   pl.BlockSpec((TM, LANES), row)],
        out_shape=[jax.ShapeDtypeStruct((n_rows, D), F32), jax.ShapeDtypeStruct((n_rows, D), F32),
                   jax.ShapeDtypeStruct((n_rows, LANES), F32)],
        compiler_params=_cparams(48),
        name="post",
    )(xp, xs, oap, oas, obp, obs, ua, ub, mhi, mlo, g_ffn, wba, wbb, wo, wrh, wrl, br)


def _multi_hot(te):
    lane = lax.broadcasted_iota(I32, (TM, LANES), 1)
    m = jnp.zeros((TM, LANES), F32)
    for k in range(TOP_K):
        m = m + (lane == te[:, k:k + 1]).astype(F32)
    return m


def _route_kernel(lg_ref, te_ref, tw_ref, cnt_ref):
    logit = lg_ref[...]
    lane = lax.broadcasted_iota(I32, (ROUTE_ROWS, LANES), 1)
    lane_f = lane.astype(F32)
    vals, idxs = [], []
    for _ in range(TOP_K):
        m = jnp.max(logit, axis=-1, keepdims=True)
        idx = jnp.min(jnp.where(logit == m, lane_f, float(LANES)), axis=-1, keepdims=True)
        vals.append(m)
        idxs.append(idx)
        logit = jnp.where(lane_f == idx, -jnp.inf, logit)
    es = [jnp.exp(v - vals[0]) for v in vals]
    den = es[0] + es[1] + es[2] + es[3]
    te = jnp.zeros((ROUTE_ROWS, LANES), F32)
    tw = jnp.zeros((ROUTE_ROWS, LANES), F32)
    for k in range(TOP_K):
        te = jnp.where(lane == k, idxs[k], te)
        tw = jnp.where(lane == k, es[k] / den, tw)
    te = te.astype(I32)
    te_ref[...] = te
    tw_ref[...] = tw
    for t in range(ROUTE_ROWS // TM):
        cnt_ref[t * 8:(t + 1) * 8, :] = jnp.broadcast_to(
            jnp.sum(_multi_hot(te[t * TM:(t + 1) * TM]), axis=0, keepdims=True), (8, LANES))


def _route_call(logits, n_rows):
    n_steps = n_rows // ROUTE_ROWS
    row = lambda i: (i, 0)
    tiles = ROUTE_ROWS // TM
    return pl.pallas_call(
        _route_kernel,
        grid=(n_steps,),
        in_specs=[pl.BlockSpec((ROUTE_ROWS, LANES), row)],
        out_specs=[pl.BlockSpec((ROUTE_ROWS, LANES), row), pl.BlockSpec((ROUTE_ROWS, LANES), row),
                   pl.BlockSpec((tiles * 8, LANES), row)],
        out_shape=[jax.ShapeDtypeStruct((n_rows, LANES), I32),
                   jax.ShapeDtypeStruct((n_rows, LANES), F32),
                   jax.ShapeDtypeStruct((n_rows // TM * 8, LANES), F32)],
        compiler_params=_cparams(32),
        name="route",
    )(logits)


def _for_each_run_piece(tab_ref, fn):
    for e in range(N_EXP):
        length = tab_ref[0, 0, e]
        src = tab_ref[0, 0, N_EXP + e]
        dst = tab_ref[0, 0, 2 * N_EXP + e]
        off = jnp.int32(0)
        for p in RUN_PIECES:
            has = (length & p) != 0
            pl.when(has)(lambda off=off, p=p, src=src, dst=dst: fn(
                pl.multiple_of(src + off, RUN), pl.multiple_of(dst + off, RUN), p))
            off = off + jnp.where(has, p, 0)


def _dispatch_kernel(pend_ref, tab_ref, tprev_ref, te_ref, gsrc_ref, h2_ref, xs_ref, pos_ref, z_ref,
                     zeros, sem, zsem):
    @pl.when(pl.program_id(0) == 0)
    def _():
        zeros[...] = jnp.zeros_like(zeros)

        def zero_copy(e):
            start = pl.multiple_of(pend_ref[e] - BM, BM)
            return pltpu.make_async_copy(zeros, xs_ref.at[pl.ds(start, BM)], zsem)

        def nonempty(e):
            return pend_ref[e] > (pend_ref[e - 1] if e > 0 else 0)

        for e in range(N_EXP):
            pl.when(nonempty(e))(lambda e=e: zero_copy(e).start())
        for e in range(N_EXP):
            pl.when(nonempty(e))(lambda e=e: zero_copy(e).wait())

        def tail_copy(j):
            return pltpu.make_async_copy(zeros, xs_ref.at[pl.ds(pl.multiple_of(j * BM, BM), BM)],
                                         zsem)

        first, last = pend_ref[N_EXP - 1] // BM, xs_ref.shape[0] // BM
        lax.fori_loop(first, last, lambda j, c: (tail_copy(j).start(), c)[1], 0)
        lax.fori_loop(first, last, lambda j, c: (tail_copy(j).wait(), c)[1], 0)

    te = te_ref[...]
    ri = lax.broadcasted_iota(I32, (TM, TM), 0)
    ci = lax.broadcasted_iota(I32, (TM, TM), 1)
    rank = _dot((ri > ci).astype(BF16), _multi_hot(te).astype(BF16)) + gsrc_ref[0:1, :]
    lane = lax.broadcasted_iota(I32, (TM, LANES), 1)
    col = lax.broadcasted_iota(I32, (TM, SORT_ROWS), 1)
    pos = jnp.zeros((TM, LANES), F32)
    onehot = jnp.zeros((TM, SORT_ROWS), F32)
    for k in range(TOP_K):
        p = jnp.sum(jnp.where(lane == te[:, k:k + 1], rank, 0.0), axis=-1, keepdims=True)
        pos = jnp.where(lane == k, p, pos)
        onehot = jnp.where(col == p.astype(I32), 1.0, onehot)
    pos_ref[...] = pos.astype(I32)
    i = pl.program_id(0)
    slot = lax.rem(i, 2)
    z_ref[slot] = lax.dot_general(onehot.astype(BF16), h2_ref[...].astype(BF16),
                                  (((0,), (0,)), ((), ())), preferred_element_type=F32)

    _for_each_run_piece(tab_ref, lambda s, d, p: pltpu.make_async_copy(
        z_ref.at[slot, pl.ds(s, p)], xs_ref.at[pl.ds(d, p)], sem.at[slot]).start())

    def wait_tile(table_ref, which):
        def wait(j, c):
            pltpu.make_async_copy(z_ref.at[which, pl.ds(0, RUN)], xs_ref.at[pl.ds(0, RUN)],
                                  sem.at[which]).wait()
            return c

        lax.fori_loop(0, table_ref[0, 0, TAB_ROWS] >> _log2(RUN), wait, 0)

    pl.when(i > 0)(lambda: wait_tile(tprev_ref, 1 - slot))
    pl.when(i == pl.num_programs(0) - 1)(lambda: wait_tile(tab_ref, slot))


def _dispatch_call(pend, tab, te, gsrc, h2, cap):
    n_tiles = tab.shape[0]
    row = lambda i, pe: (i, 0)
    return pl.pallas_call(
        _dispatch_kernel,
        grid_spec=pltpu.PrefetchScalarGridSpec(
            num_scalar_prefetch=1,
            grid=(n_tiles,),
            in_specs=[pl.BlockSpec((1, 1, LANES), lambda i, pe: (i, 0, 0), memory_space=pltpu.SMEM),
                      pl.BlockSpec((1, 1, LANES), lambda i, pe: (jnp.maximum(i - 1, 0), 0, 0),
                                   memory_space=pltpu.SMEM),
                      pl.BlockSpec((TM, LANES), row), pl.BlockSpec((8, LANES), row),
                      pl.BlockSpec((TM, D), row)],
            out_specs=[pl.BlockSpec(memory_space=pl.ANY), pl.BlockSpec((TM, LANES), row)],
            scratch_shapes=[pltpu.VMEM((2, SORT_ROWS, D), F32), pltpu.VMEM((BM, D), F32),
                            pltpu.SemaphoreType.DMA((2,)), pltpu.SemaphoreType.DMA]),
        out_shape=[jax.ShapeDtypeStruct((cap, D), F32),
                   jax.ShapeDtypeStruct((n_tiles * TM, LANES), I32)],
        compiler_params=_cparams(40),
        name="dispatch",
    )(pend, tab, tab, te, gsrc, h2)


def _expert_kernel(be_ref, nv_ref, seg_ref, nxt_ref, x_ref, wgu_hbm, bgu_ref, wd_hbm, bd_ref, o_ref,
                   wgu_f, wd_f, wgu_s, wd_s, sem):
    j = pl.program_id(0)
    jc = jnp.minimum(j, nv_ref[0] - 1)
    e = be_ref[jc]
    slot = lax.rem(seg_ref[jc], 2)
    first = (j == 0) | ((j < nv_ref[0]) & (e != be_ref[jnp.maximum(jc - 1, 0)]))

    def fetch(expert, into):
        return (pltpu.make_async_copy(wgu_hbm.at[expert], wgu_f.at[into], sem.at[0, into]),
                pltpu.make_async_copy(wd_hbm.at[expert], wd_f.at[into], sem.at[1, into]))

    @pl.when(j == 0)
    def _():
        for c in fetch(e, slot):
            c.start()

    @pl.when(first)
    def _():
        for c in fetch(e, slot):
            c.wait()

        @pl.when(nxt_ref[jc] >= 0)
        def _():
            for c in fetch(nxt_ref[jc], 1 - slot):
                c.start()

        wgu_s[...] = wgu_f[slot].astype(BF16)
        wd_s[...] = wd_f[slot].astype(BF16)

    @pl.when(j < nv_ref[0])
    def _():
        gu = _dot(x_ref[...].astype(BF16), wgu_s[...]) + bgu_ref[0]
        gate = jnp.minimum(gu[:, :D], SWIGLU_LIMIT)
        up = jnp.clip(gu[:, D:], -SWIGLU_LIMIT, SWIGLU_LIMIT)
        act = (up + 1.0) * (gate * _sigmoid(SWIGLU_ALPHA * gate))
        o_ref[...] = _dot(act.astype(BF16), wd_s[...]) + bd_ref[0]

    @pl.when(j >= nv_ref[0])
    def _():
        o_ref[...] = jnp.zeros_like(o_ref)


def _expert_call(block_e, n_valid, seg, nxt, xs, wgu, bgu, wd, bd):
    n_blocks = xs.shape[0] // BM
    blk = lambda j, be, nv, sg, nx: (jnp.minimum(j, nv[0] - 1), 0)
    blk_out = lambda j, be, nv, sg, nx: (j, 0)
    exp = lambda j, be, nv, sg, nx: (be[jnp.minimum(j, nv[0] - 1)], 0, 0)
    return pl.pallas_call(
        _expert_kernel,
        grid_spec=pltpu.PrefetchScalarGridSpec(
            num_scalar_prefetch=4,
            grid=(n_blocks,),
            in_specs=[pl.BlockSpec((BM, D), blk),
                      pl.BlockSpec(memory_space=pl.ANY), pl.BlockSpec((1, 1, 2 * D), exp),
                      pl.BlockSpec(memory_space=pl.ANY), pl.BlockSpec((1, 1, D), exp)],
            out_specs=pl.BlockSpec((BM, D), blk_out),
            scratch_shapes=[pltpu.VMEM((2, D, 2 * D), F32), pltpu.VMEM((2, D, D), F32),
                            pltpu.VMEM((D, 2 * D), BF16), pltpu.VMEM((D, D), BF16),
                            pltpu.SemaphoreType.DMA((2, 2))]),
        out_shape=jax.ShapeDtypeStruct(xs.shape, F32),
        compiler_params=_cparams(56),
        name="expert",
    )(block_e, n_valid, seg, nxt, xs, wgu, bgu, wd, bd)


def _combine_kernel(geom, tcur_ref, tnxt_ref, eo_ref, pos_ref, x1_ref, tw_ref, mhi_ref, mlo_ref,
                    gfin_ref, yp_ref, ys_ref, buf, sem):
    i = pl.program_id(0)
    n_prompt_tiles = geom[0] // TM
    slot = lax.rem(i, 2)

    def gather(tab_ref, into):
        _for_each_run_piece(tab_ref, lambda s, d, p: pltpu.make_async_copy(
            eo_ref.at[pl.ds(d, p)], buf.at[into, pl.ds(s, p)], sem.at[into]).start())

    @pl.when(i == 0)
    def _():
        buf[...] = jnp.zeros_like(buf)
        gather(tcur_ref, 0)

    @pl.when(i + 1 < pl.num_programs(0))
    def _():
        gather(tnxt_ref, 1 - slot)

    def wait(j, c):
        pltpu.make_async_copy(eo_ref.at[pl.ds(0, RUN)], buf.at[slot, pl.ds(0, RUN)],
                              sem.at[slot]).wait()
        return c

    lax.fori_loop(0, tcur_ref[0, 0, TAB_ROWS] >> _log2(RUN), wait, 0)
    gt2 = _row_mod(i, geom, mhi_ref, mlo_ref)
    tw = tw_ref[...]
    pos = pos_ref[...]
    col = lax.broadcasted_iota(I32, (TM, SORT_ROWS), 1)
    wsel = jnp.zeros((TM, SORT_ROWS), F32)
    for k in range(TOP_K):
        wsel = jnp.where(col == pos[:, k:k + 1], tw[:, k:k + 1], wsel)
    y = _dot(wsel.astype(BF16), buf[slot].astype(BF16))
    x2 = x1_ref[...] + gt2 * y
    ms = jnp.mean(x2 * x2, axis=-1, keepdims=True)
    out = x2 * lax.rsqrt(ms + EPS) * gfin_ref[...]

    @pl.when(i < n_prompt_tiles)
    def _():
        yp_ref[...] = out

    @pl.when(i >= n_prompt_tiles)
    def _():
        ys_ref[...] = out


def _combine_call(geom, n_rows, tab, eo, pos, x1, tw, mhi, mlo, g_final):
    n_tiles = n_rows // TM
    n_prompt = geom[0]
    last_p = n_prompt // TM - 1
    const = lambda i: (0, 0)
    row = lambda i: (i, 0)
    return pl.pallas_call(
        functools.partial(_combine_kernel, geom),
        grid=(n_tiles,),
        in_specs=[pl.BlockSpec((1, 1, LANES), lambda i: (i, 0, 0), memory_space=pltpu.SMEM),
                  pl.BlockSpec((1, 1, LANES), lambda i: (jnp.minimum(i + 1, n_tiles - 1), 0, 0),
                               memory_space=pltpu.SMEM),
                  pl.BlockSpec(memory_space=pl.ANY), pl.BlockSpec((TM, LANES), row),
                  pl.BlockSpec((TM, D), row), pl.BlockSpec((TM, LANES), row),
                  pl.BlockSpec(mhi.shape, const), pl.BlockSpec(mlo.shape, const),
                  pl.BlockSpec((1, D), const)],
        out_specs=[pl.BlockSpec((TM, D), lambda i: (jnp.minimum(i, last_p), 0)),
                   pl.BlockSpec((n_rows - n_prompt, D), const)],
        out_shape=[jax.ShapeDtypeStruct((n_prompt, D), F32),
                   jax.ShapeDtypeStruct((n_rows - n_prompt, D), F32)],
        scratch_shapes=[pltpu.VMEM((2, SORT_ROWS, D), F32), pltpu.SemaphoreType.DMA((2,))],
        compiler_params=_cparams(48),
        name="combine",
    )(tab, tab, eo, pos, x1, tw, mhi, mlo, g_final)


def kernel(x_prompt, x_sample, c_prompt, c_sample, state_gla, state_hgrn, w_ada, b_ada, g_norm_mix,
           g_norm_ffn, w_in, w_gla_gate2, b_gla_gate, g_gla_onorm, hgrn_lb_logits, g_hgrn_onorm,
           w_branch_a, w_branch_b, w_out, w_router, b_router, w_gate_up, b_gate_up, w_down, b_down,
           g_final):
    assert w_ada.shape[0] == 1, "single-layer trunk only"
    bp, lp, _ = x_prompt.shape
    bs, ls, _ = x_sample.shape
    n_p, n_s = bp * lp, bs * ls
    n = n_p + n_s
    assert n_p % TM == 0 and n_s == TM and lp % (2 * CHUNK) == 0 and ls <= CHUNK and bp + bs <= 32
    assert SORT_ROWS >= TM * TOP_K + N_EXP * (RUN - 1) and RUN_PIECES[0] == TM
    assert n % ROUTE_ROWS == 0 and ROUTE_ROWS % TM == 0
    geom = (n_p, lp, ls, bp)
    xp = x_prompt.reshape(n_p, D)
    xs = x_sample.reshape(n_s, D)

    c_all = jnp.zeros((32, D), F32).at[:bp].set(c_prompt).at[bp:bp + bs].set(c_sample)
    mod = _mod_call(c_all, w_ada[0], b_ada[0].reshape(1, 6 * D))
    sh1, sc1, gt1, sh2, sc2, gt2 = [mod[:, j * D:(j + 1) * D] for j in range(6)]
    def table_and_sample_rows(t):
        return t, jnp.repeat(t[bp:bp + bs], ls, axis=0)

    m1_hi, m1_lo = table_and_sample_rows(jnp.concatenate([sh1, sc1], axis=1))
    m2_hi, m2_lo = table_and_sample_rows(jnp.concatenate([gt1, sh2, sc2], axis=1))
    m3_hi, m3_lo = table_and_sample_rows(gt2)

    wi = w_in[0]
    wm = jnp.concatenate([wi[:, :3072], wi[:, 3072 + GATE_RANK:]], axis=1).astype(BF16)
    wlr = jnp.pad(wi[:, 3072:3072 + GATE_RANK], ((0, 0), (0, LANES - GATE_RANK))).astype(BF16)
    w2 = jnp.pad(w_gla_gate2[0], ((0, LANES - GATE_RANK), (0, 0))).astype(BF16)
    arrs = _proj_call(geom, n, xp, xs, m1_hi, m1_lo, g_norm_mix[0].reshape(1, D), hgrn_lb_logits,
                      wm, wlr, w2, b_gla_gate[0].reshape(1, -1))
    qa, ka, va, ra, ga, qb, kb, gb, ib, rb, ua, ub = arrs
    rec_in = (qa, ka, va, ra, ga, qb, kb, gb, ib, rb)

    gna = g_gla_onorm[0].reshape(1, GLA_DV)
    gnb = g_hgrn_onorm[0].reshape(1, HG_DV)
    oap, obp, sa_p, sb_p = _rec_call(rec_in, gna, gnb, bp, lp, 2 * CHUNK, 0)
    oas, obs, sa_s, sb_s = _rec_call(rec_in, gna, gnb, bs, ls, ls, n_p,
                                     s0=(state_gla[0], state_hgrn[0]))

    wr = jnp.pad(w_router[0], ((0, 0), (0, LANES - N_EXP)))
    wr_hi, wr_lo = _split_bf16(wr)
    br = jnp.pad(b_router[0], (0, LANES - N_EXP), constant_values=NEG).reshape(1, LANES)
    x1, h2, logits = _post_call(geom, n, xp, xs, oap, oas, obp, obs, ua, ub, m2_hi, m2_lo,
                                g_norm_ffn[0].reshape(1, D), w_branch_a[0].astype(BF16),
                                w_branch_b[0].astype(BF16), w_out[0].astype(BF16), wr_hi, wr_lo, br)

    n_tiles = n // TM
    te, tw, cnt = _route_call(logits, n)
    run = (cnt[::8, :N_EXP].astype(I32) + RUN - 1) // RUN * RUN
    counts = jnp.sum(run, axis=0)
    seg_len = (counts + BM - 1) // BM * BM
    pend = jnp.cumsum(seg_len).astype(I32)
    run_dst = (pend - seg_len)[None, :] + jnp.cumsum(run, axis=0) - run
    run_src = jnp.cumsum(run, axis=1) - run
    tab = jnp.zeros((n_tiles, LANES), I32)
    tab = tab.at[:, :N_EXP].set(run).at[:, N_EXP:2 * N_EXP].set(run_src)
    tab = tab.at[:, 2 * N_EXP:3 * N_EXP].set(run_dst).at[:, TAB_ROWS].set(jnp.sum(run, axis=1))
    tab = tab.reshape(n_tiles, 1, LANES)
    gsrc = jnp.repeat(jnp.pad(run_src.astype(F32), ((0, 0), (0, LANES - N_EXP))), 8, axis=0)
    cap = (n * TOP_K + n_tiles * N_EXP * (RUN - 1) + N_EXP * (BM - 1) + BM - 1) // BM * BM
    block_start = jnp.arange(cap // BM, dtype=I32) * BM
    block_e = jnp.minimum(jnp.sum((pend[None, :] <= block_start[:, None]).astype(I32), axis=1),
                          N_EXP - 1)
    n_valid = pend[-1:] // BM
    ids = jnp.arange(N_EXP, dtype=I32)
    nonempty = counts > 0
    later = jnp.where(nonempty[None, :] & (ids[None, :] > ids[:, None]), ids[None, :], N_EXP)
    next_e = jnp.min(later, axis=1)
    next_e = jnp.where(next_e == N_EXP, -1, next_e)
    is_e = block_e[:, None] == ids[None, :]
    seg = jnp.sum(jnp.where(is_e, (jnp.cumsum(nonempty.astype(I32)) - 1)[None, :], 0), axis=1)
    nxt = jnp.sum(jnp.where(is_e, next_e[None, :], 0), axis=1)
    xs_sorted, pos = _dispatch_call(pend, tab, te, gsrc, h2, cap)
    eo = _expert_call(block_e, n_valid, seg, nxt, xs_sorted, w_gate_up[0],
                      b_gate_up[0].reshape(N_EXP, 1, -1), w_down[0], b_down[0].reshape(N_EXP, 1, -1))

    yp, ys = _combine_call(geom, n, tab, eo, pos, x1, tw, m3_hi, m3_lo, g_final.reshape(1, D))
    return (yp.reshape(bp, lp, D), ys.reshape(bs, ls, D),
            sa_p[None], sb_p[None], sa_s[None], sb_s[None])
```

```python
import functools

import jax
import jax.numpy as jnp
from jax import lax
from jax.experimental import pallas as pl
from jax.experimental.pallas import tpu as pltpu

F32 = jnp.float32
BF16 = jnp.bfloat16
I32 = jnp.int32

EPS = 1e-6
D = 1024
GLA_H, GLA_DK, GLA_DV = 4, 128, 256
HG_H, HG_DK, HG_DV = 8, 128, 128
GATE_RANK = 16
GATE_NORM = 16.0
N_EXP = 32
TOP_K = 4
SWIGLU_LIMIT = 7.0
SWIGLU_ALPHA = 1.702

LANES = 128
TM = 256
CHUNK = 128
SUB = 32
NSUB = CHUNK // SUB
EXP_CLAMP = 80.0
BM = 512
ROUTE_ROWS = 1280
RUN = 8
RUN_PIECES = (256, 128, 64, 32, 16, 8)
WAIT_ROWS = 128
SORT_ROWS = TM * TOP_K + 256
TAB_ROWS = 3 * N_EXP
NEG = -1e30

C_QA, C_KA, C_VA, C_RA, C_QB, C_FB, C_IB, C_RB, C_UA, C_UB, C_END = (
    0, 512, 1024, 2048, 3072, 4096, 5120, 6144, 7168, 8192, 9216)


def _sigmoid(x):
    return 1.0 / (1.0 + jnp.exp(-x))


def _dot(a, b):
    return jnp.dot(a, b, preferred_element_type=F32)


def _split_bf16(x):
    hi = x.astype(BF16)
    lo = (x - hi.astype(F32)).astype(BF16)
    return hi, lo


def _log2(n):
    assert n > 0 and n & (n - 1) == 0, "power of two expected"
    return n.bit_length() - 1


def _cparams(vmem_mb):
    return pltpu.CompilerParams(dimension_semantics=("arbitrary",),
                                vmem_limit_bytes=vmem_mb * 1024 * 1024)


def _mod_kernel(c_ref, w_ref, b_ref, o_ref):
    c = c_ref[...]
    s = (c * _sigmoid(c)).astype(BF16)
    o_ref[...] = _dot(s, w_ref[...].astype(BF16)) + b_ref[...]


def _mod_call(c_all, w_ada, b_ada):
    rows = c_all.shape[0]
    return pl.pallas_call(
        _mod_kernel,
        grid=(6,),
        in_specs=[pl.BlockSpec((rows, D), lambda j: (0, 0)),
                  pl.BlockSpec((D, D), lambda j: (0, j)),
                  pl.BlockSpec((1, D), lambda j: (0, j))],
        out_specs=pl.BlockSpec((rows, D), lambda j: (0, j)),
        out_shape=jax.ShapeDtypeStruct((rows, 6 * D), F32),
        compiler_params=_cparams(32),
        name="mod",
    )(c_all, w_ada, b_ada)


def _row_mod(i, geom, table_ref, sample_ref):
    n_prompt_rows, prompt_len, _, n_prompt_seq = geom
    seq = jnp.minimum((i * TM) >> _log2(prompt_len), n_prompt_seq - 1)
    return jnp.where(i < n_prompt_rows // TM, table_ref[pl.ds(seq, 1), :], sample_ref[...])


def _proj_kernel(geom, xp_ref, xs_ref, mhi_ref, mlo_ref, g_ref, lbl_ref, wm_ref, wlr_ref,
                 w2_ref, bg_ref,
                 qa_ref, ka_ref, va_ref, ra_ref, ga_ref, qb_ref, kb_ref, gb_ref, ib_ref,
                 rb_ref, ua_ref, ub_ref):
    i = pl.program_id(0)
    n_prompt_tiles = geom[0] // TM
    x = jnp.where(i < n_prompt_tiles, xp_ref[...], xs_ref[...])
    mod = _row_mod(i, geom, mhi_ref, mlo_ref)
    sh, sc = mod[:, :D], mod[:, D:]
    ms = jnp.mean(x * x, axis=-1, keepdims=True)
    h = x * lax.rsqrt(ms + EPS) * g_ref[...]
    hb = (h * (1.0 + sc) + sh).astype(BF16)

    def proj(a, b):
        return _dot(hb, wm_ref[:, a:b])

    qa_ref[...] = (proj(C_QA, C_KA) * GLA_DK ** -0.5).astype(BF16)
    ka_ref[...] = proj(C_KA, C_VA).astype(BF16)
    va_ref[...] = proj(C_VA, C_RA).astype(BF16)
    r = proj(C_RA, C_QB)
    ra_ref[...] = (r * _sigmoid(r)).astype(BF16)
    lr_hi, lr_lo = _split_bf16(_dot(hb, wlr_ref[...]))
    xg = _dot(lr_hi, w2_ref[...]) + _dot(lr_lo, w2_ref[...]) + bg_ref[...]
    ga_ref[...] = (jnp.minimum(xg, 0.0) - jnp.log1p(jnp.exp(-jnp.abs(xg)))) * (1.0 / GATE_NORM)
    q = proj(C_QB, C_FB)
    qb_ref[...] = (q * _sigmoid(q) * HG_DK ** -0.5).astype(BF16)
    lbl = lbl_ref[...]
    e = jnp.exp(lbl - jnp.max(lbl, axis=0, keepdims=True))
    lb = e[0:1, :] / jnp.sum(e, axis=0, keepdims=True)
    fb = proj(C_FB, C_IB)
    kb = (1.0 - lb) * _sigmoid(-fb)
    kb_ref[...] = kb.astype(BF16)
    gb_ref[...] = jnp.log(lb + (1.0 - lb) * _sigmoid(fb))
    ib_ref[...] = proj(C_IB, C_RB).astype(BF16)
    r = proj(C_RB, C_UA)
    rb_ref[...] = (r * _sigmoid(r)).astype(BF16)
    ua_ref[...] = _sigmoid(proj(C_UA, C_UB)).astype(BF16)
    ub_ref[...] = _sigmoid(proj(C_UB, C_END)).astype(BF16)


def _proj_call(geom, n_rows, xp, xs, mhi, mlo, g_mix, lbl, wm, wlr, w2, bg):
    n_tiles = n_rows // TM
    last_p = geom[0] // TM - 1
    const = lambda i: (0, 0)
    row = lambda i: (i, 0)
    widths = [(512, BF16), (512, BF16), (D, BF16), (D, BF16), (512, F32), (D, BF16), (D, BF16),
              (D, F32), (D, BF16), (D, BF16), (D, BF16), (D, BF16)]
    return pl.pallas_call(
        functools.partial(_proj_kernel, geom),
        grid=(n_tiles,),
        in_specs=[pl.BlockSpec((TM, D), lambda i: (jnp.minimum(i, last_p), 0)),
                  pl.BlockSpec(xs.shape, const),
                  pl.BlockSpec(mhi.shape, const), pl.BlockSpec(mlo.shape, const),
                  pl.BlockSpec((1, D), const), pl.BlockSpec(lbl.shape, const),
                  pl.BlockSpec(wm.shape, const, pipeline_mode=pl.Buffered(1)),
                  pl.BlockSpec(wlr.shape, const), pl.BlockSpec(w2.shape, const),
                  pl.BlockSpec(bg.shape, const)],
        out_specs=[pl.BlockSpec((TM, w), row) for w, _ in widths],
        out_shape=[jax.ShapeDtypeStruct((n_rows, w), dt) for w, dt in widths],
        compiler_params=_cparams(56),
        name="proj",
    )(xp, xs, mhi, mlo, g_mix, lbl, wm, wlr, w2, bg)


def _rec_chunk(tri, causal, g, q, k, v, r, n_heads, dk, dv, s_ref, gn, o_ref, r0, rows_out):
    g_hi, g_lo = _split_bf16(g)
    cum = _dot(tri, g_hi) + _dot(tri, g_lo)
    width = cum.shape[1]
    tot = cum[CHUNK - 1:CHUNK, :]
    refs = [jnp.zeros((1, width), F32)] + [cum[j * SUB - 1:j * SUB, :] for j in range(1, NSUB)]
    d = cum - jnp.concatenate([jnp.broadcast_to(b, (SUB, width)) for b in refs], axis=0)
    qn = q.astype(F32) * jnp.exp(d)
    kn = k.astype(F32) * jnp.exp(jnp.minimum(-d, EXP_CLAMP))
    sub = lambda x, j: x[j * SUB:(j + 1) * SUB, :]
    qs = jnp.concatenate([sub(qn, j) * jnp.exp(refs[j]) for j in range(NSUB)],
                         axis=0).astype(BF16)
    kd = jnp.concatenate([sub(kn, j) * jnp.exp(tot - refs[j]) for j in range(NSUB)], axis=0)
    et = jnp.exp(tot)
    zero_rows = lambda n: jnp.zeros((n, width), F32)
    q_to, k_of = [], []
    for j in range(NSUB):
        parts = [zero_rows(j * SUB)] if j else []
        parts += [sub(qn, i) * jnp.exp(refs[i] - refs[j]) if i > j else sub(qn, i)
                  for i in range(j, NSUB)]
        q_to.append(jnp.concatenate(parts, axis=0).astype(BF16))
        parts = ([zero_rows(j * SUB)] if j else []) + [sub(kn, j)]
        if j + 1 < NSUB:
            parts.append(zero_rows(CHUNK - (j + 1) * SUB))
        k_of.append(jnp.concatenate(parts, axis=0).astype(BF16))
    for h in range(n_heads):
        ks = slice(h * dk, (h + 1) * dk)
        vs = slice(h * dv, (h + 1) * dv)
        a = lax.dot_general(jnp.concatenate([x[:, ks] for x in q_to], axis=1),
                            jnp.concatenate([x[:, ks] for x in k_of], axis=1),
                            (((1,), (1,)), ((), ())), preferred_element_type=F32)
        a = jnp.where(causal, a, 0.0).astype(BF16)
        s = s_ref[0, h]
        kd_t = kd[:, ks].T.astype(BF16)
        av = _dot(jnp.concatenate([a, kd_t], axis=0), v[:, vs])
        o = _dot(qs[:, ks], s.astype(BF16)) + av[:CHUNK]
        et_col = jnp.broadcast_to(et[:, ks], (dk, dk)).T
        s_ref[0, h] = s * jnp.tile(et_col, (1, dv // dk)) + av[CHUNK:]
        ms = jnp.mean(o * o, axis=-1, keepdims=True)
        og = o * lax.rsqrt(ms + EPS) * gn * r[:, vs].astype(F32)
        o_ref[r0:r0 + rows_out, vs] = og[:rows_out].astype(BF16)


def _rec_kernel(rows, has_s0, *refs):
    (qa_ref, ka_ref, va_ref, ra_ref, ga_ref, qb_ref, kb_ref, gb_ref, ib_ref, rb_ref,
     gna_ref, gnb_ref) = refs[:12]
    refs = refs[12:]
    if has_s0:
        s0a_ref, s0b_ref = refs[:2]
        refs = refs[2:]
    oga_ref, ogb_ref, sa_ref, sb_ref = refs

    @pl.when(pl.program_id(1) == 0)
    def _():
        if has_s0:
            sa_ref[...] = s0a_ref[...]
            sb_ref[...] = s0b_ref[...]
        else:
            sa_ref[...] = jnp.zeros_like(sa_ref)
            sb_ref[...] = jnp.zeros_like(sb_ref)

    ti = lax.broadcasted_iota(I32, (CHUNK, CHUNK), 0)
    si = lax.broadcasted_iota(I32, (CHUNK, CHUNK), 1)
    causal = ti >= si
    tri = causal.astype(BF16)
    n_chunks = max(rows // CHUNK, 1)
    rows_out = min(rows, CHUNK)

    def load(ref, c):
        if rows >= CHUNK:
            return ref[c * CHUNK:(c + 1) * CHUNK, :]
        x = ref[...]
        return jnp.concatenate([x, jnp.zeros((CHUNK - rows, x.shape[1]), x.dtype)], axis=0)

    for c in range(n_chunks):
        _rec_chunk(tri, causal, load(ga_ref, c), load(qa_ref, c), load(ka_ref, c), load(va_ref, c),
                   load(ra_ref, c), GLA_H, GLA_DK, GLA_DV, sa_ref, gna_ref[...], oga_ref,
                   c * CHUNK, rows_out)
        _rec_chunk(tri, causal, load(gb_ref, c), load(qb_ref, c), load(kb_ref, c), load(ib_ref, c),
                   load(rb_ref, c), HG_H, HG_DK, HG_DV, sb_ref, gnb_ref[...], ogb_ref,
                   c * CHUNK, rows_out)


def _rec_call(arrs, gna, gnb, n_seq, seq_len, rows, row0, s0=None):
    steps = seq_len // rows
    blk0 = row0 // rows
    row_in = lambda b, t: (blk0 + b * steps + t, 0)
    row_out = lambda b, t: (b * steps + t, 0)
    const = lambda b, t: (0, 0)
    st = lambda b, t: (b, 0, 0, 0)
    in_specs = [pl.BlockSpec((rows, a.shape[1]), row_in) for a in arrs]
    in_specs += [pl.BlockSpec(gna.shape, const), pl.BlockSpec(gnb.shape, const)]
    args = list(arrs) + [gna, gnb]
    if s0 is not None:
        in_specs += [pl.BlockSpec((1, GLA_H, GLA_DK, GLA_DV), st),
                     pl.BlockSpec((1, HG_H, HG_DK, HG_DV), st)]
        args += [s0[0], s0[1]]
    n_rows = n_seq * seq_len
    return pl.pallas_call(
        functools.partial(_rec_kernel, rows, s0 is not None),
        grid=(n_seq, steps),
        in_specs=in_specs,
        out_specs=[pl.BlockSpec((rows, D), row_out), pl.BlockSpec((rows, D), row_out),
                   pl.BlockSpec((1, GLA_H, GLA_DK, GLA_DV), st),
                   pl.BlockSpec((1, HG_H, HG_DK, HG_DV), st)],
        out_shape=[jax.ShapeDtypeStruct((n_rows, D), BF16), jax.ShapeDtypeStruct((n_rows, D), BF16),
                   jax.ShapeDtypeStruct((n_seq, GLA_H, GLA_DK, GLA_DV), F32),
                   jax.ShapeDtypeStruct((n_seq, HG_H, HG_DK, HG_DV), F32)],
        compiler_params=pltpu.CompilerParams(dimension_semantics=("arbitrary", "arbitrary"),
                                             vmem_limit_bytes=48 * 1024 * 1024),
        name="rec_s0" if s0 is not None else "rec",
    )(*args)


def _post_kernel(geom, xp_ref, xs_ref, oap_ref, oas_ref, obp_ref, obs_ref, ua_ref, ub_ref,
                 mhi_ref, mlo_ref, gf_ref, wba_ref, wbb_ref, wo_ref, wrh_ref, wrl_ref, br_ref,
                 x1_ref, h2_ref, lg_ref):
    i = pl.program_id(0)
    is_prompt = i < geom[0] // TM
    x = jnp.where(is_prompt, xp_ref[...], xs_ref[...])
    oga = jnp.where(is_prompt, oap_ref[...], oas_ref[...])
    ogb = jnp.where(is_prompt, obp_ref[...], obs_ref[...])
    mod = _row_mod(i, geom, mhi_ref, mlo_ref)
    gt1, sh2, sc2 = mod[:, :D], mod[:, D:2 * D], mod[:, 2 * D:]
    merged = (ua_ref[...].astype(F32) * _dot(oga, wba_ref[...])
              + ub_ref[...].astype(F32) * _dot(ogb, wbb_ref[...]))
    x1 = x + gt1 * _dot(merged.astype(BF16), wo_ref[...])
    x1_ref[...] = x1
    ms = jnp.mean(x1 * x1, axis=-1, keepdims=True)
    h2 = x1 * lax.rsqrt(ms + EPS) * gf_ref[...] * (1.0 + sc2) + sh2
    h2_ref[...] = h2
    h_hi, h_lo = _split_bf16(h2)
    lg_ref[...] = (_dot(h_hi, wrh_ref[...]) + _dot(h_lo, wrh_ref[...]) + _dot(h_hi, wrl_ref[...])
                   + br_ref[...])


def _post_call(geom, n_rows, xp, xs, oap, oas, obp, obs, ua, ub, mhi, mlo, g_ffn, wba, wbb, wo,
               wrh, wrl, br):
    n_tiles = n_rows // TM
    last_p = geom[0] // TM - 1
    const = lambda i: (0, 0)
    row = lambda i: (i, 0)
    prow = pl.BlockSpec((TM, D), lambda i: (jnp.minimum(i, last_p), 0))
    full = lambda a: pl.BlockSpec(a.shape, const)
    return pl.pallas_call(
        functools.partial(_post_kernel, geom),
        grid=(n_tiles,),
        in_specs=[prow, full(xs), prow, full(oas), prow, full(obs),
                  pl.BlockSpec((TM, D), row), pl.BlockSpec((TM, D), row),
                  full(mhi), full(mlo), full(g_ffn), full(wba), full(wbb), full(wo),
                  full(wrh), full(wrl), full(br)],
        out_specs=[pl.BlockSpec((TM, D), row), pl.BlockSpec((TM, D), row),
                   pl.BlockSpec((TM, LANES), row)],
        out_shape=[jax.ShapeDtypeStruct((n_rows, D), F32), jax.ShapeDtypeStruct((n_rows, D), F32),
                   jax.ShapeDtypeStruct((n_rows, LANES), F32)],
        compiler_params=_cparams(48),
        name="post",
    )(xp, xs, oap, oas, obp, obs, ua, ub, mhi, mlo, g_ffn, wba, wbb, wo, wrh, wrl, br)


def _multi_hot(te):
    lane = lax.broadcasted_iota(I32, (TM, LANES), 1)
    m = jnp.zeros((TM, LANES), F32)
    for k in range(TOP_K):
        m = m + (lane == te[:, k:k + 1]).astype(F32)
    return m


def _route_kernel(lg_ref, te_ref, tw_ref, cnt_ref):
    logit = lg_ref[...]
    lane = lax.broadcasted_iota(I32, (ROUTE_ROWS, LANES), 1)
    lane_f = lane.astype(F32)
    vals, idxs = [], []
    for _ in range(TOP_K):
        m = jnp.max(logit, axis=-1, keepdims=True)
        idx = jnp.min(jnp.where(logit == m, lane_f, float(LANES)), axis=-1, keepdims=True)
        vals.append(m)
        idxs.append(idx)
        logit = jnp.where(lane_f == idx, -jnp.inf, logit)
    es = [jnp.exp(v - vals[0]) for v in vals]
    den = es[0] + es[1] + es[2] + es[3]
    te = jnp.zeros((ROUTE_ROWS, LANES), F32)
    tw = jnp.zeros((ROUTE_ROWS, LANES), F32)
    for k in range(TOP_K):
        te = jnp.where(lane == k, idxs[k], te)
        tw = jnp.where(lane == k, es[k] / den, tw)
    te = te.astype(I32)
    te_ref[...] = te
    tw_ref[...] = tw
    for t in range(ROUTE_ROWS // TM):
        cnt_ref[t * 8:(t + 1) * 8, :] = jnp.broadcast_to(
            jnp.sum(_multi_hot(te[t * TM:(t + 1) * TM]), axis=0, keepdims=True), (8, LANES))


def _route_call(logits, n_rows):
    n_steps = n_rows // ROUTE_ROWS
    row = lambda i: (i, 0)
    tiles = ROUTE_ROWS // TM
    return pl.pallas_call(
        _route_kernel,
        grid=(n_steps,),
        in_specs=[pl.BlockSpec((ROUTE_ROWS, LANES), row)],
        out_specs=[pl.BlockSpec((ROUTE_ROWS, LANES), row), pl.BlockSpec((ROUTE_ROWS, LANES), row),
                   pl.BlockSpec((tiles * 8, LANES), row)],
        out_shape=[jax.ShapeDtypeStruct((n_rows, LANES), I32),
                   jax.ShapeDtypeStruct((n_rows, LANES), F32),
                   jax.ShapeDtypeStruct((n_rows // TM * 8, LANES), F32)],
        compiler_params=_cparams(32),
        name="route",
    )(logits)


def _wait_rows(rows, wait_n_rows):
    lax.fori_loop(0, rows >> _log2(WAIT_ROWS), lambda j, c: (wait_n_rows(WAIT_ROWS), c)[1], 0)
    lax.fori_loop(0, (rows & (WAIT_ROWS - 1)) >> _log2(RUN),
                  lambda j, c: (wait_n_rows(RUN), c)[1], 0)


def _for_each_run_piece(tab_ref, fn):
    for e in range(N_EXP):
        length = tab_ref[0, 0, e]
        src = tab_ref[0, 0, N_EXP + e]
        dst = tab_ref[0, 0, 2 * N_EXP + e]
        off = jnp.int32(0)
        for p in RUN_PIECES:
            has = (length & p) != 0
            pl.when(has)(lambda off=off, p=p, src=src, dst=dst: fn(
                pl.multiple_of(src + off, RUN), pl.multiple_of(dst + off, RUN), p))
            off = off + jnp.where(has, p, 0)


def _dispatch_kernel(pend_ref, tab_ref, tprev_ref, te_ref, gsrc_ref, h2_ref, xs_ref, pos_ref, z_ref,
                     zeros, sem, zsem):
    @pl.when(pl.program_id(0) == 0)
    def _():
        zeros[...] = jnp.zeros_like(zeros)

        def zero_copy(e):
            start = pl.multiple_of(pend_ref[e] - BM, BM)
            return pltpu.make_async_copy(zeros, xs_ref.at[pl.ds(start, BM)], zsem)

        def nonempty(e):
            return pend_ref[e] > (pend_ref[e - 1] if e > 0 else 0)

        for e in range(N_EXP):
            pl.when(nonempty(e))(lambda e=e: zero_copy(e).start())
        for e in range(N_EXP):
            pl.when(nonempty(e))(lambda e=e: zero_copy(e).wait())

        def tail_copy(j):
            return pltpu.make_async_copy(zeros, xs_ref.at[pl.ds(pl.multiple_of(j * BM, BM), BM)],
                                         zsem)

        first, last = pend_ref[N_EXP - 1] // BM, xs_ref.shape[0] // BM
        lax.fori_loop(first, last, lambda j, c: (tail_copy(j).start(), c)[1], 0)
        lax.fori_loop(first, last, lambda j, c: (tail_copy(j).wait(), c)[1], 0)

    te = te_ref[...]
    ri = lax.broadcasted_iota(I32, (TM, TM), 0)
    ci = lax.broadcasted_iota(I32, (TM, TM), 1)
    rank = _dot((ri > ci).astype(BF16), _multi_hot(te).astype(BF16)) + gsrc_ref[0:1, :]
    lane = lax.broadcasted_iota(I32, (TM, LANES), 1)
    col = lax.broadcasted_iota(I32, (TM, SORT_ROWS), 1)
    pos = jnp.zeros((TM, LANES), F32)
    onehot = jnp.zeros((TM, SORT_ROWS), F32)
    for k in range(TOP_K):
        p = jnp.sum(jnp.where(lane == te[:, k:k + 1], rank, 0.0), axis=-1, keepdims=True)
        pos = jnp.where(lane == k, p, pos)
        onehot = jnp.where(col == p.astype(I32), 1.0, onehot)
    pos_ref[...] = pos.astype(I32)
    i = pl.program_id(0)
    slot = lax.rem(i, 2)
    z_ref[slot] = lax.dot_general(onehot.astype(BF16), h2_ref[...].astype(BF16),
                                  (((0,), (0,)), ((), ())), preferred_element_type=F32)

    _for_each_run_piece(tab_ref, lambda s, d, p: pltpu.make_async_copy(
        z_ref.at[slot, pl.ds(s, p)], xs_ref.at[pl.ds(d, p)], sem.at[slot]).start())

    def wait_tile(table_ref, which):
        _wait_rows(table_ref[0, 0, TAB_ROWS], lambda n: pltpu.make_async_copy(
            z_ref.at[which, pl.ds(0, n)], xs_ref.at[pl.ds(0, n)], sem.at[which]).wait())

    pl.when(i > 0)(lambda: wait_tile(tprev_ref, 1 - slot))
    pl.when(i == pl.num_programs(0) - 1)(lambda: wait_tile(tab_ref, slot))


def _dispatch_call(pend, tab, te, gsrc, h2, cap):
    n_tiles = tab.shape[0]
    row = lambda i, pe: (i, 0)
    return pl.pallas_call(
        _dispatch_kernel,
        grid_spec=pltpu.PrefetchScalarGridSpec(
            num_scalar_prefetch=1,
            grid=(n_tiles,),
            in_specs=[pl.BlockSpec((1, 1, LANES), lambda i, pe: (i, 0, 0), memory_space=pltpu.SMEM),
                      pl.BlockSpec((1, 1, LANES), lambda i, pe: (jnp.maximum(i - 1, 0), 0, 0),
                                   memory_space=pltpu.SMEM),
                      pl.BlockSpec((TM, LANES), row), pl.BlockSpec((8, LANES), row),
                      pl.BlockSpec((TM, D), row)],
            out_specs=[pl.BlockSpec(memory_space=pl.ANY), pl.BlockSpec((TM, LANES), row)],
            scratch_shapes=[pltpu.VMEM((2, SORT_ROWS, D), F32), pltpu.VMEM((BM, D), F32),
                            pltpu.SemaphoreType.DMA((2,)), pltpu.SemaphoreType.DMA]),
        out_shape=[jax.ShapeDtypeStruct((cap, D), F32),
                   jax.ShapeDtypeStruct((n_tiles * TM, LANES), I32)],
        compiler_params=_cparams(40),
        name="dispatch",
    )(pend, tab, tab, te, gsrc, h2)


def _expert_kernel(be_ref, nv_ref, seg_ref, nxt_ref, x_ref, wgu_hbm, bgu_ref, wd_hbm, bd_ref, o_ref,
                   wgu_f, wd_f, wgu_s, wd_s, sem):
    j = pl.program_id(0)
    jc = jnp.minimum(j, nv_ref[0] - 1)
    e = be_ref[jc]
    slot = lax.rem(seg_ref[jc], 2)
    first = (j == 0) | ((j < nv_ref[0]) & (e != be_ref[jnp.maximum(jc - 1, 0)]))

    def fetch(expert, into):
        return (pltpu.make_async_copy(wgu_hbm.at[expert], wgu_f.at[into], sem.at[0, into]),
                pltpu.make_async_copy(wd_hbm.at[expert], wd_f.at[into], sem.at[1, into]))

    @pl.when(j == 0)
    def _():
        for c in fetch(e, slot):
            c.start()

    @pl.when(first)
    def _():
        for c in fetch(e, slot):
            c.wait()

        @pl.when(nxt_ref[jc] >= 0)
        def _():
            for c in fetch(nxt_ref[jc], 1 - slot):
                c.start()

        wgu_s[...] = wgu_f[slot].astype(BF16)
        wd_s[...] = wd_f[slot].astype(BF16)

    @pl.when(j < nv_ref[0])
    def _():
        gu = _dot(x_ref[...].astype(BF16), wgu_s[...]) + bgu_ref[0]
        gate = jnp.minimum(gu[:, :D], SWIGLU_LIMIT)
        up = jnp.clip(gu[:, D:], -SWIGLU_LIMIT, SWIGLU_LIMIT)
        act = (up + 1.0) * (gate * _sigmoid(SWIGLU_ALPHA * gate))
        o_ref[...] = _dot(act.astype(BF16), wd_s[...]) + bd_ref[0]

    @pl.when(j >= nv_ref[0])
    def _():
        o_ref[...] = jnp.zeros_like(o_ref)


def _expert_call(block_e, n_valid, seg, nxt, xs, wgu, bgu, wd, bd):
    n_blocks = xs.shape[0] // BM
    blk = lambda j, be, nv, sg, nx: (jnp.minimum(j, nv[0] - 1), 0)
    blk_out = lambda j, be, nv, sg, nx: (j, 0)
    exp = lambda j, be, nv, sg, nx: (be[jnp.minimum(j, nv[0] - 1)], 0, 0)
    return pl.pallas_call(
        _expert_kernel,
        grid_spec=pltpu.PrefetchScalarGridSpec(
            num_scalar_prefetch=4,
            grid=(n_blocks,),
            in_specs=[pl.BlockSpec((BM, D), blk),
                      pl.BlockSpec(memory_space=pl.ANY), pl.BlockSpec((1, 1, 2 * D), exp),
                      pl.BlockSpec(memory_space=pl.ANY), pl.BlockSpec((1, 1, D), exp)],
            out_specs=pl.BlockSpec((BM, D), blk_out),
            scratch_shapes=[pltpu.VMEM((2, D, 2 * D), F32), pltpu.VMEM((2, D, D), F32),
                            pltpu.VMEM((D, 2 * D), BF16), pltpu.VMEM((D, D), BF16),
                            pltpu.SemaphoreType.DMA((2, 2))]),
        out_shape=jax.ShapeDtypeStruct(xs.shape, F32),
        compiler_params=_cparams(56),
        name="expert",
    )(block_e, n_valid, seg, nxt, xs, wgu, bgu, wd, bd)


def _combine_kernel(geom, tcur_ref, tnxt_ref, eo_ref, pos_ref, x1_ref, tw_ref, mhi_ref, mlo_ref,
                    gfin_ref, yp_ref, ys_ref, buf, sem):
    i = pl.program_id(0)
    n_prompt_tiles = geom[0] // TM
    slot = lax.rem(i, 2)

    def gather(tab_ref, into):
        _for_each_run_piece(tab_ref, lambda s, d, p: pltpu.make_async_copy(
            eo_ref.at[pl.ds(d, p)], buf.at[into, pl.ds(s, p)], sem.at[into]).start())

    @pl.when(i == 0)
    def _():
        buf[...] = jnp.zeros_like(buf)
        gather(tcur_ref, 0)

    @pl.when(i + 1 < pl.num_programs(0))
    def _():
        gather(tnxt_ref, 1 - slot)

    _wait_rows(tcur_ref[0, 0, TAB_ROWS], lambda n: pltpu.make_async_copy(
        eo_ref.at[pl.ds(0, n)], buf.at[slot, pl.ds(0, n)], sem.at[slot]).wait())
    gt2 = _row_mod(i, geom, mhi_ref, mlo_ref)
    tw = tw_ref[...]
    pos = pos_ref[...]
    col = lax.broadcasted_iota(I32, (TM, SORT_ROWS), 1)
    wsel = jnp.zeros((TM, SORT_ROWS), F32)
    for k in range(TOP_K):
        wsel = jnp.where(col == pos[:, k:k + 1], tw[:, k:k + 1], wsel)
    y = _dot(wsel.astype(BF16), buf[slot].astype(BF16))
    x2 = x1_ref[...] + gt2 * y
    ms = jnp.mean(x2 * x2, axis=-1, keepdims=True)
    out = x2 * lax.rsqrt(ms + EPS) * gfin_ref[...]

    @pl.when(i < n_prompt_tiles)
    def _():
        yp_ref[...] = out

    @pl.when(i >= n_prompt_tiles)
    def _():
        ys_ref[...] = out


def _combine_call(geom, n_rows, tab, eo, pos, x1, tw, mhi, mlo, g_final):
    n_tiles = n_rows // TM
    n_prompt = geom[0]
    last_p = n_prompt // TM - 1
    const = lambda i: (0, 0)
    row = lambda i: (i, 0)
    return pl.pallas_call(
        functools.partial(_combine_kernel, geom),
        grid=(n_tiles,),
        in_specs=[pl.BlockSpec((1, 1, LANES), lambda i: (i, 0, 0), memory_space=pltpu.SMEM),
                  pl.BlockSpec((1, 1, LANES), lambda i: (jnp.minimum(i + 1, n_tiles - 1), 0, 0),
                               memory_space=pltpu.SMEM),
                  pl.BlockSpec(memory_space=pl.ANY), pl.BlockSpec((TM, LANES), row),
                  pl.BlockSpec((TM, D), row), pl.BlockSpec((TM, LANES), row),
                  pl.BlockSpec(mhi.shape, const), pl.BlockSpec(mlo.shape, const),
                  pl.BlockSpec((1, D), const)],
        out_specs=[pl.BlockSpec((TM, D), lambda i: (jnp.minimum(i, last_p), 0)),
                   pl.BlockSpec((n_rows - n_prompt, D), const)],
        out_shape=[jax.ShapeDtypeStruct((n_prompt, D), F32),
                   jax.ShapeDtypeStruct((n_rows - n_prompt, D), F32)],
        scratch_shapes=[pltpu.VMEM((2, SORT_ROWS, D), F32), pltpu.SemaphoreType.DMA((2,))],
        compiler_params=_cparams(48),
        name="combine",
    )(tab, tab, eo, pos, x1, tw, mhi, mlo, g_final)


def kernel(x_prompt, x_sample, c_prompt, c_sample, state_gla, state_hgrn, w_ada, b_ada, g_norm_mix,
           g_norm_ffn, w_in, w_gla_gate2, b_gla_gate, g_gla_onorm, hgrn_lb_logits, g_hgrn_onorm,
           w_branch_a, w_branch_b, w_out, w_router, b_router, w_gate_up, b_gate_up, w_down, b_down,
           g_final):
    assert w_ada.shape[0] == 1, "single-layer trunk only"
    bp, lp, _ = x_prompt.shape
    bs, ls, _ = x_sample.shape
    n_p, n_s = bp * lp, bs * ls
    n = n_p + n_s
    assert n_p % TM == 0 and n_s == TM and lp % (2 * CHUNK) == 0 and ls <= CHUNK and bp + bs <= 32
    assert SORT_ROWS >= TM * TOP_K + N_EXP * (RUN - 1) and RUN_PIECES[0] == TM
    assert n % ROUTE_ROWS == 0 and ROUTE_ROWS % TM == 0
    geom = (n_p, lp, ls, bp)
    xp = x_prompt.reshape(n_p, D)
    xs = x_sample.reshape(n_s, D)

    c_all = jnp.zeros((32, D), F32).at[:bp].set(c_prompt).at[bp:bp + bs].set(c_sample)
    mod = _mod_call(c_all, w_ada[0], b_ada[0].reshape(1, 6 * D))
    sh1, sc1, gt1, sh2, sc2, gt2 = [mod[:, j * D:(j + 1) * D] for j in range(6)]
    def table_and_sample_rows(t):
        return t, jnp.repeat(t[bp:bp + bs], ls, axis=0)

    m1_hi, m1_lo = table_and_sample_rows(jnp.concatenate([sh1, sc1], axis=1))
    m2_hi, m2_lo = table_and_sample_rows(jnp.concatenate([gt1, sh2, sc2], axis=1))
    m3_hi, m3_lo = table_and_sample_rows(gt2)

    wi = w_in[0]
    wm = jnp.concatenate([wi[:, :3072], wi[:, 3072 + GATE_RANK:]], axis=1).astype(BF16)
    wlr = jnp.pad(wi[:, 3072:3072 + GATE_RANK], ((0, 0), (0, LANES - GATE_RANK))).astype(BF16)
    w2 = jnp.pad(w_gla_gate2[0], ((0, LANES - GATE_RANK), (0, 0))).astype(BF16)
    arrs = _proj_call(geom, n, xp, xs, m1_hi, m1_lo, g_norm_mix[0].reshape(1, D), hgrn_lb_logits,
                      wm, wlr, w2, b_gla_gate[0].reshape(1, -1))
    qa, ka, va, ra, ga, qb, kb, gb, ib, rb, ua, ub = arrs
    rec_in = (qa, ka, va, ra, ga, qb, kb, gb, ib, rb)

    gna = g_gla_onorm[0].reshape(1, GLA_DV)
    gnb = g_hgrn_onorm[0].reshape(1, HG_DV)
    oap, obp, sa_p, sb_p = _rec_call(rec_in, gna, gnb, bp, lp, 2 * CHUNK, 0)
    oas, obs, sa_s, sb_s = _rec_call(rec_in, gna, gnb, bs, ls, ls, n_p,
                                     s0=(state_gla[0], state_hgrn[0]))

    wr = jnp.pad(w_router[0], ((0, 0), (0, LANES - N_EXP)))
    wr_hi, wr_lo = _split_bf16(wr)
    br = jnp.pad(b_router[0], (0, LANES - N_EXP), constant_values=NEG).reshape(1, LANES)
    x1, h2, logits = _post_call(geom, n, xp, xs, oap, oas, obp, obs, ua, ub, m2_hi, m2_lo,
                                g_norm_ffn[0].reshape(1, D), w_branch_a[0].astype(BF16),
                                w_branch_b[0].astype(BF16), w_out[0].astype(BF16), wr_hi, wr_lo, br)

    n_tiles = n // TM
    te, tw, cnt = _route_call(logits, n)
    run = (cnt[::8, :N_EXP].astype(I32) + RUN - 1) // RUN * RUN
    counts = jnp.sum(run, axis=0)
    seg_len = (counts + BM - 1) // BM * BM
    pend = jnp.cumsum(seg_len).astype(I32)
    run_dst = (pend - seg_len)[None, :] + jnp.cumsum(run, axis=0) - run
    run_src = jnp.cumsum(run, axis=1) - run
    tab = jnp.zeros((n_tiles, LANES), I32)
    tab = tab.at[:, :N_EXP].set(run).at[:, N_EXP:2 * N_EXP].set(run_src)
    tab = tab.at[:, 2 * N_EXP:3 * N_EXP].set(run_dst).at[:, TAB_ROWS].set(jnp.sum(run, axis=1))
    tab = tab.reshape(n_tiles, 1, LANES)
    gsrc = jnp.repeat(jnp.pad(run_src.astype(F32), ((0, 0), (0, LANES - N_EXP))), 8, axis=0)
    cap = (n * TOP_K + n_tiles * N_EXP * (RUN - 1) + N_EXP * (BM - 1) + BM - 1) // BM * BM
    block_start = jnp.arange(cap // BM, dtype=I32) * BM
    block_e = jnp.minimum(jnp.sum((pend[None, :] <= block_start[:, None]).astype(I32), axis=1),
                          N_EXP - 1)
    n_valid = pend[-1:] // BM
    ids = jnp.arange(N_EXP, dtype=I32)
    nonempty = counts > 0
    later = jnp.where(nonempty[None, :] & (ids[None, :] > ids[:, None]), ids[None, :], N_EXP)
    next_e = jnp.min(later, axis=1)
    next_e = jnp.where(next_e == N_EXP, -1, next_e)
    is_e = block_e[:, None] == ids[None, :]
    seg = jnp.sum(jnp.where(is_e, (jnp.cumsum(nonempty.astype(I32)) - 1)[None, :], 0), axis=1)
    nxt = jnp.sum(jnp.where(is_e, next_e[None, :], 0), axis=1)
    xs_sorted, pos = _dispatch_call(pend, tab, te, gsrc, h2, cap)
    eo = _expert_call(block_e, n_valid, seg, nxt, xs_sorted, w_gate_up[0],
                      b_gate_up[0].reshape(N_EXP, 1, -1), w_down[0], b_down[0].reshape(N_EXP, 1, -1))

    yp, ys = _combine_call(geom, n, tab, eo, pos, x1, tw, m3_hi, m3_lo, g_final.reshape(1, D))
    return (yp.reshape(bp, lp, D), ys.reshape(bs, ls, D),
            sa_p[None], sb_p[None], sa_s[None], sb_s[None])
```

```python
import functools

import jax
import jax.numpy as jnp
from jax import lax
from jax.experimental import pallas as pl
from jax.experimental.pallas import tpu as pltpu

F32 = jnp.float32
BF16 = jnp.bfloat16
I32 = jnp.int32

EPS = 1e-6
D = 1024
GLA_H, GLA_DK, GLA_DV = 4, 128, 256
HG_H, HG_DK, HG_DV = 8, 128, 128
GATE_RANK = 16
GATE_NORM = 16.0
N_EXP = 32
TOP_K = 4
SWIGLU_LIMIT = 7.0
SWIGLU_ALPHA = 1.702

LANES = 128
TM = 256
CHUNK = 128
SUB = 32
NSUB = CHUNK // SUB
REC_ROWS = 2 * CHUNK
POST_ROWS = 128
EXP_CLAMP = 80.0
BM = 512
ROUTE_ROWS = 1280
RUN = 8
RUN_PIECES = (256, 128, 64, 32, 16, 8)
WAIT_ROWS = 128
SORT_ROWS = TM * TOP_K + 256
TAB_ROWS = 3 * N_EXP
NEG = -1e30

C_QA, C_KA, C_VA, C_RA, C_QB, C_FB, C_IB, C_RB, C_UA, C_UB, C_END = (
    0, 512, 1024, 2048, 3072, 4096, 5120, 6144, 7168, 8192, 9216)


def _sigmoid(x):
    return 1.0 / (1.0 + jnp.exp(-x))


def _dot(a, b):
    return jnp.dot(a, b, preferred_element_type=F32)


def _split_bf16(x):
    hi = x.astype(BF16)
    lo = (x - hi.astype(F32)).astype(BF16)
    return hi, lo


def _log2(n):
    assert n > 0 and n & (n - 1) == 0, "power of two expected"
    return n.bit_length() - 1


def _cparams(vmem_mb):
    return pltpu.CompilerParams(dimension_semantics=("arbitrary",),
                                vmem_limit_bytes=vmem_mb * 1024 * 1024)


def _mod_kernel(c_ref, w_ref, b_ref, o_ref):
    c = c_ref[...]
    s = (c * _sigmoid(c)).astype(BF16)
    o_ref[...] = _dot(s, w_ref[...].astype(BF16)) + b_ref[...]


def _mod_call(c_all, w_ada, b_ada):
    rows = c_all.shape[0]
    return pl.pallas_call(
        _mod_kernel,
        grid=(6,),
        in_specs=[pl.BlockSpec((rows, D), lambda j: (0, 0)),
                  pl.BlockSpec((D, D), lambda j: (0, j)),
                  pl.BlockSpec((1, D), lambda j: (0, j))],
        out_specs=pl.BlockSpec((rows, D), lambda j: (0, j)),
        out_shape=jax.ShapeDtypeStruct((rows, 6 * D), F32),
        compiler_params=_cparams(32),
        name="mod",
    )(c_all, w_ada, b_ada)


def _row_mod(i, geom, table_ref, sample_ref):
    n_prompt_rows, prompt_len, _, n_prompt_seq = geom
    seq = jnp.minimum((i * TM) >> _log2(prompt_len), n_prompt_seq - 1)
    return jnp.where(i < n_prompt_rows // TM, table_ref[pl.ds(seq, 1), :], sample_ref[...])


def _proj_kernel(geom, xp_ref, xs_ref, mhi_ref, mlo_ref, g_ref, lbl_ref, wm_ref, wlr_ref,
                 w2_ref, bg_ref,
                 qa_ref, ka_ref, va_ref, ra_ref, ga_ref, qb_ref, kb_ref, gb_ref, ib_ref,
                 rb_ref, ua_ref, ub_ref):
    i = pl.program_id(0)
    n_prompt_tiles = geom[0] // TM
    x = jnp.where(i < n_prompt_tiles, xp_ref[...], xs_ref[...])
    mod = _row_mod(i, geom, mhi_ref, mlo_ref)
    sh, sc = mod[:, :D], mod[:, D:]
    ms = jnp.mean(x * x, axis=-1, keepdims=True)
    h = x * lax.rsqrt(ms + EPS) * g_ref[...]
    hb = (h * (1.0 + sc) + sh).astype(BF16)

    def proj(a, b):
        return _dot(hb, wm_ref[:, a:b])

    cols = (C_QA, C_KA, C_VA, C_RA, C_QB, C_FB, C_IB, C_RB, C_UA, C_UB, C_END)
    z_qa, z_ka, z_va, z_ra, z_qb, z_fb, z_ib, z_rb, z_ua, z_ub = [
        proj(a, b) for a, b in zip(cols[:-1], cols[1:])]
    lr_hi, lr_lo = _split_bf16(_dot(hb, wlr_ref[...]))
    qa_ref[...] = (z_qa * GLA_DK ** -0.5).astype(BF16)
    ka_ref[...] = z_ka.astype(BF16)
    va_ref[...] = z_va.astype(BF16)
    ra_ref[...] = (z_ra * _sigmoid(z_ra)).astype(BF16)
    xg = _dot(lr_hi, w2_ref[...]) + _dot(lr_lo, w2_ref[...]) + bg_ref[...]
    ga_ref[...] = (jnp.minimum(xg, 0.0) - jnp.log1p(jnp.exp(-jnp.abs(xg)))) * (1.0 / GATE_NORM)
    qb_ref[...] = (z_qb * _sigmoid(z_qb) * HG_DK ** -0.5).astype(BF16)
    lbl = lbl_ref[...]
    e = jnp.exp(lbl - jnp.max(lbl, axis=0, keepdims=True))
    lb = e[0:1, :] / jnp.sum(e, axis=0, keepdims=True)
    kb_ref[...] = ((1.0 - lb) * _sigmoid(-z_fb)).astype(BF16)
    gb_ref[...] = jnp.log(lb + (1.0 - lb) * _sigmoid(z_fb))
    ib_ref[...] = z_ib.astype(BF16)
    rb_ref[...] = (z_rb * _sigmoid(z_rb)).astype(BF16)
    ua_ref[...] = _sigmoid(z_ua).astype(BF16)
    ub_ref[...] = _sigmoid(z_ub).astype(BF16)


def _proj_call(geom, n_rows, xp, xs, mhi, mlo, g_mix, lbl, wm, wlr, w2, bg):
    n_tiles = n_rows // TM
    last_p = geom[0] // TM - 1
    const = lambda i: (0, 0)
    row = lambda i: (i, 0)
    widths = [(512, BF16), (512, BF16), (D, BF16), (D, BF16), (512, F32), (D, BF16), (D, BF16),
              (D, F32), (D, BF16), (D, BF16), (D, BF16), (D, BF16)]
    return pl.pallas_call(
        functools.partial(_proj_kernel, geom),
        grid=(n_tiles,),
        in_specs=[pl.BlockSpec((TM, D), lambda i: (jnp.minimum(i, last_p), 0)),
                  pl.BlockSpec(xs.shape, const),
                  pl.BlockSpec(mhi.shape, const), pl.BlockSpec(mlo.shape, const),
                  pl.BlockSpec((1, D), const), pl.BlockSpec(lbl.shape, const),
                  pl.BlockSpec(wm.shape, const, pipeline_mode=pl.Buffered(1)),
                  pl.BlockSpec(wlr.shape, const), pl.BlockSpec(w2.shape, const),
                  pl.BlockSpec(bg.shape, const)],
        out_specs=[pl.BlockSpec((TM, w), row) for w, _ in widths],
        out_shape=[jax.ShapeDtypeStruct((n_rows, w), dt) for w, dt in widths],
        compiler_params=_cparams(56),
        name="proj",
    )(xp, xs, mhi, mlo, g_mix, lbl, wm, wlr, w2, bg)


def _rec_prep(tri, g, q, k):
    g_hi, g_lo = _split_bf16(g)
    cum = _dot(tri, g_hi) + _dot(tri, g_lo)
    width = cum.shape[1]
    tot = cum[CHUNK - 1:CHUNK, :]
    refs = [jnp.zeros((1, width), F32)] + [cum[j * SUB - 1:j * SUB, :] for j in range(1, NSUB)]
    d = cum - jnp.concatenate([jnp.broadcast_to(b, (SUB, width)) for b in refs], axis=0)
    qn = q.astype(F32) * jnp.exp(d)
    kn = k.astype(F32) * jnp.exp(jnp.minimum(-d, EXP_CLAMP))
    sub = lambda x, j: x[j * SUB:(j + 1) * SUB, :]
    qs = jnp.concatenate([sub(qn, j) * jnp.exp(refs[j]) for j in range(NSUB)],
                         axis=0).astype(BF16)
    kd = jnp.concatenate([sub(kn, j) * jnp.exp(tot - refs[j]) for j in range(NSUB)], axis=0)
    et = jnp.exp(tot)
    zero_rows = lambda n: jnp.zeros((n, width), F32)
    q_to, k_of = [], []
    for j in range(NSUB):
        parts = [zero_rows(j * SUB)] if j else []
        parts += [sub(qn, i) * jnp.exp(refs[i] - refs[j]) if i > j else sub(qn, i)
                  for i in range(j, NSUB)]
        q_to.append(jnp.concatenate(parts, axis=0).astype(BF16))
        parts = ([zero_rows(j * SUB)] if j else []) + [sub(kn, j)]
        if j + 1 < NSUB:
            parts.append(zero_rows(CHUNK - (j + 1) * SUB))
        k_of.append(jnp.concatenate(parts, axis=0).astype(BF16))
    return qs, kd, et, q_to, k_of


def _rec_kernel(rows, has_s0, *refs):
    (qa_ref, ka_ref, va_ref, ra_ref, ga_ref, qb_ref, kb_ref, gb_ref, ib_ref, rb_ref,
     gna_ref, gnb_ref) = refs[:12]
    refs = refs[12:]
    if has_s0:
        s0a_ref, s0b_ref = refs[:2]
        refs = refs[2:]
    oga_ref, ogb_ref, sa_ref, sb_ref = refs

    @pl.when(pl.program_id(1) == 0)
    def _():
        if has_s0:
            sa_ref[...] = s0a_ref[...]
            sb_ref[...] = s0b_ref[...]
        else:
            sa_ref[...] = jnp.zeros_like(sa_ref)
            sb_ref[...] = jnp.zeros_like(sb_ref)

    ti = lax.broadcasted_iota(I32, (CHUNK, CHUNK), 0)
    si = lax.broadcasted_iota(I32, (CHUNK, CHUNK), 1)
    causal = ti >= si
    tri = causal.astype(BF16)
    n_chunks = max(rows // CHUNK, 1)
    rows_out = min(rows, CHUNK)

    def load(ref, c):
        if rows >= CHUNK:
            return ref[c * CHUNK:(c + 1) * CHUNK, :]
        x = ref[...]
        return jnp.concatenate([x, jnp.zeros((CHUNK - rows, x.shape[1]), x.dtype)], axis=0)

    mixers = ((GLA_H, GLA_DK, GLA_DV, ga_ref, qa_ref, ka_ref, va_ref, ra_ref, sa_ref, gna_ref, oga_ref),
              (HG_H, HG_DK, HG_DV, gb_ref, qb_ref, kb_ref, ib_ref, rb_ref, sb_ref, gnb_ref, ogb_ref))
    units = []
    for c in range(n_chunks):
        for (nh, dk, dv, g_ref, q_ref, k_ref, v_ref, r_ref, s_ref, gn_ref, o_ref) in mixers:
            qs, kd, et, q_to, k_of = _rec_prep(tri, load(g_ref, c), load(q_ref, c), load(k_ref, c))
            for h in range(nh):
                ks = slice(h * dk, (h + 1) * dk)
                units.append(dict(
                    c=c, h=h, dk=dk, dv=dv, vs=slice(h * dv, (h + 1) * dv), s_ref=s_ref,
                    gn_ref=gn_ref, o_ref=o_ref, v_ref=v_ref, r_ref=r_ref,
                    qs=qs[:, ks], kd=kd[:, ks], et=et[:, ks],
                    q_to=jnp.concatenate([x[:, ks] for x in q_to], axis=1),
                    k_of=jnp.concatenate([x[:, ks] for x in k_of], axis=1)))
    for u in units:
        u["a"] = lax.dot_general(u["q_to"], u["k_of"], (((1,), (1,)), ((), ())),
                                 preferred_element_type=F32)
    for u in units:
        u["lhs"] = jnp.concatenate([jnp.where(causal, u["a"], 0.0).astype(BF16),
                                    u["kd"].T.astype(BF16)], axis=0)
    for u in units:
        u["av"] = _dot(u["lhs"], load(u["v_ref"], u["c"])[:, u["vs"]])
    for u in units:
        dk, dv, h = u["dk"], u["dv"], u["h"]
        s = u["s_ref"][0, h]
        o = _dot(u["qs"], s.astype(BF16)) + u["av"][:CHUNK]
        et_col = jnp.broadcast_to(u["et"], (dk, dk)).T
        u["s_ref"][0, h] = s * jnp.tile(et_col, (1, dv // dk)) + u["av"][CHUNK:]
        ms = jnp.mean(o * o, axis=-1, keepdims=True)
        og = (o * lax.rsqrt(ms + EPS) * u["gn_ref"][...]
              * load(u["r_ref"], u["c"])[:, u["vs"]].astype(F32))
        r0 = u["c"] * CHUNK
        u["o_ref"][r0:r0 + rows_out, u["vs"]] = og[:rows_out].astype(BF16)


def _rec_call(arrs, gna, gnb, n_seq, seq_len, rows, row0, s0=None):
    steps = seq_len // rows
    blk0 = row0 // rows
    row_in = lambda b, t: (blk0 + b * steps + t, 0)
    row_out = lambda b, t: (b * steps + t, 0)
    const = lambda b, t: (0, 0)
    st = lambda b, t: (b, 0, 0, 0)
    in_specs = [pl.BlockSpec((rows, a.shape[1]), row_in) for a in arrs]
    in_specs += [pl.BlockSpec(gna.shape, const), pl.BlockSpec(gnb.shape, const)]
    args = list(arrs) + [gna, gnb]
    if s0 is not None:
        in_specs += [pl.BlockSpec((1, GLA_H, GLA_DK, GLA_DV), st),
                     pl.BlockSpec((1, HG_H, HG_DK, HG_DV), st)]
        args += [s0[0], s0[1]]
    n_rows = n_seq * seq_len
    return pl.pallas_call(
        functools.partial(_rec_kernel, rows, s0 is not None),
        grid=(n_seq, steps),
        in_specs=in_specs,
        out_specs=[pl.BlockSpec((rows, D), row_out), pl.BlockSpec((rows, D), row_out),
                   pl.BlockSpec((1, GLA_H, GLA_DK, GLA_DV), st),
                   pl.BlockSpec((1, HG_H, HG_DK, HG_DV), st)],
        out_shape=[jax.ShapeDtypeStruct((n_rows, D), BF16), jax.ShapeDtypeStruct((n_rows, D), BF16),
                   jax.ShapeDtypeStruct((n_seq, GLA_H, GLA_DK, GLA_DV), F32),
                   jax.ShapeDtypeStruct((n_seq, HG_H, HG_DK, HG_DV), F32)],
        compiler_params=pltpu.CompilerParams(dimension_semantics=("arbitrary", "arbitrary"),
                                             vmem_limit_bytes=48 * 1024 * 1024),
        name="rec_s0" if s0 is not None else "rec",
    )(*args)


def _post_kernel(geom, xp_ref, xs_ref, oap_ref, oas_ref, obp_ref, obs_ref, ua_ref, ub_ref,
                 mhi_ref, mlo_ref, gf_ref, wba_ref, wbb_ref, wo_ref, wrh_ref, wrl_ref, br_ref,
                 x1_ref, h2_ref, lg_ref):
    i = pl.program_id(0)
    is_prompt = i < geom[0] // TM
    mod = _row_mod(i, geom, mhi_ref, mlo_ref)
    groups = [slice(j * POST_ROWS, (j + 1) * POST_ROWS) for j in range(TM // POST_ROWS)]
    pa = [_dot(jnp.where(is_prompt, oap_ref[g, :], oas_ref[g, :]), wba_ref[...]) for g in groups]
    pb = [_dot(jnp.where(is_prompt, obp_ref[g, :], obs_ref[g, :]), wbb_ref[...]) for g in groups]
    merged = [(ua_ref[g, :].astype(F32) * a + ub_ref[g, :].astype(F32) * b).astype(BF16)
              for g, a, b in zip(groups, pa, pb)]
    y = [_dot(m, wo_ref[...]) for m in merged]
    h2s = []
    for g, yg in zip(groups, y):
        x1 = jnp.where(is_prompt, xp_ref[g, :], xs_ref[g, :]) + mod[g, :D] * yg
        x1_ref[g, :] = x1
        ms = jnp.mean(x1 * x1, axis=-1, keepdims=True)
        h2 = x1 * lax.rsqrt(ms + EPS) * gf_ref[...] * (1.0 + mod[g, 2 * D:]) + mod[g, D:2 * D]
        h2_ref[g, :] = h2
        h2s.append(_split_bf16(h2))
    for g, (h_hi, h_lo) in zip(groups, h2s):
        lg_ref[g, :] = (_dot(h_hi, wrh_ref[...]) + _dot(h_lo, wrh_ref[...])
                        + _dot(h_hi, wrl_ref[...]) + br_ref[...])


def _post_call(geom, n_rows, xp, xs, oap, oas, obp, obs, ua, ub, mhi, mlo, g_ffn, wba, wbb, wo,
               wrh, wrl, br):
    n_tiles = n_rows // TM
    last_p = geom[0] // TM - 1
    const = lambda i: (0, 0)
    row = lambda i: (i, 0)
    prow = pl.BlockSpec((TM, D), lambda i: (jnp.minimum(i, last_p), 0))
    full = lambda a: pl.BlockSpec(a.shape, const)
    return pl.pallas_call(
        functools.partial(_post_kernel, geom),
        grid=(n_tiles,),
        in_specs=[prow, full(xs), prow, full(oas), prow, full(obs),
                  pl.BlockSpec((TM, D), row), pl.BlockSpec((TM, D), row),
                  full(mhi), full(mlo), full(g_ffn), full(wba), full(wbb), full(wo),
                  full(wrh), full(wrl), full(br)],
        out_specs=[pl.BlockSpec((TM, D), row), pl.BlockSpec((TM, D), row),
                   pl.BlockSpec((TM, LANES), row)],
        out_shape=[jax.ShapeDtypeStruct((n_rows, D), F32), jax.ShapeDtypeStruct((n_rows, D), F32),
                   jax.ShapeDtypeStruct((n_rows, LANES), F32)],
        compiler_params=_cparams(48),
        name="post",
    )(xp, xs, oap, oas, obp, obs, ua, ub, mhi, mlo, g_ffn, wba, wbb, wo, wrh, wrl, br)


def _multi_hot(te):
    lane = lax.broadcasted_iota(I32, (TM, LANES), 1)
    m = jnp.zeros((TM, LANES), F32)
    for k in range(TOP_K):
        m = m + (lane == te[:, k:k + 1]).astype(F32)
    return m


def _route_kernel(lg_ref, te_ref, tw_ref, cnt_ref):
    logit = lg_ref[...]
    lane = lax.broadcasted_iota(I32, (ROUTE_ROWS, LANES), 1)
    lane_f = lane.astype(F32)
    vals, idxs = [], []
    for _ in range(TOP_K):
        m = jnp.max(logit, axis=-1, keepdims=True)
        idx = jnp.min(jnp.where(logit == m, lane_f, float(LANES)), axis=-1, keepdims=True)
        vals.append(m)
        idxs.append(idx)
        logit = jnp.where(lane_f == idx, -jnp.inf, logit)
    es = [jnp.exp(v - vals[0]) for v in vals]
    den = es[0] + es[1] + es[2] + es[3]
    te = jnp.zeros((ROUTE_ROWS, LANES), F32)
    tw = jnp.zeros((ROUTE_ROWS, LANES), F32)
    for k in range(TOP_K):
        te = jnp.where(lane == k, idxs[k], te)
        tw = jnp.where(lane == k, es[k] / den, tw)
    te = te.astype(I32)
    te_ref[...] = te
    tw_ref[...] = tw
    for t in range(ROUTE_ROWS // TM):
        cnt_ref[t * 8:(t + 1) * 8, :] = jnp.broadcast_to(
            jnp.sum(_multi_hot(te[t * TM:(t + 1) * TM]), axis=0, keepdims=True), (8, LANES))


def _route_call(logits, n_rows):
    n_steps = n_rows // ROUTE_ROWS
    row = lambda i: (i, 0)
    tiles = ROUTE_ROWS // TM
    return pl.pallas_call(
        _route_kernel,
        grid=(n_steps,),
        in_specs=[pl.BlockSpec((ROUTE_ROWS, LANES), row)],
        out_specs=[pl.BlockSpec((ROUTE_ROWS, LANES), row), pl.BlockSpec((ROUTE_ROWS, LANES), row),
                   pl.BlockSpec((tiles * 8, LANES), row)],
        out_shape=[jax.ShapeDtypeStruct((n_rows, LANES), I32),
                   jax.ShapeDtypeStruct((n_rows, LANES), F32),
                   jax.ShapeDtypeStruct((n_rows // TM * 8, LANES), F32)],
        compiler_params=_cparams(32),
        name="route",
    )(logits)


def _wait_rows(rows, wait_n_rows):
    lax.fori_loop(0, rows >> _log2(WAIT_ROWS), lambda j, c: (wait_n_rows(WAIT_ROWS), c)[1], 0)
    lax.fori_loop(0, (rows & (WAIT_ROWS - 1)) >> _log2(RUN),
                  lambda j, c: (wait_n_rows(RUN), c)[1], 0)


def _for_each_run_piece(tab_ref, fn):
    for e in range(N_EXP):
        length = tab_ref[0, 0, e]
        src = tab_ref[0, 0, N_EXP + e]
        dst = tab_ref[0, 0, 2 * N_EXP + e]
        off = jnp.int32(0)
        for p in RUN_PIECES:
            has = (length & p) != 0
            pl.when(has)(lambda off=off, p=p, src=src, dst=dst: fn(
                pl.multiple_of(src + off, RUN), pl.multiple_of(dst + off, RUN), p))
            off = off + jnp.where(has, p, 0)


def _dispatch_kernel(pend_ref, tab_ref, tprev_ref, te_ref, gsrc_ref, h2_ref, xs_ref, pos_ref, z_ref,
                     zeros, sem, zsem):
    @pl.when(pl.program_id(0) == 0)
    def _():
        zeros[...] = jnp.zeros_like(zeros)

        def zero_copy(e):
            start = pl.multiple_of(pend_ref[e] - BM, BM)
            return pltpu.make_async_copy(zeros, xs_ref.at[pl.ds(start, BM)], zsem)

        def nonempty(e):
            return pend_ref[e] > (pend_ref[e - 1] if e > 0 else 0)

        for e in range(N_EXP):
            pl.when(nonempty(e))(lambda e=e: zero_copy(e).start())
        for e in range(N_EXP):
            pl.when(nonempty(e))(lambda e=e: zero_copy(e).wait())

        def tail_copy(j):
            return pltpu.make_async_copy(zeros, xs_ref.at[pl.ds(pl.multiple_of(j * BM, BM), BM)],
                                         zsem)

        first, last = pend_ref[N_EXP - 1] // BM, xs_ref.shape[0] // BM
        lax.fori_loop(first, last, lambda j, c: (tail_copy(j).start(), c)[1], 0)
        lax.fori_loop(first, last, lambda j, c: (tail_copy(j).wait(), c)[1], 0)

    te = te_ref[...]
    ri = lax.broadcasted_iota(I32, (TM, TM), 0)
    ci = lax.broadcasted_iota(I32, (TM, TM), 1)
    rank = _dot((ri > ci).astype(BF16), _multi_hot(te).astype(BF16)) + gsrc_ref[0:1, :]
    lane = lax.broadcasted_iota(I32, (TM, LANES), 1)
    pos = jnp.zeros((TM, LANES), F32)
    for k in range(TOP_K):
        p = jnp.sum(jnp.where(lane == te[:, k:k + 1], rank, 0.0), axis=-1, keepdims=True)
        pos = jnp.where(lane == k, p, pos)
    pos_ref[...] = pos.astype(I32)
    pos_t = pos.T.astype(I32)
    row = lax.broadcasted_iota(I32, (SORT_ROWS, TM), 0)
    onehot = jnp.zeros((SORT_ROWS, TM), F32)
    for k in range(TOP_K):
        onehot = jnp.where(row == pos_t[k:k + 1, :], 1.0, onehot)
    i = pl.program_id(0)
    slot = lax.rem(i, 2)
    z_ref[slot] = _dot(onehot.astype(BF16), h2_ref[...].astype(BF16))

    _for_each_run_piece(tab_ref, lambda s, d, p: pltpu.make_async_copy(
        z_ref.at[slot, pl.ds(s, p)], xs_ref.at[pl.ds(d, p)], sem.at[slot]).start())

    def wait_tile(table_ref, which):
        _wait_rows(table_ref[0, 0, TAB_ROWS], lambda n: pltpu.make_async_copy(
            z_ref.at[which, pl.ds(0, n)], xs_ref.at[pl.ds(0, n)], sem.at[which]).wait())

    pl.when(i > 0)(lambda: wait_tile(tprev_ref, 1 - slot))
    pl.when(i == pl.num_programs(0) - 1)(lambda: wait_tile(tab_ref, slot))


def _dispatch_call(pend, tab, te, gsrc, h2, cap):
    n_tiles = tab.shape[0]
    row = lambda i, pe: (i, 0)
    return pl.pallas_call(
        _dispatch_kernel,
        grid_spec=pltpu.PrefetchScalarGridSpec(
            num_scalar_prefetch=1,
            grid=(n_tiles,),
            in_specs=[pl.BlockSpec((1, 1, LANES), lambda i, pe: (i, 0, 0), memory_space=pltpu.SMEM),
                      pl.BlockSpec((1, 1, LANES), lambda i, pe: (jnp.maximum(i - 1, 0), 0, 0),
                                   memory_space=pltpu.SMEM),
                      pl.BlockSpec((TM, LANES), row), pl.BlockSpec((8, LANES), row),
                      pl.BlockSpec((TM, D), row)],
            out_specs=[pl.BlockSpec(memory_space=pl.ANY), pl.BlockSpec((TM, LANES), row)],
            scratch_shapes=[pltpu.VMEM((2, SORT_ROWS, D), F32), pltpu.VMEM((BM, D), F32),
                            pltpu.SemaphoreType.DMA((2,)), pltpu.SemaphoreType.DMA]),
        out_shape=[jax.ShapeDtypeStruct((cap, D), F32),
                   jax.ShapeDtypeStruct((n_tiles * TM, LANES), I32)],
        compiler_params=_cparams(40),
        name="dispatch",
    )(pend, tab, tab, te, gsrc, h2)


def _expert_kernel(be_ref, nv_ref, seg_ref, nxt_ref, x_ref, wgu_hbm, bgu_ref, wd_hbm, bd_ref, o_ref,
                   wgu_f, wd_f, wgu_s, wd_s, sem):
    j = pl.program_id(0)
    jc = jnp.minimum(j, nv_ref[0] - 1)
    e = be_ref[jc]
    slot = lax.rem(seg_ref[jc], 2)
    first = (j == 0) | ((j < nv_ref[0]) & (e != be_ref[jnp.maximum(jc - 1, 0)]))

    def fetch(expert, into):
        return (pltpu.make_async_copy(wgu_hbm.at[expert], wgu_f.at[into], sem.at[0, into]),
                pltpu.make_async_copy(wd_hbm.at[expert], wd_f.at[into], sem.at[1, into]))

    @pl.when(j == 0)
    def _():
        for c in fetch(e, slot):
            c.start()

    @pl.when(first)
    def _():
        for c in fetch(e, slot):
            c.wait()

        @pl.when(nxt_ref[jc] >= 0)
        def _():
            for c in fetch(nxt_ref[jc], 1 - slot):
                c.start()

        wgu_s[...] = wgu_f[slot].astype(BF16)
        wd_s[...] = wd_f[slot].astype(BF16)

    @pl.when(j < nv_ref[0])
    def _():
        groups = [slice(j * (BM // 2), (j + 1) * (BM // 2)) for j in range(2)]
        gus = [_dot(x_ref[g, :].astype(BF16), wgu_s[...]) + bgu_ref[0] for g in groups]
        acts = []
        for gu in gus:
            gate = jnp.minimum(gu[:, :D], SWIGLU_LIMIT)
            up = jnp.clip(gu[:, D:], -SWIGLU_LIMIT, SWIGLU_LIMIT)
            acts.append(((up + 1.0) * (gate * _sigmoid(SWIGLU_ALPHA * gate))).astype(BF16))
        for g, act in zip(groups, acts):
            o_ref[g, :] = _dot(act, wd_s[...]) + bd_ref[0]

    @pl.when(j >= nv_ref[0])
    def _():
        o_ref[...] = jnp.zeros_like(o_ref)


def _expert_call(block_e, n_valid, seg, nxt, xs, wgu, bgu, wd, bd):
    n_blocks = xs.shape[0] // BM
    blk = lambda j, be, nv, sg, nx: (jnp.minimum(j, nv[0] - 1), 0)
    blk_out = lambda j, be, nv, sg, nx: (j, 0)
    exp = lambda j, be, nv, sg, nx: (be[jnp.minimum(j, nv[0] - 1)], 0, 0)
    return pl.pallas_call(
        _expert_kernel,
        grid_spec=pltpu.PrefetchScalarGridSpec(
            num_scalar_prefetch=4,
            grid=(n_blocks,),
            in_specs=[pl.BlockSpec((BM, D), blk),
                      pl.BlockSpec(memory_space=pl.ANY), pl.BlockSpec((1, 1, 2 * D), exp),
                      pl.BlockSpec(memory_space=pl.ANY), pl.BlockSpec((1, 1, D), exp)],
            out_specs=pl.BlockSpec((BM, D), blk_out),
            scratch_shapes=[pltpu.VMEM((2, D, 2 * D), F32), pltpu.VMEM((2, D, D), F32),
                            pltpu.VMEM((D, 2 * D), BF16), pltpu.VMEM((D, D), BF16),
                            pltpu.SemaphoreType.DMA((2, 2))]),
        out_shape=jax.ShapeDtypeStruct(xs.shape, F32),
        compiler_params=_cparams(56),
        name="expert",
    )(block_e, n_valid, seg, nxt, xs, wgu, bgu, wd, bd)


def _combine_kernel(geom, tcur_ref, tnxt_ref, eo_ref, pos_ref, x1_ref, tw_ref, mhi_ref, mlo_ref,
                    gfin_ref, yp_ref, ys_ref, buf, sem):
    i = pl.program_id(0)
    n_prompt_tiles = geom[0] // TM
    slot = lax.rem(i, 2)

    def gather(tab_ref, into):
        _for_each_run_piece(tab_ref, lambda s, d, p: pltpu.make_async_copy(
            eo_ref.at[pl.ds(d, p)], buf.at[into, pl.ds(s, p)], sem.at[into]).start())

    @pl.when(i == 0)
    def _():
        buf[...] = jnp.zeros_like(buf)
        gather(tcur_ref, 0)

    @pl.when(i + 1 < pl.num_programs(0))
    def _():
        gather(tnxt_ref, 1 - slot)

    _wait_rows(tcur_ref[0, 0, TAB_ROWS], lambda n: pltpu.make_async_copy(
        eo_ref.at[pl.ds(0, n)], buf.at[slot, pl.ds(0, n)], sem.at[slot]).wait())
    gt2 = _row_mod(i, geom, mhi_ref, mlo_ref)
    tw = tw_ref[...]
    pos = pos_ref[...]
    col = lax.broadcasted_iota(I32, (TM, SORT_ROWS), 1)
    wsel = jnp.zeros((TM, SORT_ROWS), F32)
    for k in range(TOP_K):
        wsel = jnp.where(col == pos[:, k:k + 1], tw[:, k:k + 1], wsel)
    y = _dot(wsel.astype(BF16), buf[slot].astype(BF16))
    x2 = x1_ref[...] + gt2 * y
    ms = jnp.mean(x2 * x2, axis=-1, keepdims=True)
    out = x2 * lax.rsqrt(ms + EPS) * gfin_ref[...]

    @pl.when(i < n_prompt_tiles)
    def _():
        yp_ref[...] = out

    @pl.when(i >= n_prompt_tiles)
    def _():
        ys_ref[...] = out


def _combine_call(geom, n_rows, tab, eo, pos, x1, tw, mhi, mlo, g_final):
    n_tiles = n_rows // TM
    n_prompt = geom[0]
    last_p = n_prompt // TM - 1
    const = lambda i: (0, 0)
    row = lambda i: (i, 0)
    return pl.pallas_call(
        functools.partial(_combine_kernel, geom),
        grid=(n_tiles,),
        in_specs=[pl.BlockSpec((1, 1, LANES), lambda i: (i, 0, 0), memory_space=pltpu.SMEM),
                  pl.BlockSpec((1, 1, LANES), lambda i: (jnp.minimum(i + 1, n_tiles - 1), 0, 0),
                               memory_space=pltpu.SMEM),
                  pl.BlockSpec(memory_space=pl.ANY), pl.BlockSpec((TM, LANES), row),
                  pl.BlockSpec((TM, D), row), pl.BlockSpec((TM, LANES), row),
                  pl.BlockSpec(mhi.shape, const), pl.BlockSpec(mlo.shape, const),
                  pl.BlockSpec((1, D), const)],
        out_specs=[pl.BlockSpec((TM, D), lambda i: (jnp.minimum(i, last_p), 0)),
                   pl.BlockSpec((n_rows - n_prompt, D), const)],
        out_shape=[jax.ShapeDtypeStruct((n_prompt, D), F32),
                   jax.ShapeDtypeStruct((n_rows - n_prompt, D), F32)],
        scratch_shapes=[pltpu.VMEM((2, SORT_ROWS, D), F32), pltpu.SemaphoreType.DMA((2,))],
        compiler_params=_cparams(48),
        name="combine",
    )(tab, tab, eo, pos, x1, tw, mhi, mlo, g_final)


def kernel(x_prompt, x_sample, c_prompt, c_sample, state_gla, state_hgrn, w_ada, b_ada, g_norm_mix,
           g_norm_ffn, w_in, w_gla_gate2, b_gla_gate, g_gla_onorm, hgrn_lb_logits, g_hgrn_onorm,
           w_branch_a, w_branch_b, w_out, w_router, b_router, w_gate_up, b_gate_up, w_down, b_down,
           g_final):
    assert w_ada.shape[0] == 1, "single-layer trunk only"
    bp, lp, _ = x_prompt.shape
    bs, ls, _ = x_sample.shape
    n_p, n_s = bp * lp, bs * ls
    n = n_p + n_s
    assert n_p % TM == 0 and n_s == TM and lp % TM == 0 and lp % REC_ROWS == 0 and ls <= CHUNK
    assert bp + bs <= 32
    assert SORT_ROWS >= TM * TOP_K + N_EXP * (RUN - 1) and RUN_PIECES[0] == TM
    assert n % ROUTE_ROWS == 0 and ROUTE_ROWS % TM == 0
    geom = (n_p, lp, ls, bp)
    xp = x_prompt.reshape(n_p, D)
    xs = x_sample.reshape(n_s, D)

    c_all = jnp.zeros((32, D), F32).at[:bp].set(c_prompt).at[bp:bp + bs].set(c_sample)
    mod = _mod_call(c_all, w_ada[0], b_ada[0].reshape(1, 6 * D))
    sh1, sc1, gt1, sh2, sc2, gt2 = [mod[:, j * D:(j + 1) * D] for j in range(6)]
    def table_and_sample_rows(t):
        return t, jnp.repeat(t[bp:bp + bs], ls, axis=0)

    m1_hi, m1_lo = table_and_sample_rows(jnp.concatenate([sh1, sc1], axis=1))
    m2_hi, m2_lo = table_and_sample_rows(jnp.concatenate([gt1, sh2, sc2], axis=1))
    m3_hi, m3_lo = table_and_sample_rows(gt2)

    wi = w_in[0]
    wm = jnp.concatenate([wi[:, :3072], wi[:, 3072 + GATE_RANK:]], axis=1).astype(BF16)
    wlr = jnp.pad(wi[:, 3072:3072 + GATE_RANK], ((0, 0), (0, LANES - GATE_RANK))).astype(BF16)
    w2 = jnp.pad(w_gla_gate2[0], ((0, LANES - GATE_RANK), (0, 0))).astype(BF16)
    arrs = _proj_call(geom, n, xp, xs, m1_hi, m1_lo, g_norm_mix[0].reshape(1, D), hgrn_lb_logits,
                      wm, wlr, w2, b_gla_gate[0].reshape(1, -1))
    qa, ka, va, ra, ga, qb, kb, gb, ib, rb, ua, ub = arrs
    rec_in = (qa, ka, va, ra, ga, qb, kb, gb, ib, rb)

    gna = g_gla_onorm[0].reshape(1, GLA_DV)
    gnb = g_hgrn_onorm[0].reshape(1, HG_DV)
    oap, obp, sa_p, sb_p = _rec_call(rec_in, gna, gnb, bp, lp, REC_ROWS, 0)
    oas, obs, sa_s, sb_s = _rec_call(rec_in, gna, gnb, bs, ls, ls, n_p,
                                     s0=(state_gla[0], state_hgrn[0]))

    wr = jnp.pad(w_router[0], ((0, 0), (0, LANES - N_EXP)))
    wr_hi, wr_lo = _split_bf16(wr)
    br = jnp.pad(b_router[0], (0, LANES - N_EXP), constant_values=NEG).reshape(1, LANES)
    x1, h2, logits = _post_call(geom, n, xp, xs, oap, oas, obp, obs, ua, ub, m2_hi, m2_lo,
                                g_norm_ffn[0].reshape(1, D), w_branch_a[0].astype(BF16),
                                w_branch_b[0].astype(BF16), w_out[0].astype(BF16), wr_hi, wr_lo, br)

    n_tiles = n // TM
    te, tw, cnt = _route_call(logits, n)
    run = (cnt[::8, :N_EXP].astype(I32) + RUN - 1) // RUN * RUN
    counts = jnp.sum(run, axis=0)
    seg_len = (counts + BM - 1) // BM * BM
    pend = jnp.cumsum(seg_len).astype(I32)
    run_dst = (pend - seg_len)[None, :] + jnp.cumsum(run, axis=0) - run
    run_src = jnp.cumsum(run, axis=1) - run
    tab = jnp.zeros((n_tiles, LANES), I32)
    tab = tab.at[:, :N_EXP].set(run).at[:, N_EXP:2 * N_EXP].set(run_src)
    tab = tab.at[:, 2 * N_EXP:3 * N_EXP].set(run_dst).at[:, TAB_ROWS].set(jnp.sum(run, axis=1))
    tab = tab.reshape(n_tiles, 1, LANES)
    gsrc = jnp.repeat(jnp.pad(run_src.astype(F32), ((0, 0), (0, LANES - N_EXP))), 8, axis=0)
    cap = (n * TOP_K + n_tiles * N_EXP * (RUN - 1) + N_EXP * (BM - 1) + BM - 1) // BM * BM
    block_start = jnp.arange(cap // BM, dtype=I32) * BM
    block_e = jnp.minimum(jnp.sum((pend[None, :] <= block_start[:, None]).astype(I32), axis=1),
                          N_EXP - 1)
    n_valid = pend[-1:] // BM
    ids = jnp.arange(N_EXP, dtype=I32)
    nonempty = counts > 0
    later = jnp.where(nonempty[None, :] & (ids[None, :] > ids[:, None]), ids[None, :], N_EXP)
    next_e = jnp.min(later, axis=1)
    next_e = jnp.where(next_e == N_EXP, -1, next_e)
    is_e = block_e[:, None] == ids[None, :]
    seg = jnp.sum(jnp.where(is_e, (jnp.cumsum(nonempty.astype(I32)) - 1)[None, :], 0), axis=1)
    nxt = jnp.sum(jnp.where(is_e, next_e[None, :], 0), axis=1)
    xs_sorted, pos = _dispatch_call(pend, tab, te, gsrc, h2, cap)
    eo = _expert_call(block_e, n_valid, seg, nxt, xs_sorted, w_gate_up[0],
                      b_gate_up[0].reshape(N_EXP, 1, -1), w_down[0], b_down[0].reshape(N_EXP, 1, -1))

    yp, ys = _combine_call(geom, n, tab, eo, pos, x1, tw, m3_hi, m3_lo, g_final.reshape(1, D))
    return (yp.reshape(bp, lp, D), ys.reshape(bs, ls, D),
            sa_p[None], sb_p[None], sa_s[None], sb_s[None])
```

```python
import functools

import jax
import jax.numpy as jnp
from jax import lax
from jax.experimental import pallas as pl
from jax.experimental.pallas import tpu as pltpu

F32 = jnp.float32
BF16 = jnp.bfloat16
I32 = jnp.int32

EPS = 1e-6
D = 1024
GLA_H, GLA_DK, GLA_DV = 4, 128, 256
HG_H, HG_DK, HG_DV = 8, 128, 128
GATE_RANK = 16
GATE_NORM = 16.0
N_EXP = 32
TOP_K = 4
SWIGLU_LIMIT = 7.0
SWIGLU_ALPHA = 1.702

LANES = 128
TM = 256
CHUNK = 128
SUB = 32
NSUB = CHUNK // SUB
REC_ROWS = 2 * CHUNK
POST_ROWS = 128
EXP_CLAMP = 80.0
BM = 512
ROUTE_ROWS = 1280
RUN = 8
RUN_PIECES = (256, 128, 64, 32, 16, 8)
WAIT_ROWS = 128
SORT_ROWS = TM * TOP_K + 256
TAB_ROWS = 3 * N_EXP
NEG = -1e30

C_QA, C_KA, C_VA, C_RA, C_QB, C_FB, C_IB, C_RB, C_UA, C_UB, C_END = (
    0, 512, 1024, 2048, 3072, 4096, 5120, 6144, 7168, 8192, 9216)


def _sigmoid(x):
    return 1.0 / (1.0 + jnp.exp(-x))


def _dot(a, b):
    return jnp.dot(a, b, preferred_element_type=F32)


def _split_bf16(x):
    hi = x.astype(BF16)
    lo = (x - hi.astype(F32)).astype(BF16)
    return hi, lo


def _log2(n):
    assert n > 0 and n & (n - 1) == 0, "power of two expected"
    return n.bit_length() - 1


def _cparams(vmem_mb):
    return pltpu.CompilerParams(dimension_semantics=("arbitrary",),
                                vmem_limit_bytes=vmem_mb * 1024 * 1024)


def _mod_kernel(c_ref, w_ref, b_ref, o_ref):
    c = c_ref[...]
    s = (c * _sigmoid(c)).astype(BF16)
    o_ref[...] = _dot(s, w_ref[...].astype(BF16)) + b_ref[...]


def _mod_call(c_all, w_ada, b_ada):
    rows = c_all.shape[0]
    return pl.pallas_call(
        _mod_kernel,
        grid=(6,),
        in_specs=[pl.BlockSpec((rows, D), lambda j: (0, 0)),
                  pl.BlockSpec((D, D), lambda j: (0, j)),
                  pl.BlockSpec((1, D), lambda j: (0, j))],
        out_specs=pl.BlockSpec((rows, D), lambda j: (0, j)),
        out_shape=jax.ShapeDtypeStruct((rows, 6 * D), F32),
        compiler_params=_cparams(32),
        name="mod",
    )(c_all, w_ada, b_ada)


def _row_mod(i, geom, table_ref, sample_ref):
    n_prompt_rows, prompt_len, _, n_prompt_seq = geom
    seq = jnp.minimum((i * TM) >> _log2(prompt_len), n_prompt_seq - 1)
    return jnp.where(i < n_prompt_rows // TM, table_ref[pl.ds(seq, 1), :], sample_ref[...])


def _proj_kernel(geom, xp_ref, xs_ref, mhi_ref, mlo_ref, g_ref, lbl_ref, wa_ref, wb_ref, wlr_ref,
                 w2_ref, bg_ref,
                 qa_ref, ka_ref, va_ref, ra_ref, ga_ref, qb_ref, kb_ref, gb_ref, ib_ref,
                 rb_ref, ua_ref, ub_ref):
    i = pl.program_id(0)
    n_prompt_tiles = geom[0] // TM
    x = jnp.where(i < n_prompt_tiles, xp_ref[...], xs_ref[...])
    mod = _row_mod(i, geom, mhi_ref, mlo_ref)
    sh, sc = mod[:, :D], mod[:, D:]
    ms = jnp.mean(x * x, axis=-1, keepdims=True)
    h = x * lax.rsqrt(ms + EPS) * g_ref[...]
    hb = (h * (1.0 + sc) + sh).astype(BF16)

    def proj(a, b):
        if b <= C_QB:
            return _dot(hb, wa_ref[:, a:b])
        return _dot(hb, wb_ref[:, a - C_QB:b - C_QB])

    cols = (C_QA, C_KA, C_VA, C_RA, C_QB, C_FB, C_IB, C_RB, C_UA, C_UB, C_END)
    z_qa, z_ka, z_va, z_ra, z_qb, z_fb, z_ib, z_rb, z_ua, z_ub = [
        proj(a, b) for a, b in zip(cols[:-1], cols[1:])]
    lr_hi, lr_lo = _split_bf16(_dot(hb, wlr_ref[...]))
    qa_ref[...] = (z_qa * GLA_DK ** -0.5).astype(BF16)
    ka_ref[...] = z_ka.astype(BF16)
    va_ref[...] = z_va.astype(BF16)
    ra_ref[...] = (z_ra * _sigmoid(z_ra)).astype(BF16)
    xg = _dot(lr_hi, w2_ref[...]) + _dot(lr_lo, w2_ref[...]) + bg_ref[...]
    ga_ref[...] = (jnp.minimum(xg, 0.0) - jnp.log1p(jnp.exp(-jnp.abs(xg)))) * (1.0 / GATE_NORM)
    qb_ref[...] = (z_qb * _sigmoid(z_qb) * HG_DK ** -0.5).astype(BF16)
    lbl = lbl_ref[...]
    e = jnp.exp(lbl - jnp.max(lbl, axis=0, keepdims=True))
    lb = e[0:1, :] / jnp.sum(e, axis=0, keepdims=True)
    kb_ref[...] = ((1.0 - lb) * _sigmoid(-z_fb)).astype(BF16)
    gb_ref[...] = jnp.log(lb + (1.0 - lb) * _sigmoid(z_fb))
    ib_ref[...] = z_ib.astype(BF16)
    rb_ref[...] = (z_rb * _sigmoid(z_rb)).astype(BF16)
    ua_ref[...] = _sigmoid(z_ua).astype(BF16)
    ub_ref[...] = _sigmoid(z_ub).astype(BF16)


def _proj_call(geom, n_rows, xp, xs, mhi, mlo, g_mix, lbl, wa, wb, wlr, w2, bg):
    n_tiles = n_rows // TM
    last_p = geom[0] // TM - 1
    const = lambda i: (0, 0)
    row = lambda i: (i, 0)
    widths = [(512, BF16), (512, BF16), (D, BF16), (D, BF16), (512, F32), (D, BF16), (D, BF16),
              (D, F32), (D, BF16), (D, BF16), (D, BF16), (D, BF16)]
    return pl.pallas_call(
        functools.partial(_proj_kernel, geom),
        grid=(n_tiles,),
        in_specs=[pl.BlockSpec((TM, D), lambda i: (jnp.minimum(i, last_p), 0)),
                  pl.BlockSpec(xs.shape, const),
                  pl.BlockSpec(mhi.shape, const), pl.BlockSpec(mlo.shape, const),
                  pl.BlockSpec((1, D), const), pl.BlockSpec(lbl.shape, const),
                  pl.BlockSpec(wa.shape, const, pipeline_mode=pl.Buffered(1)),
                  pl.BlockSpec(wb.shape, const, pipeline_mode=pl.Buffered(1)),
                  pl.BlockSpec(wlr.shape, const), pl.BlockSpec(w2.shape, const),
                  pl.BlockSpec(bg.shape, const)],
        out_specs=[pl.BlockSpec((TM, w), row) for w, _ in widths],
        out_shape=[jax.ShapeDtypeStruct((n_rows, w), dt) for w, dt in widths],
        compiler_params=_cparams(56),
        name="proj",
    )(xp, xs, mhi, mlo, g_mix, lbl, wa, wb, wlr, w2, bg)


def _rec_prep(tri, g, q, k):
    g_hi, g_lo = _split_bf16(g)
    cum = _dot(tri, g_hi) + _dot(tri, g_lo)
    width = cum.shape[1]
    tot = cum[CHUNK - 1:CHUNK, :]
    refs = [jnp.zeros((1, width), F32)] + [cum[j * SUB - 1:j * SUB, :] for j in range(1, NSUB)]
    d = cum - jnp.concatenate([jnp.broadcast_to(b, (SUB, width)) for b in refs], axis=0)
    qn = q.astype(F32) * jnp.exp(d)
    kn = k.astype(F32) * jnp.exp(jnp.minimum(-d, EXP_CLAMP))
    sub = lambda x, j: x[j * SUB:(j + 1) * SUB, :]
    qs = jnp.concatenate([sub(qn, j) * jnp.exp(refs[j]) for j in range(NSUB)],
                         axis=0).astype(BF16)
    kd = jnp.concatenate([sub(kn, j) * jnp.exp(tot - refs[j]) for j in range(NSUB)], axis=0)
    et = jnp.exp(tot)
    zero_rows = lambda n: jnp.zeros((n, width), F32)
    q_to, k_of = [], []
    for j in range(NSUB):
        parts = [zero_rows(j * SUB)] if j else []
        parts += [sub(qn, i) * jnp.exp(refs[i] - refs[j]) if i > j else sub(qn, i)
                  for i in range(j, NSUB)]
        q_to.append(jnp.concatenate(parts, axis=0).astype(BF16))
        parts = ([zero_rows(j * SUB)] if j else []) + [sub(kn, j)]
        if j + 1 < NSUB:
            parts.append(zero_rows(CHUNK - (j + 1) * SUB))
        k_of.append(jnp.concatenate(parts, axis=0).astype(BF16))
    return qs, kd, et, q_to, k_of


def _rec_kernel(rows, has_s0, *refs):
    (qa_ref, ka_ref, va_ref, ra_ref, ga_ref, qb_ref, kb_ref, gb_ref, ib_ref, rb_ref,
     gna_ref, gnb_ref) = refs[:12]
    refs = refs[12:]
    if has_s0:
        s0a_ref, s0b_ref = refs[:2]
        refs = refs[2:]
    oga_ref, ogb_ref, sa_ref, sb_ref = refs

    @pl.when(pl.program_id(1) == 0)
    def _():
        if has_s0:
            sa_ref[...] = s0a_ref[...]
            sb_ref[...] = s0b_ref[...]
        else:
            sa_ref[...] = jnp.zeros_like(sa_ref)
            sb_ref[...] = jnp.zeros_like(sb_ref)

    ti = lax.broadcasted_iota(I32, (CHUNK, CHUNK), 0)
    si = lax.broadcasted_iota(I32, (CHUNK, CHUNK), 1)
    causal = ti >= si
    tri = causal.astype(BF16)
    n_chunks = max(rows // CHUNK, 1)
    rows_out = min(rows, CHUNK)

    def load(ref, c):
        if rows >= CHUNK:
            return ref[c * CHUNK:(c + 1) * CHUNK, :]
        x = ref[...]
        return jnp.concatenate([x, jnp.zeros((CHUNK - rows, x.shape[1]), x.dtype)], axis=0)

    mixers = ((GLA_H, GLA_DK, GLA_DV, ga_ref, qa_ref, ka_ref, va_ref, ra_ref, sa_ref, gna_ref, oga_ref),
              (HG_H, HG_DK, HG_DV, gb_ref, qb_ref, kb_ref, ib_ref, rb_ref, sb_ref, gnb_ref, ogb_ref))
    units = []
    for c in range(n_chunks):
        for (nh, dk, dv, g_ref, q_ref, k_ref, v_ref, r_ref, s_ref, gn_ref, o_ref) in mixers:
            qs, kd, et, q_to, k_of = _rec_prep(tri, load(g_ref, c), load(q_ref, c), load(k_ref, c))
            for h in range(nh):
                ks = slice(h * dk, (h + 1) * dk)
                units.append(dict(
                    c=c, h=h, dk=dk, dv=dv, vs=slice(h * dv, (h + 1) * dv), s_ref=s_ref,
                    gn_ref=gn_ref, o_ref=o_ref, v_ref=v_ref, r_ref=r_ref,
                    qs=qs[:, ks], kd=kd[:, ks], et=et[:, ks],
                    q_to=jnp.concatenate([x[:, ks] for x in q_to], axis=1),
                    k_of=jnp.concatenate([x[:, ks] for x in k_of], axis=1)))
    for u in units:
        u["a"] = lax.dot_general(u["q_to"], u["k_of"], (((1,), (1,)), ((), ())),
                                 preferred_element_type=F32)
    for u in units:
        u["lhs"] = jnp.concatenate([jnp.where(causal, u["a"], 0.0).astype(BF16),
                                    u["kd"].T.astype(BF16)], axis=0)
    for u in units:
        u["av"] = _dot(u["lhs"], load(u["v_ref"], u["c"])[:, u["vs"]])
    for u in units:
        dk, dv, h = u["dk"], u["dv"], u["h"]
        s = u["s_ref"][0, h]
        o = _dot(u["qs"], s.astype(BF16)) + u["av"][:CHUNK]
        et_col = jnp.broadcast_to(u["et"], (dk, dk)).T
        u["s_ref"][0, h] = s * jnp.tile(et_col, (1, dv // dk)) + u["av"][CHUNK:]
        ms = jnp.mean(o * o, axis=-1, keepdims=True)
        og = (o * lax.rsqrt(ms + EPS) * u["gn_ref"][...]
              * load(u["r_ref"], u["c"])[:, u["vs"]].astype(F32))
        r0 = u["c"] * CHUNK
        u["o_ref"][r0:r0 + rows_out, u["vs"]] = og[:rows_out].astype(BF16)


def _rec_call(arrs, gna, gnb, n_seq, seq_len, rows, row0, s0=None):
    steps = seq_len // rows
    blk0 = row0 // rows
    row_in = lambda b, t: (blk0 + b * steps + t, 0)
    row_out = lambda b, t: (b * steps + t, 0)
    const = lambda b, t: (0, 0)
    st = lambda b, t: (b, 0, 0, 0)
    in_specs = [pl.BlockSpec((rows, a.shape[1]), row_in) for a in arrs]
    in_specs += [pl.BlockSpec(gna.shape, const), pl.BlockSpec(gnb.shape, const)]
    args = list(arrs) + [gna, gnb]
    if s0 is not None:
        in_specs += [pl.BlockSpec((1, GLA_H, GLA_DK, GLA_DV), st),
                     pl.BlockSpec((1, HG_H, HG_DK, HG_DV), st)]
        args += [s0[0], s0[1]]
    n_rows = n_seq * seq_len
    return pl.pallas_call(
        functools.partial(_rec_kernel, rows, s0 is not None),
        grid=(n_seq, steps),
        in_specs=in_specs,
        out_specs=[pl.BlockSpec((rows, D), row_out), pl.BlockSpec((rows, D), row_out),
                   pl.BlockSpec((1, GLA_H, GLA_DK, GLA_DV), st),
                   pl.BlockSpec((1, HG_H, HG_DK, HG_DV), st)],
        out_shape=[jax.ShapeDtypeStruct((n_rows, D), BF16), jax.ShapeDtypeStruct((n_rows, D), BF16),
                   jax.ShapeDtypeStruct((n_seq, GLA_H, GLA_DK, GLA_DV), F32),
                   jax.ShapeDtypeStruct((n_seq, HG_H, HG_DK, HG_DV), F32)],
        compiler_params=pltpu.CompilerParams(dimension_semantics=("arbitrary", "arbitrary"),
                                             vmem_limit_bytes=48 * 1024 * 1024),
        name="rec_s0" if s0 is not None else "rec",
    )(*args)


def _post_kernel(geom, xp_ref, xs_ref, oap_ref, oas_ref, obp_ref, obs_ref, ua_ref, ub_ref,
                 mhi_ref, mlo_ref, gf_ref, wba_ref, wbb_ref, wo_ref, wrh_ref, wrl_ref, br_ref,
                 x1_ref, h2_ref, lg_ref):
    i = pl.program_id(0)
    is_prompt = i < geom[0] // TM
    mod = _row_mod(i, geom, mhi_ref, mlo_ref)
    groups = [slice(j * POST_ROWS, (j + 1) * POST_ROWS) for j in range(TM // POST_ROWS)]
    pa = [_dot(jnp.where(is_prompt, oap_ref[g, :], oas_ref[g, :]), wba_ref[...]) for g in groups]
    pb = [_dot(jnp.where(is_prompt, obp_ref[g, :], obs_ref[g, :]), wbb_ref[...]) for g in groups]
    merged = [(ua_ref[g, :].astype(F32) * a + ub_ref[g, :].astype(F32) * b).astype(BF16)
              for g, a, b in zip(groups, pa, pb)]
    y = [_dot(m, wo_ref[...]) for m in merged]
    h2s = []
    for g, yg in zip(groups, y):
        x1 = jnp.where(is_prompt, xp_ref[g, :], xs_ref[g, :]) + mod[g, :D] * yg
        x1_ref[g, :] = x1
        ms = jnp.mean(x1 * x1, axis=-1, keepdims=True)
        h2 = x1 * lax.rsqrt(ms + EPS) * gf_ref[...] * (1.0 + mod[g, 2 * D:]) + mod[g, D:2 * D]
        h2_ref[g, :] = h2
        h2s.append(_split_bf16(h2))
    for g, (h_hi, h_lo) in zip(groups, h2s):
        lg_ref[g, :] = (_dot(h_hi, wrh_ref[...]) + _dot(h_lo, wrh_ref[...])
                        + _dot(h_hi, wrl_ref[...]) + br_ref[...])


def _post_call(geom, n_rows, xp, xs, oap, oas, obp, obs, ua, ub, mhi, mlo, g_ffn, wba, wbb, wo,
               wrh, wrl, br):
    n_tiles = n_rows // TM
    last_p = geom[0] // TM - 1
    const = lambda i: (0, 0)
    row = lambda i: (i, 0)
    prow = pl.BlockSpec((TM, D), lambda i: (jnp.minimum(i, last_p), 0))
    full = lambda a: pl.BlockSpec(a.shape, const)
    return pl.pallas_call(
        functools.partial(_post_kernel, geom),
        grid=(n_tiles,),
        in_specs=[prow, full(xs), prow, full(oas), prow, full(obs),
                  pl.BlockSpec((TM, D), row), pl.BlockSpec((TM, D), row),
                  full(mhi), full(mlo), full(g_ffn), full(wba), full(wbb), full(wo),
                  full(wrh), full(wrl), full(br)],
        out_specs=[pl.BlockSpec((TM, D), row), pl.BlockSpec((TM, D), row),
                   pl.BlockSpec((TM, LANES), row)],
        out_shape=[jax.ShapeDtypeStruct((n_rows, D), F32), jax.ShapeDtypeStruct((n_rows, D), F32),
                   jax.ShapeDtypeStruct((n_rows, LANES), F32)],
        compiler_params=_cparams(48),
        name="post",
    )(xp, xs, oap, oas, obp, obs, ua, ub, mhi, mlo, g_ffn, wba, wbb, wo, wrh, wrl, br)


def _multi_hot(te):
    lane = lax.broadcasted_iota(I32, (TM, LANES), 1)
    m = jnp.zeros((TM, LANES), F32)
    for k in range(TOP_K):
        m = m + (lane == te[:, k:k + 1]).astype(F32)
    return m


def _route_kernel(lg_ref, te_ref, tw_ref, cnt_ref):
    logit = lg_ref[...]
    lane = lax.broadcasted_iota(I32, (ROUTE_ROWS, LANES), 1)
    lane_f = lane.astype(F32)
    vals, idxs = [], []
    for _ in range(TOP_K):
        m = jnp.max(logit, axis=-1, keepdims=True)
        idx = jnp.min(jnp.where(logit == m, lane_f, float(LANES)), axis=-1, keepdims=True)
        vals.append(m)
        idxs.append(idx)
        logit = jnp.where(lane_f == idx, -jnp.inf, logit)
    es = [jnp.exp(v - vals[0]) for v in vals]
    den = es[0] + es[1] + es[2] + es[3]
    te = jnp.zeros((ROUTE_ROWS, LANES), F32)
    tw = jnp.zeros((ROUTE_ROWS, LANES), F32)
    for k in range(TOP_K):
        te = jnp.where(lane == k, idxs[k], te)
        tw = jnp.where(lane == k, es[k] / den, tw)
    te = te.astype(I32)
    te_ref[...] = te
    tw_ref[...] = tw
    for t in range(ROUTE_ROWS // TM):
        cnt_ref[t * 8:(t + 1) * 8, :] = jnp.broadcast_to(
            jnp.sum(_multi_hot(te[t * TM:(t + 1) * TM]), axis=0, keepdims=True), (8, LANES))


def _route_call(logits, n_rows):
    n_steps = n_rows // ROUTE_ROWS
    row = lambda i: (i, 0)
    tiles = ROUTE_ROWS // TM
    return pl.pallas_call(
        _route_kernel,
        grid=(n_steps,),
        in_specs=[pl.BlockSpec((ROUTE_ROWS, LANES), row)],
        out_specs=[pl.BlockSpec((ROUTE_ROWS, LANES), row), pl.BlockSpec((ROUTE_ROWS, LANES), row),
                   pl.BlockSpec((tiles * 8, LANES), row)],
        out_shape=[jax.ShapeDtypeStruct((n_rows, LANES), I32),
                   jax.ShapeDtypeStruct((n_rows, LANES), F32),
                   jax.ShapeDtypeStruct((n_rows // TM * 8, LANES), F32)],
        compiler_params=_cparams(32),
        name="route",
    )(logits)


def _wait_rows(rows, wait_n_rows):
    lax.fori_loop(0, rows >> _log2(WAIT_ROWS), lambda j, c: (wait_n_rows(WAIT_ROWS), c)[1], 0)
    lax.fori_loop(0, (rows & (WAIT_ROWS - 1)) >> _log2(RUN),
                  lambda j, c: (wait_n_rows(RUN), c)[1], 0)


def _for_each_run_piece(tab_ref, fn):
    for e in range(N_EXP):
        length = tab_ref[0, 0, e]
        src = tab_ref[0, 0, N_EXP + e]
        dst = tab_ref[0, 0, 2 * N_EXP + e]
        off = jnp.int32(0)
        for p in RUN_PIECES:
            has = (length & p) != 0
            pl.when(has)(lambda off=off, p=p, src=src, dst=dst: fn(
                pl.multiple_of(src + off, RUN), pl.multiple_of(dst + off, RUN), p))
            off = off + jnp.where(has, p, 0)


def _dispatch_kernel(pend_ref, tab_ref, tprev_ref, te_ref, gsrc_ref, h2_ref, xs_ref, pos_ref, z_ref,
                     zeros, sem, zsem):
    @pl.when(pl.program_id(0) == 0)
    def _():
        zeros[...] = jnp.zeros_like(zeros)

        def zero_copy(e):
            start = pl.multiple_of(pend_ref[e] - BM, BM)
            return pltpu.make_async_copy(zeros, xs_ref.at[pl.ds(start, BM)], zsem)

        def nonempty(e):
            return pend_ref[e] > (pend_ref[e - 1] if e > 0 else 0)

        for e in range(N_EXP):
            pl.when(nonempty(e))(lambda e=e: zero_copy(e).start())
        for e in range(N_EXP):
            pl.when(nonempty(e))(lambda e=e: zero_copy(e).wait())

        def tail_copy(j):
            return pltpu.make_async_copy(zeros, xs_ref.at[pl.ds(pl.multiple_of(j * BM, BM), BM)],
                                         zsem)

        first, last = pend_ref[N_EXP - 1] // BM, xs_ref.shape[0] // BM
        lax.fori_loop(first, last, lambda j, c: (tail_copy(j).start(), c)[1], 0)
        lax.fori_loop(first, last, lambda j, c: (tail_copy(j).wait(), c)[1], 0)

    te = te_ref[...]
    ri = lax.broadcasted_iota(I32, (TM, TM), 0)
    ci = lax.broadcasted_iota(I32, (TM, TM), 1)
    rank = _dot((ri > ci).astype(BF16), _multi_hot(te).astype(BF16)) + gsrc_ref[0:1, :]
    lane = lax.broadcasted_iota(I32, (TM, LANES), 1)
    pos = jnp.zeros((TM, LANES), F32)
    for k in range(TOP_K):
        p = jnp.sum(jnp.where(lane == te[:, k:k + 1], rank, 0.0), axis=-1, keepdims=True)
        pos = jnp.where(lane == k, p, pos)
    pos_ref[...] = pos.astype(I32)
    pos_t = pos.T.astype(I32)
    row = lax.broadcasted_iota(I32, (SORT_ROWS, TM), 0)
    onehot = jnp.zeros((SORT_ROWS, TM), F32)
    for k in range(TOP_K):
        onehot = jnp.where(row == pos_t[k:k + 1, :], 1.0, onehot)
    i = pl.program_id(0)
    slot = lax.rem(i, 2)
    z_ref[slot] = _dot(onehot.astype(BF16), h2_ref[...].astype(BF16))

    _for_each_run_piece(tab_ref, lambda s, d, p: pltpu.make_async_copy(
        z_ref.at[slot, pl.ds(s, p)], xs_ref.at[pl.ds(d, p)], sem.at[slot]).start())

    def wait_tile(table_ref, which):
        _wait_rows(table_ref[0, 0, TAB_ROWS], lambda n: pltpu.make_async_copy(
            z_ref.at[which, pl.ds(0, n)], xs_ref.at[pl.ds(0, n)], sem.at[which]).wait())

    pl.when(i > 0)(lambda: wait_tile(tprev_ref, 1 - slot))
    pl.when(i == pl.num_programs(0) - 1)(lambda: wait_tile(tab_ref, slot))


def _dispatch_call(pend, tab, te, gsrc, h2, cap):
    n_tiles = tab.shape[0]
    row = lambda i, pe: (i, 0)
    return pl.pallas_call(
        _dispatch_kernel,
        grid_spec=pltpu.PrefetchScalarGridSpec(
            num_scalar_prefetch=1,
            grid=(n_tiles,),
            in_specs=[pl.BlockSpec((1, 1, LANES), lambda i, pe: (i, 0, 0), memory_space=pltpu.SMEM),
                      pl.BlockSpec((1, 1, LANES), lambda i, pe: (jnp.maximum(i - 1, 0), 0, 0),
                                   memory_space=pltpu.SMEM),
                      pl.BlockSpec((TM, LANES), row), pl.BlockSpec((8, LANES), row),
                      pl.BlockSpec((TM, D), row)],
            out_specs=[pl.BlockSpec(memory_space=pl.ANY), pl.BlockSpec((TM, LANES), row)],
            scratch_shapes=[pltpu.VMEM((2, SORT_ROWS, D), F32), pltpu.VMEM((BM, D), F32),
                            pltpu.SemaphoreType.DMA((2,)), pltpu.SemaphoreType.DMA]),
        out_shape=[jax.ShapeDtypeStruct((cap, D), F32),
                   jax.ShapeDtypeStruct((n_tiles * TM, LANES), I32)],
        compiler_params=_cparams(40),
        name="dispatch",
    )(pend, tab, tab, te, gsrc, h2)


def _expert_kernel(be_ref, nv_ref, seg_ref, nxt_ref, fill_ref, x_ref, wgu_hbm, bgu_ref, wd_hbm, bd_ref,
                   o_ref, wgu_f, wd_f, wgu_s, wd_s, sem):
    j = pl.program_id(0)
    jc = jnp.minimum(j, nv_ref[0] - 1)
    e = be_ref[jc]
    slot = lax.rem(seg_ref[jc], 2)
    first = (j == 0) | ((j < nv_ref[0]) & (e != be_ref[jnp.maximum(jc - 1, 0)]))

    def fetch(expert, into):
        return (pltpu.make_async_copy(wgu_hbm.at[expert], wgu_f.at[into], sem.at[0, into]),
                pltpu.make_async_copy(wd_hbm.at[expert], wd_f.at[into], sem.at[1, into]))

    @pl.when(j == 0)
    def _():
        for c in fetch(e, slot):
            c.start()

    @pl.when(first)
    def _():
        for c in fetch(e, slot):
            c.wait()

        @pl.when(nxt_ref[jc] >= 0)
        def _():
            for c in fetch(nxt_ref[jc], 1 - slot):
                c.start()

        wgu_s[...] = wgu_f[slot].astype(BF16)
        wd_s[...] = wd_f[slot].astype(BF16)

    def mlp(groups):
        gus = [_dot(x_ref[g, :].astype(BF16), wgu_s[...]) + bgu_ref[0] for g in groups]
        acts = []
        for gu in gus:
            gate = jnp.minimum(gu[:, :D], SWIGLU_LIMIT)
            up = jnp.clip(gu[:, D:], -SWIGLU_LIMIT, SWIGLU_LIMIT)
            acts.append(((up + 1.0) * (gate * _sigmoid(SWIGLU_ALPHA * gate))).astype(BF16))
        for g, act in zip(groups, acts):
            o_ref[g, :] = _dot(act, wd_s[...]) + bd_ref[0]

    lower, upper = slice(0, BM // 2), slice(BM // 2, BM)
    needs_upper = fill_ref[jc] > BM // 2

    @pl.when((j < nv_ref[0]) & needs_upper)
    def _():
        mlp([lower, upper])

    @pl.when((j < nv_ref[0]) & jnp.logical_not(needs_upper))
    def _():
        mlp([lower])
        o_ref[upper, :] = jnp.zeros((BM // 2, D), F32)

    @pl.when(j >= nv_ref[0])
    def _():
        o_ref[...] = jnp.zeros_like(o_ref)


def _expert_call(block_e, n_valid, seg, nxt, fill, xs, wgu, bgu, wd, bd):
    n_blocks = xs.shape[0] // BM
    blk = lambda j, be, nv, sg, nx, fl: (jnp.minimum(j, nv[0] - 1), 0)
    blk_out = lambda j, be, nv, sg, nx, fl: (j, 0)
    exp = lambda j, be, nv, sg, nx, fl: (be[jnp.minimum(j, nv[0] - 1)], 0, 0)
    return pl.pallas_call(
        _expert_kernel,
        grid_spec=pltpu.PrefetchScalarGridSpec(
            num_scalar_prefetch=5,
            grid=(n_blocks,),
            in_specs=[pl.BlockSpec((BM, D), blk),
                      pl.BlockSpec(memory_space=pl.ANY), pl.BlockSpec((1, 1, 2 * D), exp),
                      pl.BlockSpec(memory_space=pl.ANY), pl.BlockSpec((1, 1, D), exp)],
            out_specs=pl.BlockSpec((BM, D), blk_out),
            scratch_shapes=[pltpu.VMEM((2, D, 2 * D), F32), pltpu.VMEM((2, D, D), F32),
                            pltpu.VMEM((D, 2 * D), BF16), pltpu.VMEM((D, D), BF16),
                            pltpu.SemaphoreType.DMA((2, 2))]),
        out_shape=jax.ShapeDtypeStruct(xs.shape, F32),
        compiler_params=_cparams(56),
        name="expert",
    )(block_e, n_valid, seg, nxt, fill, xs, wgu, bgu, wd, bd)


def _combine_kernel(geom, tcur_ref, tnxt_ref, eo_ref, pos_ref, x1_ref, tw_ref, mhi_ref, mlo_ref,
                    gfin_ref, yp_ref, ys_ref, buf, sem):
    i = pl.program_id(0)
    n_prompt_tiles = geom[0] // TM
    slot = lax.rem(i, 2)

    def gather(tab_ref, into):
        _for_each_run_piece(tab_ref, lambda s, d, p: pltpu.make_async_copy(
            eo_ref.at[pl.ds(d, p)], buf.at[into, pl.ds(s, p)], sem.at[into]).start())

    @pl.when(i == 0)
    def _():
        buf[...] = jnp.zeros_like(buf)
        gather(tcur_ref, 0)

    @pl.when(i + 1 < pl.num_programs(0))
    def _():
        gather(tnxt_ref, 1 - slot)

    _wait_rows(tcur_ref[0, 0, TAB_ROWS], lambda n: pltpu.make_async_copy(
        eo_ref.at[pl.ds(0, n)], buf.at[slot, pl.ds(0, n)], sem.at[slot]).wait())
    gt2 = _row_mod(i, geom, mhi_ref, mlo_ref)
    tw = tw_ref[...]
    pos = pos_ref[...]
    col = lax.broadcasted_iota(I32, (TM, SORT_ROWS), 1)
    wsel = jnp.zeros((TM, SORT_ROWS), F32)
    for k in range(TOP_K):
        wsel = jnp.where(col == pos[:, k:k + 1], tw[:, k:k + 1], wsel)
    y = _dot(wsel.astype(BF16), buf[slot].astype(BF16))
    x2 = x1_ref[...] + gt2 * y
    ms = jnp.mean(x2 * x2, axis=-1, keepdims=True)
    out = x2 * lax.rsqrt(ms + EPS) * gfin_ref[...]

    @pl.when(i < n_prompt_tiles)
    def _():
        yp_ref[...] = out

    @pl.when(i >= n_prompt_tiles)
    def _():
        ys_ref[...] = out


def _combine_call(geom, n_rows, tab, eo, pos, x1, tw, mhi, mlo, g_final):
    n_tiles = n_rows // TM
    n_prompt = geom[0]
    last_p = n_prompt // TM - 1
    const = lambda i: (0, 0)
    row = lambda i: (i, 0)
    return pl.pallas_call(
        functools.partial(_combine_kernel, geom),
        grid=(n_tiles,),
        in_specs=[pl.BlockSpec((1, 1, LANES), lambda i: (i, 0, 0), memory_space=pltpu.SMEM),
                  pl.BlockSpec((1, 1, LANES), lambda i: (jnp.minimum(i + 1, n_tiles - 1), 0, 0),
                               memory_space=pltpu.SMEM),
                  pl.BlockSpec(memory_space=pl.ANY), pl.BlockSpec((TM, LANES), row),
                  pl.BlockSpec((TM, D), row), pl.BlockSpec((TM, LANES), row),
                  pl.BlockSpec(mhi.shape, const), pl.BlockSpec(mlo.shape, const),
                  pl.BlockSpec((1, D), const)],
        out_specs=[pl.BlockSpec((TM, D), lambda i: (jnp.minimum(i, last_p), 0)),
                   pl.BlockSpec((n_rows - n_prompt, D), const)],
        out_shape=[jax.ShapeDtypeStruct((n_prompt, D), F32),
                   jax.ShapeDtypeStruct((n_rows - n_prompt, D), F32)],
        scratch_shapes=[pltpu.VMEM((2, SORT_ROWS, D), F32), pltpu.SemaphoreType.DMA((2,))],
        compiler_params=_cparams(48),
        name="combine",
    )(tab, tab, eo, pos, x1, tw, mhi, mlo, g_final)


def kernel(x_prompt, x_sample, c_prompt, c_sample, state_gla, state_hgrn, w_ada, b_ada, g_norm_mix,
           g_norm_ffn, w_in, w_gla_gate2, b_gla_gate, g_gla_onorm, hgrn_lb_logits, g_hgrn_onorm,
           w_branch_a, w_branch_b, w_out, w_router, b_router, w_gate_up, b_gate_up, w_down, b_down,
           g_final):
    assert w_ada.shape[0] == 1, "single-layer trunk only"
    bp, lp, _ = x_prompt.shape
    bs, ls, _ = x_sample.shape
    n_p, n_s = bp * lp, bs * ls
    n = n_p + n_s
    assert n_p % TM == 0 and n_s == TM and lp % TM == 0 and lp % REC_ROWS == 0 and ls <= CHUNK
    assert bp + bs <= 32
    assert SORT_ROWS >= TM * TOP_K + N_EXP * (RUN - 1) and RUN_PIECES[0] == TM
    assert n % ROUTE_ROWS == 0 and ROUTE_ROWS % TM == 0
    geom = (n_p, lp, ls, bp)
    xp = x_prompt.reshape(n_p, D)
    xs = x_sample.reshape(n_s, D)

    c_all = jnp.zeros((32, D), F32).at[:bp].set(c_prompt).at[bp:bp + bs].set(c_sample)
    mod = _mod_call(c_all, w_ada[0], b_ada[0].reshape(1, 6 * D))
    sh1, sc1, gt1, sh2, sc2, gt2 = [mod[:, j * D:(j + 1) * D] for j in range(6)]
    def table_and_sample_rows(t):
        return t, jnp.repeat(t[bp:bp + bs], ls, axis=0)

    m1_hi, m1_lo = table_and_sample_rows(jnp.concatenate([sh1, sc1], axis=1))
    m2_hi, m2_lo = table_and_sample_rows(jnp.concatenate([gt1, sh2, sc2], axis=1))
    m3_hi, m3_lo = table_and_sample_rows(gt2)

    wi = w_in[0]
    wa = wi[:, :C_QB].astype(BF16)
    wb = wi[:, C_QB + GATE_RANK:].astype(BF16)
    wlr = jnp.pad(wi[:, 3072:3072 + GATE_RANK], ((0, 0), (0, LANES - GATE_RANK))).astype(BF16)
    w2 = jnp.pad(w_gla_gate2[0], ((0, LANES - GATE_RANK), (0, 0))).astype(BF16)
    arrs = _proj_call(geom, n, xp, xs, m1_hi, m1_lo, g_norm_mix[0].reshape(1, D), hgrn_lb_logits,
                      wa, wb, wlr, w2, b_gla_gate[0].reshape(1, -1))
    qa, ka, va, ra, ga, qb, kb, gb, ib, rb, ua, ub = arrs
    rec_in = (qa, ka, va, ra, ga, qb, kb, gb, ib, rb)

    gna = g_gla_onorm[0].reshape(1, GLA_DV)
    gnb = g_hgrn_onorm[0].reshape(1, HG_DV)
    oap, obp, sa_p, sb_p = _rec_call(rec_in, gna, gnb, bp, lp, REC_ROWS, 0)
    oas, obs, sa_s, sb_s = _rec_call(rec_in, gna, gnb, bs, ls, ls, n_p,
                                     s0=(state_gla[0], state_hgrn[0]))

    wr = jnp.pad(w_router[0], ((0, 0), (0, LANES - N_EXP)))
    wr_hi, wr_lo = _split_bf16(wr)
    br = jnp.pad(b_router[0], (0, LANES - N_EXP), constant_values=NEG).reshape(1, LANES)
    x1, h2, logits = _post_call(geom, n, xp, xs, oap, oas, obp, obs, ua, ub, m2_hi, m2_lo,
                                g_norm_ffn[0].reshape(1, D), w_branch_a[0].astype(BF16),
                                w_branch_b[0].astype(BF16), w_out[0].astype(BF16), wr_hi, wr_lo, br)

    n_tiles = n // TM
    te, tw, cnt = _route_call(logits, n)
    run = (cnt[::8, :N_EXP].astype(I32) + RUN - 1) // RUN * RUN
    counts = jnp.sum(run, axis=0)
    seg_len = (counts + BM - 1) // BM * BM
    pend = jnp.cumsum(seg_len).astype(I32)
    run_dst = (pend - seg_len)[None, :] + jnp.cumsum(run, axis=0) - run
    run_src = jnp.cumsum(run, axis=1) - run
    tab = jnp.zeros((n_tiles, LANES), I32)
    tab = tab.at[:, :N_EXP].set(run).at[:, N_EXP:2 * N_EXP].set(run_src)
    tab = tab.at[:, 2 * N_EXP:3 * N_EXP].set(run_dst).at[:, TAB_ROWS].set(jnp.sum(run, axis=1))
    tab = tab.reshape(n_tiles, 1, LANES)
    gsrc = jnp.repeat(jnp.pad(run_src.astype(F32), ((0, 0), (0, LANES - N_EXP))), 8, axis=0)
    cap = (n * TOP_K + n_tiles * N_EXP * (RUN - 1) + N_EXP * (BM - 1) + BM - 1) // BM * BM
    block_start = jnp.arange(cap // BM, dtype=I32) * BM
    block_e = jnp.minimum(jnp.sum((pend[None, :] <= block_start[:, None]).astype(I32), axis=1),
                          N_EXP - 1)
    n_valid = pend[-1:] // BM
    ids = jnp.arange(N_EXP, dtype=I32)
    nonempty = counts > 0
    later = jnp.where(nonempty[None, :] & (ids[None, :] > ids[:, None]), ids[None, :], N_EXP)
    next_e = jnp.min(later, axis=1)
    next_e = jnp.where(next_e == N_EXP, -1, next_e)
    is_e = block_e[:, None] == ids[None, :]
    seg = jnp.sum(jnp.where(is_e, (jnp.cumsum(nonempty.astype(I32)) - 1)[None, :], 0), axis=1)
    nxt = jnp.sum(jnp.where(is_e, next_e[None, :], 0), axis=1)
    fill = jnp.clip(jnp.sum(jnp.where(is_e, (pend - seg_len + counts)[None, :], 0), axis=1)
                    - block_start, 0, BM)
    xs_sorted, pos = _dispatch_call(pend, tab, te, gsrc, h2, cap)
    eo = _expert_call(block_e, n_valid, seg, nxt, fill, xs_sorted, w_gate_up[0],
                      b_gate_up[0].reshape(N_EXP, 1, -1), w_down[0], b_down[0].reshape(N_EXP, 1, -1))

    yp, ys = _combine_call(geom, n, tab, eo, pos, x1, tw, m3_hi, m3_lo, g_final.reshape(1, D))
    return (yp.reshape(bp, lp, D), ys.reshape(bs, ls, D),
            sa_p[None], sb_p[None], sa_s[None], sb_s[None])
```

```python
import functools

import jax
import jax.numpy as jnp
from jax import lax
from jax.experimental import pallas as pl
from jax.experimental.pallas import tpu as pltpu

F32 = jnp.float32
BF16 = jnp.bfloat16
I32 = jnp.int32

EPS = 1e-6
D = 1024
GLA_H, GLA_DK, GLA_DV = 4, 128, 256
HG_H, HG_DK, HG_DV = 8, 128, 128
GATE_RANK = 16
GATE_NORM = 16.0
N_EXP = 32
TOP_K = 4
SWIGLU_LIMIT = 7.0
SWIGLU_ALPHA = 1.702

LANES = 128
TM = 256
CHUNK = 128
SUB = 32
NSUB = CHUNK // SUB
REC_ROWS = 2 * CHUNK
POST_ROWS = 128
EXP_CLAMP = 80.0
BM = 512
ROUTE_ROWS = 1280
RUN = 8
RUN_PIECES = (256, 128, 64, 32, 16, 8)
WAIT_ROWS = 128
SORT_ROWS = TM * TOP_K + 256
TAB_COUNT = 2 * N_EXP * len(RUN_PIECES)
TAB_ROWS = TAB_COUNT + len(RUN_PIECES)
TAB_LEN = 512
NEG = -1e30

C_QA, C_KA, C_VA, C_RA, C_QB, C_FB, C_IB, C_RB, C_UA, C_UB, C_END = (
    0, 512, 1024, 2048, 3072, 4096, 5120, 6144, 7168, 8192, 9216)


def _sigmoid(x):
    return 1.0 / (1.0 + jnp.exp(-x))


def _dot(a, b):
    return jnp.dot(a, b, preferred_element_type=F32)


def _split_bf16(x):
    hi = x.astype(BF16)
    lo = (x - hi.astype(F32)).astype(BF16)
    return hi, lo


def _log2(n):
    assert n > 0 and n & (n - 1) == 0, "power of two expected"
    return n.bit_length() - 1


def _cparams(vmem_mb):
    return pltpu.CompilerParams(dimension_semantics=("arbitrary",),
                                vmem_limit_bytes=vmem_mb * 1024 * 1024)


def _mod_kernel(c_ref, w_ref, b_ref, o_ref):
    c = c_ref[...]
    s = (c * _sigmoid(c)).astype(BF16)
    o_ref[...] = _dot(s, w_ref[...].astype(BF16)) + b_ref[...]


def _mod_call(c_all, w_ada, b_ada):
    rows = c_all.shape[0]
    return pl.pallas_call(
        _mod_kernel,
        grid=(6,),
        in_specs=[pl.BlockSpec((rows, D), lambda j: (0, 0)),
                  pl.BlockSpec((D, D), lambda j: (0, j)),
                  pl.BlockSpec((1, D), lambda j: (0, j))],
        out_specs=pl.BlockSpec((rows, D), lambda j: (0, j)),
        out_shape=jax.ShapeDtypeStruct((rows, 6 * D), F32),
        compiler_params=_cparams(32),
        name="mod",
    )(c_all, w_ada, b_ada)


def _row_mod(i, geom, table_ref, sample_ref):
    n_prompt_rows, prompt_len, _, n_prompt_seq = geom
    seq = jnp.minimum((i * TM) >> _log2(prompt_len), n_prompt_seq - 1)
    return jnp.where(i < n_prompt_rows // TM, table_ref[pl.ds(seq, 1), :], sample_ref[...])


def _proj_kernel(geom, xp_ref, xs_ref, mhi_ref, mlo_ref, g_ref, lbl_ref, wa_ref, wb_ref, wlr_ref,
                 w2_ref, bg_ref,
                 qa_ref, ka_ref, va_ref, ra_ref, ga_ref, qb_ref, kb_ref, gb_ref, ib_ref,
                 rb_ref, ua_ref, ub_ref):
    i = pl.program_id(0)
    n_prompt_tiles = geom[0] // TM
    x = jnp.where(i < n_prompt_tiles, xp_ref[...], xs_ref[...])
    mod = _row_mod(i, geom, mhi_ref, mlo_ref)
    sh, sc = mod[:, :D], mod[:, D:]
    ms = jnp.mean(x * x, axis=-1, keepdims=True)
    h = x * lax.rsqrt(ms + EPS) * g_ref[...]
    hb = (h * (1.0 + sc) + sh).astype(BF16)

    def proj(a, b):
        if b <= C_QB:
            return _dot(hb, wa_ref[:, a:b])
        return _dot(hb, wb_ref[:, a - C_QB:b - C_QB])

    cols = (C_QA, C_KA, C_VA, C_RA, C_QB, C_FB, C_IB, C_RB, C_UA, C_UB, C_END)
    z_qa, z_ka, z_va, z_ra, z_qb, z_fb, z_ib, z_rb, z_ua, z_ub = [
        proj(a, b) for a, b in zip(cols[:-1], cols[1:])]
    lr_hi, lr_lo = _split_bf16(_dot(hb, wlr_ref[...]))
    qa_ref[...] = (z_qa * GLA_DK ** -0.5).astype(BF16)
    ka_ref[...] = z_ka.astype(BF16)
    va_ref[...] = z_va.astype(BF16)
    ra_ref[...] = (z_ra * _sigmoid(z_ra)).astype(BF16)
    xg = _dot(lr_hi, w2_ref[...]) + _dot(lr_lo, w2_ref[...]) + bg_ref[...]
    ga_ref[...] = (jnp.minimum(xg, 0.0) - jnp.log1p(jnp.exp(-jnp.abs(xg)))) * (1.0 / GATE_NORM)
    qb_ref[...] = (z_qb * _sigmoid(z_qb) * HG_DK ** -0.5).astype(BF16)
    lbl = lbl_ref[...]
    e = jnp.exp(lbl - jnp.max(lbl, axis=0, keepdims=True))
    lb = e[0:1, :] / jnp.sum(e, axis=0, keepdims=True)
    kb_ref[...] = ((1.0 - lb) * _sigmoid(-z_fb)).astype(BF16)
    gb_ref[...] = jnp.log(lb + (1.0 - lb) * _sigmoid(z_fb))
    ib_ref[...] = z_ib.astype(BF16)
    rb_ref[...] = (z_rb * _sigmoid(z_rb)).astype(BF16)
    ua_ref[...] = _sigmoid(z_ua).astype(BF16)
    ub_ref[...] = _sigmoid(z_ub).astype(BF16)


def _proj_call(geom, n_rows, xp, xs, mhi, mlo, g_mix, lbl, wa, wb, wlr, w2, bg):
    n_tiles = n_rows // TM
    last_p = geom[0] // TM - 1
    const = lambda i: (0, 0)
    row = lambda i: (i, 0)
    widths = [(512, BF16), (512, BF16), (D, BF16), (D, BF16), (512, F32), (D, BF16), (D, BF16),
              (D, F32), (D, BF16), (D, BF16), (D, BF16), (D, BF16)]
    return pl.pallas_call(
        functools.partial(_proj_kernel, geom),
        grid=(n_tiles,),
        in_specs=[pl.BlockSpec((TM, D), lambda i: (jnp.minimum(i, last_p), 0)),
                  pl.BlockSpec(xs.shape, const),
                  pl.BlockSpec(mhi.shape, const), pl.BlockSpec(mlo.shape, const),
                  pl.BlockSpec((1, D), const), pl.BlockSpec(lbl.shape, const),
                  pl.BlockSpec(wa.shape, const, pipeline_mode=pl.Buffered(1)),
                  pl.BlockSpec(wb.shape, const, pipeline_mode=pl.Buffered(1)),
                  pl.BlockSpec(wlr.shape, const), pl.BlockSpec(w2.shape, const),
                  pl.BlockSpec(bg.shape, const)],
        out_specs=[pl.BlockSpec((TM, w), row) for w, _ in widths],
        out_shape=[jax.ShapeDtypeStruct((n_rows, w), dt) for w, dt in widths],
        compiler_params=_cparams(56),
        name="proj",
    )(xp, xs, mhi, mlo, g_mix, lbl, wa, wb, wlr, w2, bg)


def _rec_prep(tri, g, q, k):
    g_hi, g_lo = _split_bf16(g)
    cum = _dot(tri, g_hi) + _dot(tri, g_lo)
    width = cum.shape[1]
    tot = cum[CHUNK - 1:CHUNK, :]
    refs = [jnp.zeros((1, width), F32)] + [cum[j * SUB - 1:j * SUB, :] for j in range(1, NSUB)]
    d = cum - jnp.concatenate([jnp.broadcast_to(b, (SUB, width)) for b in refs], axis=0)
    qn = q.astype(F32) * jnp.exp(d)
    kn = k.astype(F32) * jnp.exp(jnp.minimum(-d, EXP_CLAMP))
    sub = lambda x, j: x[j * SUB:(j + 1) * SUB, :]
    qs = jnp.concatenate([sub(qn, j) * jnp.exp(refs[j]) for j in range(NSUB)],
                         axis=0).astype(BF16)
    kd = jnp.concatenate([sub(kn, j) * jnp.exp(tot - refs[j]) for j in range(NSUB)], axis=0)
    et = jnp.exp(tot)
    zero_rows = lambda n: jnp.zeros((n, width), F32)
    q_to, k_of = [], []
    for j in range(NSUB):
        parts = [zero_rows(j * SUB)] if j else []
        parts += [sub(qn, i) * jnp.exp(refs[i] - refs[j]) if i > j else sub(qn, i)
                  for i in range(j, NSUB)]
        q_to.append(jnp.concatenate(parts, axis=0).astype(BF16))
        parts = ([zero_rows(j * SUB)] if j else []) + [sub(kn, j)]
        if j + 1 < NSUB:
            parts.append(zero_rows(CHUNK - (j + 1) * SUB))
        k_of.append(jnp.concatenate(parts, axis=0).astype(BF16))
    return qs, kd, et, q_to, k_of, jnp.max(-d, axis=(0, 1), keepdims=True)


def _rec_kernel(rows, has_s0, *refs):
    (qa_ref, ka_ref, va_ref, ra_ref, ga_ref, qb_ref, kb_ref, gb_ref, ib_ref, rb_ref,
     gna_ref, gnb_ref) = refs[:12]
    refs = refs[12:]
    if has_s0:
        s0a_ref, s0b_ref = refs[:2]
        refs = refs[2:]
    oga_ref, ogb_ref, sa_ref, sb_ref, qf, kf, vf, of = refs

    @pl.when(pl.program_id(1) == 0)
    def _():
        if has_s0:
            sa_ref[...] = s0a_ref[...]
            sb_ref[...] = s0b_ref[...]
        else:
            sa_ref[...] = jnp.zeros_like(sa_ref)
            sb_ref[...] = jnp.zeros_like(sb_ref)

    ti = lax.broadcasted_iota(I32, (CHUNK, CHUNK), 0)
    si = lax.broadcasted_iota(I32, (CHUNK, CHUNK), 1)
    causal = ti >= si
    tri = causal.astype(BF16)
    n_chunks = max(rows // CHUNK, 1)
    rows_out = min(rows, CHUNK)

    def load(ref, c):
        if rows >= CHUNK:
            return ref[c * CHUNK:(c + 1) * CHUNK, :]
        x = ref[...]
        return jnp.concatenate([x, jnp.zeros((CHUNK - rows, x.shape[1]), x.dtype)], axis=0)

    mixers = ((GLA_H, GLA_DK, GLA_DV, ga_ref, qa_ref, ka_ref, va_ref, ra_ref, sa_ref, gna_ref, oga_ref),
              (HG_H, HG_DK, HG_DV, gb_ref, qb_ref, kb_ref, ib_ref, rb_ref, sb_ref, gnb_ref, ogb_ref))
    units = []
    worst = jnp.zeros((1, 1), F32)
    for c in range(n_chunks):
        for (nh, dk, dv, g_ref, q_ref, k_ref, v_ref, r_ref, s_ref, gn_ref, o_ref) in mixers:
            qs, kd, et, q_to, k_of, w = _rec_prep(tri, load(g_ref, c), load(q_ref, c),
                                                  load(k_ref, c))
            worst = jnp.maximum(worst, w)
            for h in range(nh):
                ks = slice(h * dk, (h + 1) * dk)
                units.append(dict(
                    c=c, h=h, dk=dk, dv=dv, vs=slice(h * dv, (h + 1) * dv), s_ref=s_ref,
                    gn_ref=gn_ref, o_ref=o_ref, v_ref=v_ref, r_ref=r_ref,
                    qs=qs[:, ks], kd=kd[:, ks], et=et[:, ks],
                    q_to=jnp.concatenate([x[:, ks] for x in q_to], axis=1),
                    k_of=jnp.concatenate([x[:, ks] for x in k_of], axis=1)))
    for u in units:
        u["a"] = lax.dot_general(u["q_to"], u["k_of"], (((1,), (1,)), ((), ())),
                                 preferred_element_type=F32)
    for u in units:
        u["lhs"] = jnp.concatenate([jnp.where(causal, u["a"], 0.0).astype(BF16),
                                    u["kd"].T.astype(BF16)], axis=0)
    for u in units:
        u["av"] = _dot(u["lhs"], load(u["v_ref"], u["c"])[:, u["vs"]])
    factored_ok = worst[0, 0] <= EXP_CLAMP

    @pl.when(factored_ok)
    def _():
        for u in units:
            dk, dv, h = u["dk"], u["dv"], u["h"]
            s = u["s_ref"][0, h]
            o = _dot(u["qs"], s.astype(BF16)) + u["av"][:CHUNK]
            et_col = jnp.broadcast_to(u["et"], (dk, dk)).T
            u["s_ref"][0, h] = s * jnp.tile(et_col, (1, dv // dk)) + u["av"][CHUNK:]
            ms = jnp.mean(o * o, axis=-1, keepdims=True)
            og = (o * lax.rsqrt(ms + EPS) * u["gn_ref"][...]
                  * load(u["r_ref"], u["c"])[:, u["vs"]].astype(F32))
            r0 = u["c"] * CHUNK
            u["o_ref"][r0:r0 + rows_out, u["vs"]] = og[:rows_out].astype(BF16)

    @pl.when(jnp.logical_not(factored_ok))
    def _():
        sublane = lax.broadcasted_iota(I32, (8, 1), 0)
        of[...] = jnp.zeros_like(of)
        for (nh, dk, dv, g_ref, q_ref, k_ref, v_ref, r_ref, s_ref, gn_ref, o_ref) in mixers:
            qf[:, :nh * dk] = q_ref[...].astype(F32)
            kf[:, :nh * dk] = k_ref[...].astype(F32)
            vf[:, :nh * dv] = v_ref[...].astype(F32)
            for h in range(nh):
                ks = slice(h * dk, (h + 1) * dk)
                vs = slice(h * dv, (h + 1) * dv)

                def frame(t, st, ks=ks, vs=vs, g_ref=g_ref):
                    rows8 = pl.ds(pl.multiple_of((t >> 3) << 3, 8), 8)
                    is_t = sublane == (t & 7)
                    at = lambda ref, cols: jnp.where(is_t, ref[rows8, cols], 0.0)
                    kv = lax.dot_general(at(vf, vs).astype(BF16), at(kf, ks).astype(BF16),
                                         (((0,), (0,)), ((), ())),
                                         preferred_element_type=F32)
                    g_t = jnp.sum(at(g_ref, ks), axis=0, keepdims=True)
                    st = st * jnp.exp(g_t) + kv
                    o8 = lax.dot_general(at(qf, ks).astype(BF16), st.astype(BF16),
                                         (((1,), (1,)), ((), ())),
                                         preferred_element_type=F32)
                    of[rows8, vs] = jnp.where(is_t, o8, of[rows8, vs])
                    return st

                s_ref[0, h] = lax.fori_loop(0, rows, frame, s_ref[0, h].T).T
                o = of[:, vs]
                ms = jnp.mean(o * o, axis=-1, keepdims=True)
                og = o * lax.rsqrt(ms + EPS) * gn_ref[...] * r_ref[:, vs].astype(F32)
                o_ref[:, vs] = og.astype(BF16)


def _rec_call(arrs, gna, gnb, n_seq, seq_len, rows, row0, s0=None):
    steps = seq_len // rows
    blk0 = row0 // rows
    row_in = lambda b, t: (blk0 + b * steps + t, 0)
    row_out = lambda b, t: (b * steps + t, 0)
    const = lambda b, t: (0, 0)
    st = lambda b, t: (b, 0, 0, 0)
    in_specs = [pl.BlockSpec((rows, a.shape[1]), row_in) for a in arrs]
    in_specs += [pl.BlockSpec(gna.shape, const), pl.BlockSpec(gnb.shape, const)]
    args = list(arrs) + [gna, gnb]
    if s0 is not None:
        in_specs += [pl.BlockSpec((1, GLA_H, GLA_DK, GLA_DV), st),
                     pl.BlockSpec((1, HG_H, HG_DK, HG_DV), st)]
        args += [s0[0], s0[1]]
    n_rows = n_seq * seq_len
    return pl.pallas_call(
        functools.partial(_rec_kernel, rows, s0 is not None),
        grid=(n_seq, steps),
        in_specs=in_specs,
        out_specs=[pl.BlockSpec((rows, D), row_out), pl.BlockSpec((rows, D), row_out),
                   pl.BlockSpec((1, GLA_H, GLA_DK, GLA_DV), st),
                   pl.BlockSpec((1, HG_H, HG_DK, HG_DV), st)],
        out_shape=[jax.ShapeDtypeStruct((n_rows, D), BF16), jax.ShapeDtypeStruct((n_rows, D), BF16),
                   jax.ShapeDtypeStruct((n_seq, GLA_H, GLA_DK, GLA_DV), F32),
                   jax.ShapeDtypeStruct((n_seq, HG_H, HG_DK, HG_DV), F32)],
        scratch_shapes=[pltpu.VMEM((rows, D), F32)] * 4,
        compiler_params=pltpu.CompilerParams(dimension_semantics=("arbitrary", "arbitrary"),
                                             vmem_limit_bytes=48 * 1024 * 1024),
        name="rec_s0" if s0 is not None else "rec",
    )(*args)


def _post_kernel(geom, xp_ref, xs_ref, oap_ref, oas_ref, obp_ref, obs_ref, ua_ref, ub_ref,
                 mhi_ref, mlo_ref, gf_ref, wba_ref, wbb_ref, wo_ref, wrh_ref, wrl_ref, br_ref,
                 x1_ref, h2_ref, lg_ref):
    i = pl.program_id(0)
    is_prompt = i < geom[0] // TM
    mod = _row_mod(i, geom, mhi_ref, mlo_ref)
    groups = [slice(j * POST_ROWS, (j + 1) * POST_ROWS) for j in range(TM // POST_ROWS)]
    pa = [_dot(jnp.where(is_prompt, oap_ref[g, :], oas_ref[g, :]), wba_ref[...]) for g in groups]
    pb = [_dot(jnp.where(is_prompt, obp_ref[g, :], obs_ref[g, :]), wbb_ref[...]) for g in groups]
    merged = [(ua_ref[g, :].astype(F32) * a + ub_ref[g, :].astype(F32) * b).astype(BF16)
              for g, a, b in zip(groups, pa, pb)]
    y = [_dot(m, wo_ref[...]) for m in merged]
    h2s = []
    for g, yg in zip(groups, y):
        x1 = jnp.where(is_prompt, xp_ref[g, :], xs_ref[g, :]) + mod[g, :D] * yg
        x1_ref[g, :] = x1
        ms = jnp.mean(x1 * x1, axis=-1, keepdims=True)
        h2 = x1 * lax.rsqrt(ms + EPS) * gf_ref[...] * (1.0 + mod[g, 2 * D:]) + mod[g, D:2 * D]
        h2_ref[g, :] = h2
        h2s.append(_split_bf16(h2))
    for g, (h_hi, h_lo) in zip(groups, h2s):
        lg_ref[g, :] = (_dot(h_hi, wrh_ref[...]) + _dot(h_lo, wrh_ref[...])
                        + _dot(h_hi, wrl_ref[...]) + br_ref[...])


def _post_call(geom, n_rows, xp, xs, oap, oas, obp, obs, ua, ub, mhi, mlo, g_ffn, wba, wbb, wo,
               wrh, wrl, br):
    n_tiles = n_rows // TM
    last_p = geom[0] // TM - 1
    const = lambda i: (0, 0)
    row = lambda i: (i, 0)
    prow = pl.BlockSpec((TM, D), lambda i: (jnp.minimum(i, last_p), 0))
    full = lambda a: pl.BlockSpec(a.shape, const)
    return pl.pallas_call(
        functools.partial(_post_kernel, geom),
        grid=(n_tiles,),
        in_specs=[prow, full(xs), prow, full(oas), prow, full(obs),
                  pl.BlockSpec((TM, D), row), pl.BlockSpec((TM, D), row),
                  full(mhi), full(mlo), full(g_ffn), full(wba), full(wbb), full(wo),
                  full(wrh), full(wrl), full(br)],
        out_specs=[pl.BlockSpec((TM, D), row), pl.BlockSpec((TM, D), row),
                   pl.BlockSpec((TM, LANES), row)],
        out_shape=[jax.ShapeDtypeStruct((n_rows, D), F32), jax.ShapeDtypeStruct((n_rows, D), F32),
                   jax.ShapeDtypeStruct((n_rows, LANES), F32)],
        compiler_params=_cparams(48),
        name="post",
    )(xp, xs, oap, oas, obp, obs, ua, ub, mhi, mlo, g_ffn, wba, wbb, wo, wrh, wrl, br)


def _multi_hot(te):
    lane = lax.broadcasted_iota(I32, (TM, LANES), 1)
    m = jnp.zeros((TM, LANES), F32)
    for k in range(TOP_K):
        m = m + (lane == te[:, k:k + 1]).astype(F32)
    return m


def _route_kernel(lg_ref, te_ref, tw_ref, cnt_ref):
    logit = lg_ref[...]
    lane = lax.broadcasted_iota(I32, (ROUTE_ROWS, LANES), 1)
    lane_f = lane.astype(F32)
    vals, idxs = [], []
    for _ in range(TOP_K):
        m = jnp.max(logit, axis=-1, keepdims=True)
        idx = jnp.min(jnp.where(logit == m, lane_f, float(LANES)), axis=-1, keepdims=True)
        vals.append(m)
        idxs.append(idx)
        logit = jnp.where(lane_f == idx, -jnp.inf, logit)
    es = [jnp.exp(v - vals[0]) for v in vals]
    den = es[0] + es[1] + es[2] + es[3]
    te = jnp.zeros((ROUTE_ROWS, LANES), F32)
    tw = jnp.zeros((ROUTE_ROWS, LANES), F32)
    for k in range(TOP_K):
        te = jnp.where(lane == k, idxs[k], te)
        tw = jnp.where(lane == k, es[k] / den, tw)
    te = te.astype(I32)
    te_ref[...] = te
    tw_ref[...] = tw
    for t in range(ROUTE_ROWS // TM):
        cnt_ref[t * 8:(t + 1) * 8, :] = jnp.broadcast_to(
            jnp.sum(_multi_hot(te[t * TM:(t + 1) * TM]), axis=0, keepdims=True), (8, LANES))


def _route_call(logits, n_rows):
    n_steps = n_rows // ROUTE_ROWS
    row = lambda i: (i, 0)
    tiles = ROUTE_ROWS // TM
    return pl.pallas_call(
        _route_kernel,
        grid=(n_steps,),
        in_specs=[pl.BlockSpec((ROUTE_ROWS, LANES), row)],
        out_specs=[pl.BlockSpec((ROUTE_ROWS, LANES), row), pl.BlockSpec((ROUTE_ROWS, LANES), row),
                   pl.BlockSpec((tiles * 8, LANES), row)],
        out_shape=[jax.ShapeDtypeStruct((n_rows, LANES), I32),
                   jax.ShapeDtypeStruct((n_rows, LANES), F32),
                   jax.ShapeDtypeStruct((n_rows // TM * 8, LANES), F32)],
        compiler_params=_cparams(32),
        name="route",
    )(logits)


def _wait_rows(rows, wait_n_rows):
    lax.fori_loop(0, rows >> _log2(WAIT_ROWS), lambda j, c: (wait_n_rows(WAIT_ROWS), c)[1], 0)
    lax.fori_loop(0, (rows & (WAIT_ROWS - 1)) >> _log2(RUN),
                  lambda j, c: (wait_n_rows(RUN), c)[1], 0)


def _for_each_run_piece(tab_ref, fn):
    for s, p in enumerate(RUN_PIECES):
        def body(j, c, s=s, p=p):
            fn(pl.multiple_of(tab_ref[0, 0, s * 2 * N_EXP + j], RUN),
               pl.multiple_of(tab_ref[0, 0, (s * 2 + 1) * N_EXP + j], RUN), p)
            return c

        lax.fori_loop(0, tab_ref[0, 0, TAB_COUNT + s], body, 0)


def _piece_table(run, run_src, run_dst):
    ids = jnp.arange(N_EXP, dtype=I32)
    earlier = (ids[None, :] < ids[:, None]).astype(I32)
    src_l, dst_l, cnt_l = [], [], []
    for p in RUN_PIECES:
        has = (run & p) != 0
        off = run - (run & (2 * p - 1))
        slot = jnp.sum(has[:, None, :] * earlier[None, :, :], axis=2)
        put = has[:, :, None] & (slot[:, :, None] == ids[None, None, :])
        src_l.append(jnp.sum(jnp.where(put, (run_src + off)[:, :, None], 0), axis=1))
        dst_l.append(jnp.sum(jnp.where(put, (run_dst + off)[:, :, None], 0), axis=1))
        cnt_l.append(jnp.sum(has.astype(I32), axis=1))
    cols = [a for pair in zip(src_l, dst_l) for a in pair]
    tab = jnp.concatenate(cols + [jnp.stack(cnt_l, axis=1), jnp.sum(run, axis=1, keepdims=True)],
                          axis=1)
    tab = jnp.pad(tab, ((0, 0), (0, TAB_LEN - tab.shape[1])))
    return tab.reshape(run.shape[0], 1, TAB_LEN)


def _dispatch_kernel(pend_ref, tab_ref, tprev_ref, te_ref, gsrc_ref, h2_ref, xs_ref, pos_ref, z_ref,
                     zeros, sem, zsem):
    @pl.when(pl.program_id(0) == 0)
    def _():
        zeros[...] = jnp.zeros_like(zeros)

        def zero_copy(e):
            start = pl.multiple_of(pend_ref[e] - BM, BM)
            return pltpu.make_async_copy(zeros, xs_ref.at[pl.ds(start, BM)], zsem)

        def nonempty(e):
            return pend_ref[e] > (pend_ref[e - 1] if e > 0 else 0)

        for e in range(N_EXP):
            pl.when(nonempty(e))(lambda e=e: zero_copy(e).start())
        for e in range(N_EXP):
            pl.when(nonempty(e))(lambda e=e: zero_copy(e).wait())

        def tail_copy(j):
            return pltpu.make_async_copy(zeros, xs_ref.at[pl.ds(pl.multiple_of(j * BM, BM), BM)],
                                         zsem)

        first, last = pend_ref[N_EXP - 1] // BM, xs_ref.shape[0] // BM
        lax.fori_loop(first, last, lambda j, c: (tail_copy(j).start(), c)[1], 0)
        lax.fori_loop(first, last, lambda j, c: (tail_copy(j).wait(), c)[1], 0)

    te = te_ref[...]
    ri = lax.broadcasted_iota(I32, (TM, TM), 0)
    ci = lax.broadcasted_iota(I32, (TM, TM), 1)
    rank = _dot((ri > ci).astype(BF16), _multi_hot(te).astype(BF16)) + gsrc_ref[0:1, :]
    lane = lax.broadcasted_iota(I32, (TM, LANES), 1)
    pos = jnp.zeros((TM, LANES), F32)
    for k in range(TOP_K):
        p = jnp.sum(jnp.where(lane == te[:, k:k + 1], rank, 0.0), axis=-1, keepdims=True)
        pos = jnp.where(lane == k, p, pos)
    pos_ref[...] = pos.astype(I32)
    pos_t = pos.T.astype(I32)
    row = lax.broadcasted_iota(I32, (SORT_ROWS, TM), 0)
    onehot = jnp.zeros((SORT_ROWS, TM), F32)
    for k in range(TOP_K):
        onehot = jnp.where(row == pos_t[k:k + 1, :], 1.0, onehot)
    i = pl.program_id(0)
    slot = lax.rem(i, 2)
    z_ref[slot] = _dot(onehot.astype(BF16), h2_ref[...].astype(BF16))

    _for_each_run_piece(tab_ref, lambda s, d, p: pltpu.make_async_copy(
        z_ref.at[slot, pl.ds(s, p)], xs_ref.at[pl.ds(d, p)], sem.at[slot]).start())

    def wait_tile(table_ref, which):
        _wait_rows(table_ref[0, 0, TAB_ROWS], lambda n: pltpu.make_async_copy(
            z_ref.at[which, pl.ds(0, n)], xs_ref.at[pl.ds(0, n)], sem.at[which]).wait())

    pl.when(i > 0)(lambda: wait_tile(tprev_ref, 1 - slot))
    pl.when(i == pl.num_programs(0) - 1)(lambda: wait_tile(tab_ref, slot))


def _dispatch_call(pend, tab, te, gsrc, h2, cap):
    n_tiles = tab.shape[0]
    row = lambda i, pe: (i, 0)
    return pl.pallas_call(
        _dispatch_kernel,
        grid_spec=pltpu.PrefetchScalarGridSpec(
            num_scalar_prefetch=1,
            grid=(n_tiles,),
            in_specs=[pl.BlockSpec((1, 1, TAB_LEN), lambda i, pe: (i, 0, 0), memory_space=pltpu.SMEM),
                      pl.BlockSpec((1, 1, TAB_LEN), lambda i, pe: (jnp.maximum(i - 1, 0), 0, 0),
                                   memory_space=pltpu.SMEM),
                      pl.BlockSpec((TM, LANES), row), pl.BlockSpec((8, LANES), row),
                      pl.BlockSpec((TM, D), row)],
            out_specs=[pl.BlockSpec(memory_space=pl.ANY), pl.BlockSpec((TM, LANES), row)],
            scratch_shapes=[pltpu.VMEM((2, SORT_ROWS, D), F32), pltpu.VMEM((BM, D), F32),
                            pltpu.SemaphoreType.DMA((2,)), pltpu.SemaphoreType.DMA]),
        out_shape=[jax.ShapeDtypeStruct((cap, D), F32),
                   jax.ShapeDtypeStruct((n_tiles * TM, LANES), I32)],
        compiler_params=_cparams(40),
        name="dispatch",
    )(pend, tab, tab, te, gsrc, h2)


def _expert_kernel(be_ref, nv_ref, seg_ref, nxt_ref, fill_ref, x_ref, wgu_hbm, bgu_ref, wd_hbm, bd_ref,
                   o_ref, wgu_f, wd_f, wgu_s, wd_s, sem):
    j = pl.program_id(0)
    jc = jnp.minimum(j, nv_ref[0] - 1)
    e = be_ref[jc]
    slot = lax.rem(seg_ref[jc], 2)
    first = (j == 0) | ((j < nv_ref[0]) & (e != be_ref[jnp.maximum(jc - 1, 0)]))

    def fetch(expert, into):
        return (pltpu.make_async_copy(wgu_hbm.at[expert], wgu_f.at[into], sem.at[0, into]),
                pltpu.make_async_copy(wd_hbm.at[expert], wd_f.at[into], sem.at[1, into]))

    @pl.when(j == 0)
    def _():
        for c in fetch(e, slot):
            c.start()

    @pl.when(first)
    def _():
        for c in fetch(e, slot):
            c.wait()

        @pl.when(nxt_ref[jc] >= 0)
        def _():
            for c in fetch(nxt_ref[jc], 1 - slot):
                c.start()

        wgu_s[...] = wgu_f[slot].astype(BF16)
        wd_s[...] = wd_f[slot].astype(BF16)

    def mlp(groups):
        gus = [_dot(x_ref[g, :].astype(BF16), wgu_s[...]) + bgu_ref[0] for g in groups]
        acts = []
        for gu in gus:
            gate = jnp.minimum(gu[:, :D], SWIGLU_LIMIT)
            up = jnp.clip(gu[:, D:], -SWIGLU_LIMIT, SWIGLU_LIMIT)
            acts.append(((up + 1.0) * (gate * _sigmoid(SWIGLU_ALPHA * gate))).astype(BF16))
        for g, act in zip(groups, acts):
            o_ref[g, :] = _dot(act, wd_s[...]) + bd_ref[0]

    lower, upper = slice(0, BM // 2), slice(BM // 2, BM)
    needs_upper = fill_ref[jc] > BM // 2

    @pl.when((j < nv_ref[0]) & needs_upper)
    def _():
        mlp([lower, upper])

    @pl.when((j < nv_ref[0]) & jnp.logical_not(needs_upper))
    def _():
        mlp([lower])
        o_ref[upper, :] = jnp.zeros((BM // 2, D), F32)

    @pl.when(j >= nv_ref[0])
    def _():
        o_ref[...] = jnp.zeros_like(o_ref)


def _expert_call(block_e, n_valid, seg, nxt, fill, xs, wgu, bgu, wd, bd):
    n_blocks = xs.shape[0] // BM
    blk = lambda j, be, nv, sg, nx, fl: (jnp.minimum(j, nv[0] - 1), 0)
    blk_out = lambda j, be, nv, sg, nx, fl: (j, 0)
    exp = lambda j, be, nv, sg, nx, fl: (be[jnp.minimum(j, nv[0] - 1)], 0, 0)
    return pl.pallas_call(
        _expert_kernel,
        grid_spec=pltpu.PrefetchScalarGridSpec(
            num_scalar_prefetch=5,
            grid=(n_blocks,),
            in_specs=[pl.BlockSpec((BM, D), blk),
                      pl.BlockSpec(memory_space=pl.ANY), pl.BlockSpec((1, 1, 2 * D), exp),
                      pl.BlockSpec(memory_space=pl.ANY), pl.BlockSpec((1, 1, D), exp)],
            out_specs=pl.BlockSpec((BM, D), blk_out),
            scratch_shapes=[pltpu.VMEM((2, D, 2 * D), F32), pltpu.VMEM((2, D, D), F32),
                            pltpu.VMEM((D, 2 * D), BF16), pltpu.VMEM((D, D), BF16),
                            pltpu.SemaphoreType.DMA((2, 2))]),
        out_shape=jax.ShapeDtypeStruct(xs.shape, F32),
        compiler_params=_cparams(56),
        name="expert",
    )(block_e, n_valid, seg, nxt, fill, xs, wgu, bgu, wd, bd)


def _combine_kernel(geom, tcur_ref, tnxt_ref, eo_ref, pos_ref, x1_ref, tw_ref, mhi_ref, mlo_ref,
                    gfin_ref, yp_ref, ys_ref, buf, sem):
    i = pl.program_id(0)
    n_prompt_tiles = geom[0] // TM
    slot = lax.rem(i, 2)

    def gather(tab_ref, into):
        _for_each_run_piece(tab_ref, lambda s, d, p: pltpu.make_async_copy(
            eo_ref.at[pl.ds(d, p)], buf.at[into, pl.ds(s, p)], sem.at[into]).start())

    @pl.when(i == 0)
    def _():
        buf[...] = jnp.zeros_like(buf)
        gather(tcur_ref, 0)

    @pl.when(i + 1 < pl.num_programs(0))
    def _():
        gather(tnxt_ref, 1 - slot)

    _wait_rows(tcur_ref[0, 0, TAB_ROWS], lambda n: pltpu.make_async_copy(
        eo_ref.at[pl.ds(0, n)], buf.at[slot, pl.ds(0, n)], sem.at[slot]).wait())
    gt2 = _row_mod(i, geom, mhi_ref, mlo_ref)
    tw = tw_ref[...]
    pos = pos_ref[...]
    col = lax.broadcasted_iota(I32, (TM, SORT_ROWS), 1)
    wsel = jnp.zeros((TM, SORT_ROWS), F32)
    for k in range(TOP_K):
        wsel = jnp.where(col == pos[:, k:k + 1], tw[:, k:k + 1], wsel)
    y = _dot(wsel.astype(BF16), buf[slot].astype(BF16))
    x2 = x1_ref[...] + gt2 * y
    ms = jnp.mean(x2 * x2, axis=-1, keepdims=True)
    out = x2 * lax.rsqrt(ms + EPS) * gfin_ref[...]

    @pl.when(i < n_prompt_tiles)
    def _():
        yp_ref[...] = out

    @pl.when(i >= n_prompt_tiles)
    def _():
        ys_ref[...] = out


def _combine_call(geom, n_rows, tab, eo, pos, x1, tw, mhi, mlo, g_final):
    n_tiles = n_rows // TM
    n_prompt = geom[0]
    last_p = n_prompt // TM - 1
    const = lambda i: (0, 0)
    row = lambda i: (i, 0)
    return pl.pallas_call(
        functools.partial(_combine_kernel, geom),
        grid=(n_tiles,),
        in_specs=[pl.BlockSpec((1, 1, TAB_LEN), lambda i: (i, 0, 0), memory_space=pltpu.SMEM),
                  pl.BlockSpec((1, 1, TAB_LEN), lambda i: (jnp.minimum(i + 1, n_tiles - 1), 0, 0),
                               memory_space=pltpu.SMEM),
                  pl.BlockSpec(memory_space=pl.ANY), pl.BlockSpec((TM, LANES), row),
                  pl.BlockSpec((TM, D), row), pl.BlockSpec((TM, LANES), row),
                  pl.BlockSpec(mhi.shape, const), pl.BlockSpec(mlo.shape, const),
                  pl.BlockSpec((1, D), const)],
        out_specs=[pl.BlockSpec((TM, D), lambda i: (jnp.minimum(i, last_p), 0)),
                   pl.BlockSpec((n_rows - n_prompt, D), const)],
        out_shape=[jax.ShapeDtypeStruct((n_prompt, D), F32),
                   jax.ShapeDtypeStruct((n_rows - n_prompt, D), F32)],
        scratch_shapes=[pltpu.VMEM((2, SORT_ROWS, D), F32), pltpu.SemaphoreType.DMA((2,))],
        compiler_params=_cparams(48),
        name="combine",
    )(tab, tab, eo, pos, x1, tw, mhi, mlo, g_final)


def kernel(x_prompt, x_sample, c_prompt, c_sample, state_gla, state_hgrn, w_ada, b_ada, g_norm_mix,
           g_norm_ffn, w_in, w_gla_gate2, b_gla_gate, g_gla_onorm, hgrn_lb_logits, g_hgrn_onorm,
           w_branch_a, w_branch_b, w_out, w_router, b_router, w_gate_up, b_gate_up, w_down, b_down,
           g_final):
    assert w_ada.shape[0] == 1, "single-layer trunk only"
    bp, lp, _ = x_prompt.shape
    bs, ls, _ = x_sample.shape
    n_p, n_s = bp * lp, bs * ls
    n = n_p + n_s
    assert n_p % TM == 0 and n_s == TM and lp % TM == 0 and lp % REC_ROWS == 0 and ls <= CHUNK
    assert bp + bs <= 32
    assert SORT_ROWS >= TM * TOP_K + N_EXP * (RUN - 1) and RUN_PIECES[0] == TM
    assert n % ROUTE_ROWS == 0 and ROUTE_ROWS % TM == 0
    geom = (n_p, lp, ls, bp)
    xp = x_prompt.reshape(n_p, D)
    xs = x_sample.reshape(n_s, D)

    c_all = jnp.zeros((32, D), F32).at[:bp].set(c_prompt).at[bp:bp + bs].set(c_sample)
    mod = _mod_call(c_all, w_ada[0], b_ada[0].reshape(1, 6 * D))
    sh1, sc1, gt1, sh2, sc2, gt2 = [mod[:, j * D:(j + 1) * D] for j in range(6)]
    def table_and_sample_rows(t):
        return t, jnp.repeat(t[bp:bp + bs], ls, axis=0)

    m1_hi, m1_lo = table_and_sample_rows(jnp.concatenate([sh1, sc1], axis=1))
    m2_hi, m2_lo = table_and_sample_rows(jnp.concatenate([gt1, sh2, sc2], axis=1))
    m3_hi, m3_lo = table_and_sample_rows(gt2)

    wi = w_in[0]
    wa = wi[:, :C_QB].astype(BF16)
    wb = wi[:, C_QB + GATE_RANK:].astype(BF16)
    wlr = jnp.pad(wi[:, 3072:3072 + GATE_RANK], ((0, 0), (0, LANES - GATE_RANK))).astype(BF16)
    w2 = jnp.pad(w_gla_gate2[0], ((0, LANES - GATE_RANK), (0, 0))).astype(BF16)
    arrs = _proj_call(geom, n, xp, xs, m1_hi, m1_lo, g_norm_mix[0].reshape(1, D), hgrn_lb_logits,
                      wa, wb, wlr, w2, b_gla_gate[0].reshape(1, -1))
    qa, ka, va, ra, ga, qb, kb, gb, ib, rb, ua, ub = arrs
    rec_in = (qa, ka, va, ra, ga, qb, kb, gb, ib, rb)

    gna = g_gla_onorm[0].reshape(1, GLA_DV)
    gnb = g_hgrn_onorm[0].reshape(1, HG_DV)
    oap, obp, sa_p, sb_p = _rec_call(rec_in, gna, gnb, bp, lp, REC_ROWS, 0)
    oas, obs, sa_s, sb_s = _rec_call(rec_in, gna, gnb, bs, ls, ls, n_p,
                                     s0=(state_gla[0], state_hgrn[0]))

    wr = jnp.pad(w_router[0], ((0, 0), (0, LANES - N_EXP)))
    wr_hi, wr_lo = _split_bf16(wr)
    br = jnp.pad(b_router[0], (0, LANES - N_EXP), constant_values=NEG).reshape(1, LANES)
    x1, h2, logits = _post_call(geom, n, xp, xs, oap, oas, obp, obs, ua, ub, m2_hi, m2_lo,
                                g_norm_ffn[0].reshape(1, D), w_branch_a[0].astype(BF16),
                                w_branch_b[0].astype(BF16), w_out[0].astype(BF16), wr_hi, wr_lo, br)

    n_tiles = n // TM
    te, tw, cnt = _route_call(logits, n)
    run = (cnt[::8, :N_EXP].astype(I32) + RUN - 1) // RUN * RUN
    counts = jnp.sum(run, axis=0)
    seg_len = (counts + BM - 1) // BM * BM
    pend = jnp.cumsum(seg_len).astype(I32)
    run_dst = (pend - seg_len)[None, :] + jnp.cumsum(run, axis=0) - run
    run_src = jnp.cumsum(run, axis=1) - run
    tab = _piece_table(run, run_src, run_dst)
    gsrc = jnp.repeat(jnp.pad(run_src.astype(F32), ((0, 0), (0, LANES - N_EXP))), 8, axis=0)
    cap = (n * TOP_K + n_tiles * N_EXP * (RUN - 1) + N_EXP * (BM - 1) + BM - 1) // BM * BM
    block_start = jnp.arange(cap // BM, dtype=I32) * BM
    block_e = jnp.minimum(jnp.sum((pend[None, :] <= block_start[:, None]).astype(I32), axis=1),
                          N_EXP - 1)
    n_valid = pend[-1:] // BM
    ids = jnp.arange(N_EXP, dtype=I32)
    nonempty = counts > 0
    later = jnp.where(nonempty[None, :] & (ids[None, :] > ids[:, None]), ids[None, :], N_EXP)
    next_e = jnp.min(later, axis=1)
    next_e = jnp.where(next_e == N_EXP, -1, next_e)
    is_e = block_e[:, None] == ids[None, :]
    seg = jnp.sum(jnp.where(is_e, (jnp.cumsum(nonempty.astype(I32)) - 1)[None, :], 0), axis=1)
    nxt = jnp.sum(jnp.where(is_e, next_e[None, :], 0), axis=1)
    fill = jnp.clip(jnp.sum(jnp.where(is_e, (pend - seg_len + counts)[None, :], 0), axis=1)
                    - block_start, 0, BM)
    xs_sorted, pos = _dispatch_call(pend, tab, te, gsrc, h2, cap)
    eo = _expert_call(block_e, n_valid, seg, nxt, fill, xs_sorted, w_gate_up[0],
                      b_gate_up[0].reshape(N_EXP, 1, -1), w_down[0], b_down[0].reshape(N_EXP, 1, -1))

    yp, ys = _combine_call(geom, n, tab, eo, pos, x1, tw, m3_hi, m3_lo, g_final.reshape(1, D))
    return (yp.reshape(bp, lp, D), ys.reshape(bs, ls, D),
            sa_p[None], sb_p[None], sa_s[None], sb_s[None])
```

```python
import functools

import jax
import jax.numpy as jnp
from jax import lax
from jax.experimental import pallas as pl
from jax.experimental.pallas import tpu as pltpu

F32 = jnp.float32
BF16 = jnp.bfloat16
I32 = jnp.int32

EPS = 1e-6
D = 1024
GLA_H, GLA_DK, GLA_DV = 4, 128, 256
HG_H, HG_DK, HG_DV = 8, 128, 128
GATE_RANK = 16
GATE_NORM = 16.0
N_EXP = 32
TOP_K = 4
SWIGLU_LIMIT = 7.0
SWIGLU_ALPHA = 1.702

LANES = 128
TM = 256
CHUNK = 128
SUB = 32
NSUB = CHUNK // SUB
REC_ROWS = 2 * CHUNK
POST_ROWS = 128
EXP_CLAMP = 80.0
BM = 512
ROUTE_ROWS = 1280
RUN = 8
RUN_PIECES = (256, 128, 64, 32, 16, 8)
WAIT_ROWS = 128
SORT_ROWS = TM * TOP_K + 256
TAB_COUNT = 2 * N_EXP * len(RUN_PIECES)
TAB_ROWS = TAB_COUNT + len(RUN_PIECES)
TAB_LEN = 512
NEG = -1e30

C_QA, C_KA, C_VA, C_RA, C_QB, C_FB, C_IB, C_RB, C_UA, C_UB, C_END = (
    0, 512, 1024, 2048, 3072, 4096, 5120, 6144, 7168, 8192, 9216)


def _sigmoid(x):
    return 1.0 / (1.0 + jnp.exp(-x))


def _dot(a, b):
    return jnp.dot(a, b, preferred_element_type=F32)


def _split_bf16(x):
    hi = x.astype(BF16)
    lo = (x - hi.astype(F32)).astype(BF16)
    return hi, lo


def _log2(n):
    assert n > 0 and n & (n - 1) == 0, "power of two expected"
    return n.bit_length() - 1


def _cparams(vmem_mb):
    return pltpu.CompilerParams(dimension_semantics=("arbitrary",),
                                vmem_limit_bytes=vmem_mb * 1024 * 1024)


def _mod_kernel(c_ref, w_ref, b_ref, o_ref):
    c = c_ref[...]
    s = (c * _sigmoid(c)).astype(BF16)
    o_ref[...] = _dot(s, w_ref[...].astype(BF16)) + b_ref[...]


def _mod_call(c_all, w_ada, b_ada):
    rows = c_all.shape[0]
    return pl.pallas_call(
        _mod_kernel,
        grid=(6,),
        in_specs=[pl.BlockSpec((rows, D), lambda j: (0, 0)),
                  pl.BlockSpec((D, D), lambda j: (0, j)),
                  pl.BlockSpec((1, D), lambda j: (0, j))],
        out_specs=pl.BlockSpec((rows, D), lambda j: (0, j)),
        out_shape=jax.ShapeDtypeStruct((rows, 6 * D), F32),
        compiler_params=_cparams(32),
        name="mod",
    )(c_all, w_ada, b_ada)


def _row_mod(i, geom, table_ref, sample_ref):
    n_prompt_rows, prompt_len, _, n_prompt_seq = geom
    seq = jnp.minimum((i * TM) >> _log2(prompt_len), n_prompt_seq - 1)
    return jnp.where(i < n_prompt_rows // TM, table_ref[pl.ds(seq, 1), :], sample_ref[...])


def _proj_kernel(geom, xp_ref, xs_ref, mhi_ref, mlo_ref, g_ref, lbl_ref, wa_ref, wb_ref, wlr_ref,
                 w2_ref, bg_ref,
                 qa_ref, ka_ref, va_ref, ra_ref, ga_ref, qb_ref, kb_ref, gb_ref, ib_ref,
                 rb_ref, ua_ref, ub_ref, wd_ref):
    i = pl.program_id(0)
    n_prompt_tiles = geom[0] // TM
    x = jnp.where(i < n_prompt_tiles, xp_ref[...], xs_ref[...])
    mod = _row_mod(i, geom, mhi_ref, mlo_ref)
    sh, sc = mod[:, :D], mod[:, D:]
    ms = jnp.mean(x * x, axis=-1, keepdims=True)
    h = x * lax.rsqrt(ms + EPS) * g_ref[...]
    hb = (h * (1.0 + sc) + sh).astype(BF16)

    def proj(a, b):
        if b <= C_QB:
            return _dot(hb, wa_ref[:, a:b])
        return _dot(hb, wb_ref[:, a - C_QB:b - C_QB])

    cols = (C_QA, C_KA, C_VA, C_RA, C_QB, C_FB, C_IB, C_RB, C_UA, C_UB, C_END)
    z_qa, z_ka, z_va, z_ra, z_qb, z_fb, z_ib, z_rb, z_ua, z_ub = [
        proj(a, b) for a, b in zip(cols[:-1], cols[1:])]
    lr_hi, lr_lo = _split_bf16(_dot(hb, wlr_ref[...]))
    qa_ref[...] = (z_qa * GLA_DK ** -0.5).astype(BF16)
    ka_ref[...] = z_ka.astype(BF16)
    va_ref[...] = z_va.astype(BF16)
    ra_ref[...] = (z_ra * _sigmoid(z_ra)).astype(BF16)
    xg = _dot(lr_hi, w2_ref[...]) + _dot(lr_lo, w2_ref[...]) + bg_ref[...]
    ga = (jnp.minimum(xg, 0.0) - jnp.log1p(jnp.exp(-jnp.abs(xg)))) * (1.0 / GATE_NORM)
    ga_ref[...] = ga
    qb_ref[...] = (z_qb * _sigmoid(z_qb) * HG_DK ** -0.5).astype(BF16)
    lbl = lbl_ref[...]
    e = jnp.exp(lbl - jnp.max(lbl, axis=0, keepdims=True))
    lb = e[0:1, :] / jnp.sum(e, axis=0, keepdims=True)
    kb_ref[...] = ((1.0 - lb) * _sigmoid(-z_fb)).astype(BF16)
    gb = jnp.log(lb + (1.0 - lb) * _sigmoid(z_fb))
    gb_ref[...] = gb
    worst = jnp.zeros((1, 1), F32)
    for g in (ga, gb):
        for j in range(TM // SUB):
            total = jnp.sum(g[j * SUB:(j + 1) * SUB, :], axis=0, keepdims=True)
            worst = jnp.maximum(worst, jnp.max(-total, axis=1, keepdims=True))
    wd_ref[...] = jnp.broadcast_to(worst, (8, LANES))
    ib_ref[...] = z_ib.astype(BF16)
    rb_ref[...] = (z_rb * _sigmoid(z_rb)).astype(BF16)
    ua_ref[...] = _sigmoid(z_ua).astype(BF16)
    ub_ref[...] = _sigmoid(z_ub).astype(BF16)


def _proj_call(geom, n_rows, xp, xs, mhi, mlo, g_mix, lbl, wa, wb, wlr, w2, bg):
    n_tiles = n_rows // TM
    last_p = geom[0] // TM - 1
    const = lambda i: (0, 0)
    row = lambda i: (i, 0)
    widths = [(512, BF16), (512, BF16), (D, BF16), (D, BF16), (512, F32), (D, BF16), (D, BF16),
              (D, F32), (D, BF16), (D, BF16), (D, BF16), (D, BF16)]
    return pl.pallas_call(
        functools.partial(_proj_kernel, geom),
        grid=(n_tiles,),
        in_specs=[pl.BlockSpec((TM, D), lambda i: (jnp.minimum(i, last_p), 0)),
                  pl.BlockSpec(xs.shape, const),
                  pl.BlockSpec(mhi.shape, const), pl.BlockSpec(mlo.shape, const),
                  pl.BlockSpec((1, D), const), pl.BlockSpec(lbl.shape, const),
                  pl.BlockSpec(wa.shape, const, pipeline_mode=pl.Buffered(1)),
                  pl.BlockSpec(wb.shape, const, pipeline_mode=pl.Buffered(1)),
                  pl.BlockSpec(wlr.shape, const), pl.BlockSpec(w2.shape, const),
                  pl.BlockSpec(bg.shape, const)],
        out_specs=[pl.BlockSpec((TM, w), row) for w, _ in widths] + [pl.BlockSpec((8, LANES), row)],
        out_shape=([jax.ShapeDtypeStruct((n_rows, w), dt) for w, dt in widths]
                   + [jax.ShapeDtypeStruct((n_tiles * 8, LANES), F32)]),
        compiler_params=_cparams(56),
        name="proj",
    )(xp, xs, mhi, mlo, g_mix, lbl, wa, wb, wlr, w2, bg)


def _rec_prep(tri, g, q, k):
    g_hi, g_lo = _split_bf16(g)
    cum = _dot(tri, g_hi) + _dot(tri, g_lo)
    width = cum.shape[1]
    tot = cum[CHUNK - 1:CHUNK, :]
    refs = [jnp.zeros((1, width), F32)] + [cum[j * SUB - 1:j * SUB, :] for j in range(1, NSUB)]
    d = cum - jnp.concatenate([jnp.broadcast_to(b, (SUB, width)) for b in refs], axis=0)
    qn = q.astype(F32) * jnp.exp(d)
    kn = k.astype(F32) * jnp.exp(jnp.minimum(-d, EXP_CLAMP))
    sub = lambda x, j: x[j * SUB:(j + 1) * SUB, :]
    qs = jnp.concatenate([sub(qn, j) * jnp.exp(refs[j]) for j in range(NSUB)],
                         axis=0).astype(BF16)
    kd = jnp.concatenate([sub(kn, j) * jnp.exp(tot - refs[j]) for j in range(NSUB)], axis=0)
    et = jnp.exp(tot)
    zero_rows = lambda n: jnp.zeros((n, width), F32)
    q_to, k_of = [], []
    for j in range(NSUB):
        parts = [zero_rows(j * SUB)] if j else []
        parts += [sub(qn, i) * jnp.exp(refs[i] - refs[j]) if i > j else sub(qn, i)
                  for i in range(j, NSUB)]
        q_to.append(jnp.concatenate(parts, axis=0).astype(BF16))
        parts = ([zero_rows(j * SUB)] if j else []) + [sub(kn, j)]
        if j + 1 < NSUB:
            parts.append(zero_rows(CHUNK - (j + 1) * SUB))
        k_of.append(jnp.concatenate(parts, axis=0).astype(BF16))
    return qs, kd, et, q_to, k_of


def _rec_kernel(rows, has_s0, fp_ref, *refs):
    (qa_ref, ka_ref, va_ref, ra_ref, ga_ref, qb_ref, kb_ref, gb_ref, ib_ref, rb_ref,
     gna_ref, gnb_ref) = refs[:12]
    refs = refs[12:]
    if has_s0:
        s0a_ref, s0b_ref = refs[:2]
        refs = refs[2:]
    oga_ref, ogb_ref, sa_ref, sb_ref, qf, kf, vf, of = refs

    @pl.when(pl.program_id(1) == 0)
    def _():
        if has_s0:
            sa_ref[...] = s0a_ref[...]
            sb_ref[...] = s0b_ref[...]
        else:
            sa_ref[...] = jnp.zeros_like(sa_ref)
            sb_ref[...] = jnp.zeros_like(sb_ref)

    ti = lax.broadcasted_iota(I32, (CHUNK, CHUNK), 0)
    si = lax.broadcasted_iota(I32, (CHUNK, CHUNK), 1)
    causal = ti >= si
    tri = causal.astype(BF16)
    n_chunks = max(rows // CHUNK, 1)
    rows_out = min(rows, CHUNK)

    def load(ref, c):
        if rows >= CHUNK:
            return ref[c * CHUNK:(c + 1) * CHUNK, :]
        x = ref[...]
        return jnp.concatenate([x, jnp.zeros((CHUNK - rows, x.shape[1]), x.dtype)], axis=0)

    mixers = ((GLA_H, GLA_DK, GLA_DV, ga_ref, qa_ref, ka_ref, va_ref, ra_ref, sa_ref, gna_ref, oga_ref),
              (HG_H, HG_DK, HG_DV, gb_ref, qb_ref, kb_ref, ib_ref, rb_ref, sb_ref, gnb_ref, ogb_ref))
    use_frames = fp_ref[pl.program_id(0) * pl.num_programs(1) + pl.program_id(1)] != 0
    pl.when(jnp.logical_not(use_frames))(
        lambda: _rec_factored(tri, causal, load, mixers, n_chunks, rows_out))
    pl.when(use_frames)(lambda: _rec_frames(rows, mixers, qf, kf, vf, of))


def _rec_factored(tri, causal, load, mixers, n_chunks, rows_out):
    units = []
    for c in range(n_chunks):
        for (nh, dk, dv, g_ref, q_ref, k_ref, v_ref, r_ref, s_ref, gn_ref, o_ref) in mixers:
            qs, kd, et, q_to, k_of = _rec_prep(tri, load(g_ref, c), load(q_ref, c), load(k_ref, c))
            for h in range(nh):
                ks = slice(h * dk, (h + 1) * dk)
                units.append(dict(
                    c=c, h=h, dk=dk, dv=dv, vs=slice(h * dv, (h + 1) * dv), s_ref=s_ref,
                    gn_ref=gn_ref, o_ref=o_ref, v_ref=v_ref, r_ref=r_ref,
                    qs=qs[:, ks], kd=kd[:, ks], et=et[:, ks],
                    q_to=jnp.concatenate([x[:, ks] for x in q_to], axis=1),
                    k_of=jnp.concatenate([x[:, ks] for x in k_of], axis=1)))
    for u in units:
        u["a"] = lax.dot_general(u["q_to"], u["k_of"], (((1,), (1,)), ((), ())),
                                 preferred_element_type=F32)
    for u in units:
        u["lhs"] = jnp.concatenate([jnp.where(causal, u["a"], 0.0).astype(BF16),
                                    u["kd"].T.astype(BF16)], axis=0)
    for u in units:
        u["av"] = _dot(u["lhs"], load(u["v_ref"], u["c"])[:, u["vs"]])
    for u in units:
        dk, dv, h = u["dk"], u["dv"], u["h"]
        s = u["s_ref"][0, h]
        o = _dot(u["qs"], s.astype(BF16)) + u["av"][:CHUNK]
        et_col = jnp.broadcast_to(u["et"], (dk, dk)).T
        u["s_ref"][0, h] = s * jnp.tile(et_col, (1, dv // dk)) + u["av"][CHUNK:]
        ms = jnp.mean(o * o, axis=-1, keepdims=True)
        og = (o * lax.rsqrt(ms + EPS) * u["gn_ref"][...]
              * load(u["r_ref"], u["c"])[:, u["vs"]].astype(F32))
        r0 = u["c"] * CHUNK
        u["o_ref"][r0:r0 + rows_out, u["vs"]] = og[:rows_out].astype(BF16)


def _rec_frames(rows, mixers, qf, kf, vf, of):
    sublane = lax.broadcasted_iota(I32, (8, 1), 0)
    of[...] = jnp.zeros_like(of)
    for (nh, dk, dv, g_ref, q_ref, k_ref, v_ref, r_ref, s_ref, gn_ref, o_ref) in mixers:
        qf[:, :nh * dk] = q_ref[...].astype(F32)
        kf[:, :nh * dk] = k_ref[...].astype(F32)
        vf[:, :nh * dv] = v_ref[...].astype(F32)
        for h in range(nh):
            ks = slice(h * dk, (h + 1) * dk)
            vs = slice(h * dv, (h + 1) * dv)

            def frame(t, st, ks=ks, vs=vs, g_ref=g_ref):
                rows8 = pl.ds(pl.multiple_of((t >> 3) << 3, 8), 8)
                is_t = sublane == (t & 7)
                at = lambda ref, cols: jnp.where(is_t, ref[rows8, cols], 0.0)
                kv = lax.dot_general(at(vf, vs).astype(BF16), at(kf, ks).astype(BF16),
                                     (((0,), (0,)), ((), ())),
                                     preferred_element_type=F32)
                g_t = jnp.sum(at(g_ref, ks), axis=0, keepdims=True)
                st = st * jnp.exp(g_t) + kv
                o8 = lax.dot_general(at(qf, ks).astype(BF16), st.astype(BF16),
                                     (((1,), (1,)), ((), ())),
                                     preferred_element_type=F32)
                of[rows8, vs] = jnp.where(is_t, o8, of[rows8, vs])
                return st

            s_ref[0, h] = lax.fori_loop(0, rows, frame, s_ref[0, h].T).T
            o = of[:, vs]
            ms = jnp.mean(o * o, axis=-1, keepdims=True)
            og = o * lax.rsqrt(ms + EPS) * gn_ref[...] * r_ref[:, vs].astype(F32)
            o_ref[:, vs] = og.astype(BF16)


def _rec_call(frame_path, arrs, gna, gnb, n_seq, seq_len, rows, row0, s0=None):
    steps = seq_len // rows
    blk0 = row0 // rows
    row_in = lambda b, t, fp: (blk0 + b * steps + t, 0)
    row_out = lambda b, t, fp: (b * steps + t, 0)
    const = lambda b, t, fp: (0, 0)
    st = lambda b, t, fp: (b, 0, 0, 0)
    in_specs = [pl.BlockSpec((rows, a.shape[1]), row_in) for a in arrs]
    in_specs += [pl.BlockSpec(gna.shape, const), pl.BlockSpec(gnb.shape, const)]
    args = list(arrs) + [gna, gnb]
    if s0 is not None:
        in_specs += [pl.BlockSpec((1, GLA_H, GLA_DK, GLA_DV), st),
                     pl.BlockSpec((1, HG_H, HG_DK, HG_DV), st)]
        args += [s0[0], s0[1]]
    n_rows = n_seq * seq_len
    return pl.pallas_call(
        functools.partial(_rec_kernel, rows, s0 is not None),
        grid_spec=pltpu.PrefetchScalarGridSpec(
            num_scalar_prefetch=1,
            grid=(n_seq, steps),
            in_specs=in_specs,
            out_specs=[pl.BlockSpec((rows, D), row_out), pl.BlockSpec((rows, D), row_out),
                       pl.BlockSpec((1, GLA_H, GLA_DK, GLA_DV), st),
                       pl.BlockSpec((1, HG_H, HG_DK, HG_DV), st)],
            scratch_shapes=[pltpu.VMEM((rows, D), F32)] * 4),
        out_shape=[jax.ShapeDtypeStruct((n_rows, D), BF16), jax.ShapeDtypeStruct((n_rows, D), BF16),
                   jax.ShapeDtypeStruct((n_seq, GLA_H, GLA_DK, GLA_DV), F32),
                   jax.ShapeDtypeStruct((n_seq, HG_H, HG_DK, HG_DV), F32)],
        compiler_params=pltpu.CompilerParams(dimension_semantics=("arbitrary", "arbitrary"),
                                             vmem_limit_bytes=48 * 1024 * 1024),
        name="rec_s0" if s0 is not None else "rec",
    )(frame_path, *args)


def _post_kernel(geom, xp_ref, xs_ref, oap_ref, oas_ref, obp_ref, obs_ref, ua_ref, ub_ref,
                 mhi_ref, mlo_ref, gf_ref, wba_ref, wbb_ref, wo_ref, wrh_ref, wrl_ref, br_ref,
                 x1_ref, h2_ref, lg_ref):
    i = pl.program_id(0)
    is_prompt = i < geom[0] // TM
    mod = _row_mod(i, geom, mhi_ref, mlo_ref)
    groups = [slice(j * POST_ROWS, (j + 1) * POST_ROWS) for j in range(TM // POST_ROWS)]
    pa = [_dot(jnp.where(is_prompt, oap_ref[g, :], oas_ref[g, :]), wba_ref[...]) for g in groups]
    pb = [_dot(jnp.where(is_prompt, obp_ref[g, :], obs_ref[g, :]), wbb_ref[...]) for g in groups]
    merged = [(ua_ref[g, :].astype(F32) * a + ub_ref[g, :].astype(F32) * b).astype(BF16)
              for g, a, b in zip(groups, pa, pb)]
    y = [_dot(m, wo_ref[...]) for m in merged]
    h2s = []
    for g, yg in zip(groups, y):
        x1 = jnp.where(is_prompt, xp_ref[g, :], xs_ref[g, :]) + mod[g, :D] * yg
        x1_ref[g, :] = x1
        ms = jnp.mean(x1 * x1, axis=-1, keepdims=True)
        h2 = x1 * lax.rsqrt(ms + EPS) * gf_ref[...] * (1.0 + mod[g, 2 * D:]) + mod[g, D:2 * D]
        h2_ref[g, :] = h2
        h2s.append(_split_bf16(h2))
    for g, (h_hi, h_lo) in zip(groups, h2s):
        lg_ref[g, :] = (_dot(h_hi, wrh_ref[...]) + _dot(h_lo, wrh_ref[...])
                        + _dot(h_hi, wrl_ref[...]) + br_ref[...])


def _post_call(geom, n_rows, xp, xs, oap, oas, obp, obs, ua, ub, mhi, mlo, g_ffn, wba, wbb, wo,
               wrh, wrl, br):
    n_tiles = n_rows // TM
    last_p = geom[0] // TM - 1
    const = lambda i: (0, 0)
    row = lambda i: (i, 0)
    prow = pl.BlockSpec((TM, D), lambda i: (jnp.minimum(i, last_p), 0))
    full = lambda a: pl.BlockSpec(a.shape, const)
    return pl.pallas_call(
        functools.partial(_post_kernel, geom),
        grid=(n_tiles,),
        in_specs=[prow, full(xs), prow, full(oas), prow, full(obs),
                  pl.BlockSpec((TM, D), row), pl.BlockSpec((TM, D), row),
                  full(mhi), full(mlo), full(g_ffn), full(wba), full(wbb), full(wo),
                  full(wrh), full(wrl), full(br)],
        out_specs=[pl.BlockSpec((TM, D), row), pl.BlockSpec((TM, D), row),
                   pl.BlockSpec((TM, LANES), row)],
        out_shape=[jax.ShapeDtypeStruct((n_rows, D), F32), jax.ShapeDtypeStruct((n_rows, D), F32),
                   jax.ShapeDtypeStruct((n_rows, LANES), F32)],
        compiler_params=_cparams(48),
        name="post",
    )(xp, xs, oap, oas, obp, obs, ua, ub, mhi, mlo, g_ffn, wba, wbb, wo, wrh, wrl, br)


def _multi_hot(te):
    lane = lax.broadcasted_iota(I32, (TM, LANES), 1)
    m = jnp.zeros((TM, LANES), F32)
    for k in range(TOP_K):
        m = m + (lane == te[:, k:k + 1]).astype(F32)
    return m


def _route_kernel(lg_ref, te_ref, tw_ref, cnt_ref):
    logit = lg_ref[...]
    lane = lax.broadcasted_iota(I32, (ROUTE_ROWS, LANES), 1)
    lane_f = lane.astype(F32)
    vals, idxs = [], []
    for _ in range(TOP_K):
        m = jnp.max(logit, axis=-1, keepdims=True)
        idx = jnp.min(jnp.where(logit == m, lane_f, float(LANES)), axis=-1, keepdims=True)
        vals.append(m)
        idxs.append(idx)
        logit = jnp.where(lane_f == idx, -jnp.inf, logit)
    es = [jnp.exp(v - vals[0]) for v in vals]
    den = es[0] + es[1] + es[2] + es[3]
    te = jnp.zeros((ROUTE_ROWS, LANES), F32)
    tw = jnp.zeros((ROUTE_ROWS, LANES), F32)
    for k in range(TOP_K):
        te = jnp.where(lane == k, idxs[k], te)
        tw = jnp.where(lane == k, es[k] / den, tw)
    te = te.astype(I32)
    te_ref[...] = te
    tw_ref[...] = tw
    for t in range(ROUTE_ROWS // TM):
        cnt_ref[t * 8:(t + 1) * 8, :] = jnp.broadcast_to(
            jnp.sum(_multi_hot(te[t * TM:(t + 1) * TM]), axis=0, keepdims=True), (8, LANES))


def _route_call(logits, n_rows):
    n_steps = n_rows // ROUTE_ROWS
    row = lambda i: (i, 0)
    tiles = ROUTE_ROWS // TM
    return pl.pallas_call(
        _route_kernel,
        grid=(n_steps,),
        in_specs=[pl.BlockSpec((ROUTE_ROWS, LANES), row)],
        out_specs=[pl.BlockSpec((ROUTE_ROWS, LANES), row), pl.BlockSpec((ROUTE_ROWS, LANES), row),
                   pl.BlockSpec((tiles * 8, LANES), row)],
        out_shape=[jax.ShapeDtypeStruct((n_rows, LANES), I32),
                   jax.ShapeDtypeStruct((n_rows, LANES), F32),
                   jax.ShapeDtypeStruct((n_rows // TM * 8, LANES), F32)],
        compiler_params=_cparams(32),
        name="route",
    )(logits)


def _wait_rows(rows, wait_n_rows):
    lax.fori_loop(0, rows >> _log2(WAIT_ROWS), lambda j, c: (wait_n_rows(WAIT_ROWS), c)[1], 0)
    lax.fori_loop(0, (rows & (WAIT_ROWS - 1)) >> _log2(RUN),
                  lambda j, c: (wait_n_rows(RUN), c)[1], 0)


def _for_each_run_piece(tab_ref, fn):
    for s, p in enumerate(RUN_PIECES):
        def body(j, c, s=s, p=p):
            fn(pl.multiple_of(tab_ref[0, 0, s * 2 * N_EXP + j], RUN),
               pl.multiple_of(tab_ref[0, 0, (s * 2 + 1) * N_EXP + j], RUN), p)
            return c

        lax.fori_loop(0, tab_ref[0, 0, TAB_COUNT + s], body, 0)


def _piece_table(run, run_src, run_dst):
    ids = jnp.arange(N_EXP, dtype=I32)
    earlier = (ids[None, :] < ids[:, None]).astype(I32)
    src_l, dst_l, cnt_l = [], [], []
    for p in RUN_PIECES:
        has = (run & p) != 0
        off = run - (run & (2 * p - 1))
        slot = jnp.sum(has[:, None, :] * earlier[None, :, :], axis=2)
        put = has[:, :, None] & (slot[:, :, None] == ids[None, None, :])
        src_l.append(jnp.sum(jnp.where(put, (run_src + off)[:, :, None], 0), axis=1))
        dst_l.append(jnp.sum(jnp.where(put, (run_dst + off)[:, :, None], 0), axis=1))
        cnt_l.append(jnp.sum(has.astype(I32), axis=1))
    cols = [a for pair in zip(src_l, dst_l) for a in pair]
    tab = jnp.concatenate(cols + [jnp.stack(cnt_l, axis=1), jnp.sum(run, axis=1, keepdims=True)],
                          axis=1)
    tab = jnp.pad(tab, ((0, 0), (0, TAB_LEN - tab.shape[1])))
    return tab.reshape(run.shape[0], 1, TAB_LEN)


def _dispatch_kernel(pend_ref, tab_ref, tprev_ref, te_ref, gsrc_ref, h2_ref, xs_ref, pos_ref, z_ref,
                     zeros, sem, zsem):
    @pl.when(pl.program_id(0) == 0)
    def _():
        zeros[...] = jnp.zeros_like(zeros)

        def zero_copy(e):
            start = pl.multiple_of(pend_ref[e] - BM, BM)
            return pltpu.make_async_copy(zeros, xs_ref.at[pl.ds(start, BM)], zsem)

        def nonempty(e):
            return pend_ref[e] > (pend_ref[e - 1] if e > 0 else 0)

        for e in range(N_EXP):
            pl.when(nonempty(e))(lambda e=e: zero_copy(e).start())
        for e in range(N_EXP):
            pl.when(nonempty(e))(lambda e=e: zero_copy(e).wait())

        def tail_copy(j):
            return pltpu.make_async_copy(zeros, xs_ref.at[pl.ds(pl.multiple_of(j * BM, BM), BM)],
                                         zsem)

        first, last = pend_ref[N_EXP - 1] // BM, xs_ref.shape[0] // BM
        lax.fori_loop(first, last, lambda j, c: (tail_copy(j).start(), c)[1], 0)
        lax.fori_loop(first, last, lambda j, c: (tail_copy(j).wait(), c)[1], 0)

    te = te_ref[...]
    ri = lax.broadcasted_iota(I32, (TM, TM), 0)
    ci = lax.broadcasted_iota(I32, (TM, TM), 1)
    rank = _dot((ri > ci).astype(BF16), _multi_hot(te).astype(BF16)) + gsrc_ref[0:1, :]
    lane = lax.broadcasted_iota(I32, (TM, LANES), 1)
    pos = jnp.zeros((TM, LANES), F32)
    for k in range(TOP_K):
        p = jnp.sum(jnp.where(lane == te[:, k:k + 1], rank, 0.0), axis=-1, keepdims=True)
        pos = jnp.where(lane == k, p, pos)
    pos_ref[...] = pos.astype(I32)
    pos_t = pos.T.astype(I32)
    row = lax.broadcasted_iota(I32, (SORT_ROWS, TM), 0)
    onehot = jnp.zeros((SORT_ROWS, TM), F32)
    for k in range(TOP_K):
        onehot = jnp.where(row == pos_t[k:k + 1, :], 1.0, onehot)
    i = pl.program_id(0)
    slot = lax.rem(i, 2)
    z_ref[slot] = _dot(onehot.astype(BF16), h2_ref[...].astype(BF16))

    _for_each_run_piece(tab_ref, lambda s, d, p: pltpu.make_async_copy(
        z_ref.at[slot, pl.ds(s, p)], xs_ref.at[pl.ds(d, p)], sem.at[slot]).start())

    def wait_tile(table_ref, which):
        _wait_rows(table_ref[0, 0, TAB_ROWS], lambda n: pltpu.make_async_copy(
            z_ref.at[which, pl.ds(0, n)], xs_ref.at[pl.ds(0, n)], sem.at[which]).wait())

    pl.when(i > 0)(lambda: wait_tile(tprev_ref, 1 - slot))
    pl.when(i == pl.num_programs(0) - 1)(lambda: wait_tile(tab_ref, slot))


def _dispatch_call(pend, tab, te, gsrc, h2, cap):
    n_tiles = tab.shape[0]
    row = lambda i, pe: (i, 0)
    return pl.pallas_call(
        _dispatch_kernel,
        grid_spec=pltpu.PrefetchScalarGridSpec(
            num_scalar_prefetch=1,
            grid=(n_tiles,),
            in_specs=[pl.BlockSpec((1, 1, TAB_LEN), lambda i, pe: (i, 0, 0), memory_space=pltpu.SMEM),
                      pl.BlockSpec((1, 1, TAB_LEN), lambda i, pe: (jnp.maximum(i - 1, 0), 0, 0),
                                   memory_space=pltpu.SMEM),
                      pl.BlockSpec((TM, LANES), row), pl.BlockSpec((8, LANES), row),
                      pl.BlockSpec((TM, D), row)],
            out_specs=[pl.BlockSpec(memory_space=pl.ANY), pl.BlockSpec((TM, LANES), row)],
            scratch_shapes=[pltpu.VMEM((2, SORT_ROWS, D), F32), pltpu.VMEM((BM, D), F32),
                            pltpu.SemaphoreType.DMA((2,)), pltpu.SemaphoreType.DMA]),
        out_shape=[jax.ShapeDtypeStruct((cap, D), F32),
                   jax.ShapeDtypeStruct((n_tiles * TM, LANES), I32)],
        compiler_params=_cparams(40),
        name="dispatch",
    )(pend, tab, tab, te, gsrc, h2)


def _expert_kernel(be_ref, nv_ref, seg_ref, nxt_ref, fill_ref, x_ref, wgu_hbm, bgu_ref, wd_hbm, bd_ref,
                   o_ref, wgu_f, wd_f, wgu_s, wd_s, sem):
    j = pl.program_id(0)
    jc = jnp.minimum(j, nv_ref[0] - 1)
    e = be_ref[jc]
    slot = lax.rem(seg_ref[jc], 2)
    first = (j == 0) | ((j < nv_ref[0]) & (e != be_ref[jnp.maximum(jc - 1, 0)]))

    def fetch(expert, into):
        return (pltpu.make_async_copy(wgu_hbm.at[expert], wgu_f.at[into], sem.at[0, into]),
                pltpu.make_async_copy(wd_hbm.at[expert], wd_f.at[into], sem.at[1, into]))

    @pl.when(j == 0)
    def _():
        for c in fetch(e, slot):
            c.start()

    @pl.when(first)
    def _():
        for c in fetch(e, slot):
            c.wait()

        @pl.when(nxt_ref[jc] >= 0)
        def _():
            for c in fetch(nxt_ref[jc], 1 - slot):
                c.start()

        wgu_s[...] = wgu_f[slot].astype(BF16)
        wd_s[...] = wd_f[slot].astype(BF16)

    def mlp(groups):
        gus = [_dot(x_ref[g, :].astype(BF16), wgu_s[...]) + bgu_ref[0] for g in groups]
        acts = []
        for gu in gus:
            gate = jnp.minimum(gu[:, :D], SWIGLU_LIMIT)
            up = jnp.clip(gu[:, D:], -SWIGLU_LIMIT, SWIGLU_LIMIT)
            acts.append(((up + 1.0) * (gate * _sigmoid(SWIGLU_ALPHA * gate))).astype(BF16))
        for g, act in zip(groups, acts):
            o_ref[g, :] = _dot(act, wd_s[...]) + bd_ref[0]

    lower, upper = slice(0, BM // 2), slice(BM // 2, BM)
    needs_upper = fill_ref[jc] > BM // 2

    @pl.when((j < nv_ref[0]) & needs_upper)
    def _():
        mlp([lower, upper])

    @pl.when((j < nv_ref[0]) & jnp.logical_not(needs_upper))
    def _():
        mlp([lower])
        o_ref[upper, :] = jnp.zeros((BM // 2, D), F32)

    @pl.when(j >= nv_ref[0])
    def _():
        o_ref[...] = jnp.zeros_like(o_ref)


def _expert_call(block_e, n_valid, seg, nxt, fill, xs, wgu, bgu, wd, bd):
    n_blocks = xs.shape[0] // BM
    blk = lambda j, be, nv, sg, nx, fl: (jnp.minimum(j, nv[0] - 1), 0)
    blk_out = lambda j, be, nv, sg, nx, fl: (j, 0)
    exp = lambda j, be, nv, sg, nx, fl: (be[jnp.minimum(j, nv[0] - 1)], 0, 0)
    return pl.pallas_call(
        _expert_kernel,
        grid_spec=pltpu.PrefetchScalarGridSpec(
            num_scalar_prefetch=5,
            grid=(n_blocks,),
            in_specs=[pl.BlockSpec((BM, D), blk),
                      pl.BlockSpec(memory_space=pl.ANY), pl.BlockSpec((1, 1, 2 * D), exp),
                      pl.BlockSpec(memory_space=pl.ANY), pl.BlockSpec((1, 1, D), exp)],
            out_specs=pl.BlockSpec((BM, D), blk_out),
            scratch_shapes=[pltpu.VMEM((2, D, 2 * D), F32), pltpu.VMEM((2, D, D), F32),
                            pltpu.VMEM((D, 2 * D), BF16), pltpu.VMEM((D, D), BF16),
                            pltpu.SemaphoreType.DMA((2, 2))]),
        out_shape=jax.ShapeDtypeStruct(xs.shape, F32),
        compiler_params=_cparams(56),
        name="expert",
    )(block_e, n_valid, seg, nxt, fill, xs, wgu, bgu, wd, bd)


def _combine_kernel(geom, tcur_ref, tnxt_ref, eo_ref, pos_ref, x1_ref, tw_ref, mhi_ref, mlo_ref,
                    gfin_ref, yp_ref, ys_ref, buf, sem):
    i = pl.program_id(0)
    n_prompt_tiles = geom[0] // TM
    slot = lax.rem(i, 2)

    def gather(tab_ref, into):
        _for_each_run_piece(tab_ref, lambda s, d, p: pltpu.make_async_copy(
            eo_ref.at[pl.ds(d, p)], buf.at[into, pl.ds(s, p)], sem.at[into]).start())

    @pl.when(i == 0)
    def _():
        buf[...] = jnp.zeros_like(buf)
        gather(tcur_ref, 0)

    @pl.when(i + 1 < pl.num_programs(0))
    def _():
        gather(tnxt_ref, 1 - slot)

    _wait_rows(tcur_ref[0, 0, TAB_ROWS], lambda n: pltpu.make_async_copy(
        eo_ref.at[pl.ds(0, n)], buf.at[slot, pl.ds(0, n)], sem.at[slot]).wait())
    gt2 = _row_mod(i, geom, mhi_ref, mlo_ref)
    tw = tw_ref[...]
    pos = pos_ref[...]
    col = lax.broadcasted_iota(I32, (TM, SORT_ROWS), 1)
    wsel = jnp.zeros((TM, SORT_ROWS), F32)
    for k in range(TOP_K):
        wsel = jnp.where(col == pos[:, k:k + 1], tw[:, k:k + 1], wsel)
    y = _dot(wsel.astype(BF16), buf[slot].astype(BF16))
    x2 = x1_ref[...] + gt2 * y
    ms = jnp.mean(x2 * x2, axis=-1, keepdims=True)
    out = x2 * lax.rsqrt(ms + EPS) * gfin_ref[...]

    @pl.when(i < n_prompt_tiles)
    def _():
        yp_ref[...] = out

    @pl.when(i >= n_prompt_tiles)
    def _():
        ys_ref[...] = out


def _combine_call(geom, n_rows, tab, eo, pos, x1, tw, mhi, mlo, g_final):
    n_tiles = n_rows // TM
    n_prompt = geom[0]
    last_p = n_prompt // TM - 1
    const = lambda i: (0, 0)
    row = lambda i: (i, 0)
    return pl.pallas_call(
        functools.partial(_combine_kernel, geom),
        grid=(n_tiles,),
        in_specs=[pl.BlockSpec((1, 1, TAB_LEN), lambda i: (i, 0, 0), memory_space=pltpu.SMEM),
                  pl.BlockSpec((1, 1, TAB_LEN), lambda i: (jnp.minimum(i + 1, n_tiles - 1), 0, 0),
                               memory_space=pltpu.SMEM),
                  pl.BlockSpec(memory_space=pl.ANY), pl.BlockSpec((TM, LANES), row),
                  pl.BlockSpec((TM, D), row), pl.BlockSpec((TM, LANES), row),
                  pl.BlockSpec(mhi.shape, const), pl.BlockSpec(mlo.shape, const),
                  pl.BlockSpec((1, D), const)],
        out_specs=[pl.BlockSpec((TM, D), lambda i: (jnp.minimum(i, last_p), 0)),
                   pl.BlockSpec((n_rows - n_prompt, D), const)],
        out_shape=[jax.ShapeDtypeStruct((n_prompt, D), F32),
                   jax.ShapeDtypeStruct((n_rows - n_prompt, D), F32)],
        scratch_shapes=[pltpu.VMEM((2, SORT_ROWS, D), F32), pltpu.SemaphoreType.DMA((2,))],
        compiler_params=_cparams(48),
        name="combine",
    )(tab, tab, eo, pos, x1, tw, mhi, mlo, g_final)


def kernel(x_prompt, x_sample, c_prompt, c_sample, state_gla, state_hgrn, w_ada, b_ada, g_norm_mix,
           g_norm_ffn, w_in, w_gla_gate2, b_gla_gate, g_gla_onorm, hgrn_lb_logits, g_hgrn_onorm,
           w_branch_a, w_branch_b, w_out, w_router, b_router, w_gate_up, b_gate_up, w_down, b_down,
           g_final):
    assert w_ada.shape[0] == 1, "single-layer trunk only"
    bp, lp, _ = x_prompt.shape
    bs, ls, _ = x_sample.shape
    n_p, n_s = bp * lp, bs * ls
    n = n_p + n_s
    assert n_p % TM == 0 and n_s == TM and lp % TM == 0 and lp % REC_ROWS == 0 and ls <= CHUNK
    assert bp + bs <= 32 and REC_ROWS == TM and TM // SUB == 8
    assert SORT_ROWS >= TM * TOP_K + N_EXP * (RUN - 1) and RUN_PIECES[0] == TM
    assert n % ROUTE_ROWS == 0 and ROUTE_ROWS % TM == 0
    geom = (n_p, lp, ls, bp)
    xp = x_prompt.reshape(n_p, D)
    xs = x_sample.reshape(n_s, D)

    c_all = jnp.zeros((32, D), F32).at[:bp].set(c_prompt).at[bp:bp + bs].set(c_sample)
    mod = _mod_call(c_all, w_ada[0], b_ada[0].reshape(1, 6 * D))
    sh1, sc1, gt1, sh2, sc2, gt2 = [mod[:, j * D:(j + 1) * D] for j in range(6)]
    def table_and_sample_rows(t):
        return t, jnp.repeat(t[bp:bp + bs], ls, axis=0)

    m1_hi, m1_lo = table_and_sample_rows(jnp.concatenate([sh1, sc1], axis=1))
    m2_hi, m2_lo = table_and_sample_rows(jnp.concatenate([gt1, sh2, sc2], axis=1))
    m3_hi, m3_lo = table_and_sample_rows(gt2)

    wi = w_in[0]
    wa = wi[:, :C_QB].astype(BF16)
    wb = wi[:, C_QB + GATE_RANK:].astype(BF16)
    wlr = jnp.pad(wi[:, 3072:3072 + GATE_RANK], ((0, 0), (0, LANES - GATE_RANK))).astype(BF16)
    w2 = jnp.pad(w_gla_gate2[0], ((0, LANES - GATE_RANK), (0, 0))).astype(BF16)
    arrs = _proj_call(geom, n, xp, xs, m1_hi, m1_lo, g_norm_mix[0].reshape(1, D), hgrn_lb_logits,
                      wa, wb, wlr, w2, b_gla_gate[0].reshape(1, -1))
    qa, ka, va, ra, ga, qb, kb, gb, ib, rb, ua, ub, worst_decay = arrs
    frame_path = (worst_decay[::8, 0] > EXP_CLAMP - 1.0).astype(I32)
    rec_in = (qa, ka, va, ra, ga, qb, kb, gb, ib, rb)

    gna = g_gla_onorm[0].reshape(1, GLA_DV)
    gnb = g_hgrn_onorm[0].reshape(1, HG_DV)
    oap, obp, sa_p, sb_p = _rec_call(frame_path[:n_p // TM], rec_in, gna, gnb, bp, lp, REC_ROWS, 0)
    oas, obs, sa_s, sb_s = _rec_call(jnp.broadcast_to(frame_path[n_p // TM], (bs,)), rec_in, gna,
                                     gnb, bs, ls, ls, n_p, s0=(state_gla[0], state_hgrn[0]))

    wr = jnp.pad(w_router[0], ((0, 0), (0, LANES - N_EXP)))
    wr_hi, wr_lo = _split_bf16(wr)
    br = jnp.pad(b_router[0], (0, LANES - N_EXP), constant_values=NEG).reshape(1, LANES)
    x1, h2, logits = _post_call(geom, n, xp, xs, oap, oas, obp, obs, ua, ub, m2_hi, m2_lo,
                                g_norm_ffn[0].reshape(1, D), w_branch_a[0].astype(BF16),
                                w_branch_b[0].astype(BF16), w_out[0].astype(BF16), wr_hi, wr_lo, br)

    n_tiles = n // TM
    te, tw, cnt = _route_call(logits, n)
    run = (cnt[::8, :N_EXP].astype(I32) + RUN - 1) // RUN * RUN
    counts = jnp.sum(run, axis=0)
    seg_len = (counts + BM - 1) // BM * BM
    pend = jnp.cumsum(seg_len).astype(I32)
    run_dst = (pend - seg_len)[None, :] + jnp.cumsum(run, axis=0) - run
    run_src = jnp.cumsum(run, axis=1) - run
    tab = _piece_table(run, run_src, run_dst)
    gsrc = jnp.repeat(jnp.pad(run_src.astype(F32), ((0, 0), (0, LANES - N_EXP))), 8, axis=0)
    cap = (n * TOP_K + n_tiles * N_EXP * (RUN - 1) + N_EXP * (BM - 1) + BM - 1) // BM * BM
    block_start = jnp.arange(cap // BM, dtype=I32) * BM
    block_e = jnp.minimum(jnp.sum((pend[None, :] <= block_start[:, None]).astype(I32), axis=1),
                          N_EXP - 1)
    n_valid = pend[-1:] // BM
    ids = jnp.arange(N_EXP, dtype=I32)
    nonempty = counts > 0
    later = jnp.where(nonempty[None, :] & (ids[None, :] > ids[:, None]), ids[None, :], N_EXP)
    next_e = jnp.min(later, axis=1)
    next_e = jnp.where(next_e == N_EXP, -1, next_e)
    is_e = block_e[:, None] == ids[None, :]
    seg = jnp.sum(jnp.where(is_e, (jnp.cumsum(nonempty.astype(I32)) - 1)[None, :], 0), axis=1)
    nxt = jnp.sum(jnp.where(is_e, next_e[None, :], 0), axis=1)
    fill = jnp.clip(jnp.sum(jnp.where(is_e, (pend - seg_len + counts)[None, :], 0), axis=1)
                    - block_start, 0, BM)
    xs_sorted, pos = _dispatch_call(pend, tab, te, gsrc, h2, cap)
    eo = _expert_call(block_e, n_valid, seg, nxt, fill, xs_sorted, w_gate_up[0],
                      b_gate_up[0].reshape(N_EXP, 1, -1), w_down[0], b_down[0].reshape(N_EXP, 1, -1))

    yp, ys = _combine_call(geom, n, tab, eo, pos, x1, tw, m3_hi, m3_lo, g_final.reshape(1, D))
    return (yp.reshape(bp, lp, D), ys.reshape(bs, ls, D),
            sa_p[None], sb_p[None], sa_s[None], sb_s[None])
```

```python
import functools

import jax
import jax.numpy as jnp
from jax import lax
from jax.experimental import pallas as pl
from jax.experimental.pallas import tpu as pltpu

F32 = jnp.float32
BF16 = jnp.bfloat16
I32 = jnp.int32

EPS = 1e-6
D = 1024
GLA_H, GLA_DK, GLA_DV = 4, 128, 256
HG_H, HG_DK, HG_DV = 8, 128, 128
GATE_RANK = 16
GATE_NORM = 16.0
N_EXP = 32
TOP_K = 4
SWIGLU_LIMIT = 7.0
SWIGLU_ALPHA = 1.702

LANES = 128
TM = 256
CHUNK = 128
SUB = 32
NSUB = CHUNK // SUB
REC_ROWS = 2 * CHUNK
POST_ROWS = 128
EXP_CLAMP = 80.0
BM = 512
ROUTE_ROWS = 1280
RUN = 8
RUN_PIECES = (256, 128, 64, 32, 16, 8)
WAIT_ROWS = 128
SORT_ROWS = TM * TOP_K + 256
TAB_COUNT = 2 * N_EXP * len(RUN_PIECES)
TAB_ROWS = TAB_COUNT + len(RUN_PIECES)
TAB_LEN = 512
NEG = -1e30

C_QA, C_KA, C_VA, C_RA, C_QB, C_FB, C_IB, C_RB, C_UA, C_UB, C_END = (
    0, 512, 1024, 2048, 3072, 4096, 5120, 6144, 7168, 8192, 9216)


def _sigmoid(x):
    return 1.0 / (1.0 + jnp.exp(-x))


def _dot(a, b):
    return jnp.dot(a, b, preferred_element_type=F32)


def _split_bf16(x):
    hi = x.astype(BF16)
    lo = (x - hi.astype(F32)).astype(BF16)
    return hi, lo


def _log2(n):
    assert n > 0 and n & (n - 1) == 0, "power of two expected"
    return n.bit_length() - 1


def _cparams(vmem_mb):
    return pltpu.CompilerParams(dimension_semantics=("arbitrary",),
                                vmem_limit_bytes=vmem_mb * 1024 * 1024)


def _mod_kernel(c_ref, w_ref, b_ref, o_ref):
    c = c_ref[...]
    s = (c * _sigmoid(c)).astype(BF16)
    o_ref[...] = _dot(s, w_ref[...].astype(BF16)) + b_ref[...]


def _mod_call(c_all, w_ada, b_ada):
    rows = c_all.shape[0]
    return pl.pallas_call(
        _mod_kernel,
        grid=(6,),
        in_specs=[pl.BlockSpec((rows, D), lambda j: (0, 0)),
                  pl.BlockSpec((D, D), lambda j: (0, j)),
                  pl.BlockSpec((1, D), lambda j: (0, j))],
        out_specs=pl.BlockSpec((rows, D), lambda j: (0, j)),
        out_shape=jax.ShapeDtypeStruct((rows, 6 * D), F32),
        compiler_params=_cparams(32),
        name="mod",
    )(c_all, w_ada, b_ada)


def _row_mod(i, geom, table_ref, sample_ref):
    n_prompt_rows, prompt_len, _, n_prompt_seq = geom
    seq = jnp.minimum((i * TM) >> _log2(prompt_len), n_prompt_seq - 1)
    return jnp.where(i < n_prompt_rows // TM, table_ref[pl.ds(seq, 1), :], sample_ref[...])


def _proj_kernel(geom, xp_ref, xs_ref, mtab_ref, msmp_ref, g_ref, lbl_ref, wa_ref, wb_ref, wlr_ref,
                 w2_ref, bg_ref,
                 qa_ref, ka_ref, va_ref, ra_ref, ga_ref, qb_ref, kb_ref, gb_ref, ib_ref,
                 rb_ref, ua_ref, ub_ref, wd_ref):
    i = pl.program_id(0)
    n_prompt_tiles = geom[0] // TM
    x = jnp.where(i < n_prompt_tiles, xp_ref[...], xs_ref[...])
    mod = _row_mod(i, geom, mtab_ref, msmp_ref)
    sh, sc = mod[:, :D], mod[:, D:]
    ms = jnp.mean(x * x, axis=-1, keepdims=True)
    h = x * lax.rsqrt(ms + EPS) * g_ref[...]
    hb = (h * (1.0 + sc) + sh).astype(BF16)

    def proj(a, b):
        if b <= C_QB:
            return _dot(hb, wa_ref[:, a:b])
        return _dot(hb, wb_ref[:, a - C_QB:b - C_QB])

    cols = (C_QA, C_KA, C_VA, C_RA, C_QB, C_FB, C_IB, C_RB, C_UA, C_UB, C_END)
    z_qa, z_ka, z_va, z_ra, z_qb, z_fb, z_ib, z_rb, z_ua, z_ub = [
        proj(a, b) for a, b in zip(cols[:-1], cols[1:])]
    lr_hi, lr_lo = _split_bf16(_dot(hb, wlr_ref[...]))
    qa_ref[...] = (z_qa * GLA_DK ** -0.5).astype(BF16)
    ka_ref[...] = z_ka.astype(BF16)
    va_ref[...] = z_va.astype(BF16)
    ra_ref[...] = (z_ra * _sigmoid(z_ra)).astype(BF16)
    xg = _dot(lr_hi, w2_ref[...]) + _dot(lr_lo, w2_ref[...]) + bg_ref[...]
    ga = (jnp.minimum(xg, 0.0) - jnp.log1p(jnp.exp(-jnp.abs(xg)))) * (1.0 / GATE_NORM)
    ga_ref[...] = ga
    qb_ref[...] = (z_qb * _sigmoid(z_qb) * HG_DK ** -0.5).astype(BF16)
    lbl = lbl_ref[...]
    e = jnp.exp(lbl - jnp.max(lbl, axis=0, keepdims=True))
    lb = e[0:1, :] / jnp.sum(e, axis=0, keepdims=True)
    kb_ref[...] = ((1.0 - lb) * _sigmoid(-z_fb)).astype(BF16)
    gb = jnp.log(lb + (1.0 - lb) * _sigmoid(z_fb))
    gb_ref[...] = gb
    worst = jnp.zeros((1, 1), F32)
    for g in (ga, gb):
        for j in range(TM // SUB):
            total = jnp.sum(g[j * SUB:(j + 1) * SUB, :], axis=0, keepdims=True)
            worst = jnp.maximum(worst, jnp.max(-total, axis=1, keepdims=True))
    wd_ref[...] = jnp.broadcast_to(worst, (8, LANES))
    ib_ref[...] = z_ib.astype(BF16)
    rb_ref[...] = (z_rb * _sigmoid(z_rb)).astype(BF16)
    ua_ref[...] = _sigmoid(z_ua).astype(BF16)
    ub_ref[...] = _sigmoid(z_ub).astype(BF16)


def _proj_call(geom, n_rows, xp, xs, mtab, msmp, g_mix, lbl, wa, wb, wlr, w2, bg):
    n_tiles = n_rows // TM
    last_p = geom[0] // TM - 1
    const = lambda i: (0, 0)
    row = lambda i: (i, 0)
    widths = [(512, BF16), (512, BF16), (D, BF16), (D, BF16), (512, F32), (D, BF16), (D, BF16),
              (D, F32), (D, BF16), (D, BF16), (D, BF16), (D, BF16)]
    return pl.pallas_call(
        functools.partial(_proj_kernel, geom),
        grid=(n_tiles,),
        in_specs=[pl.BlockSpec((TM, D), lambda i: (jnp.minimum(i, last_p), 0)),
                  pl.BlockSpec(xs.shape, const),
                  pl.BlockSpec(mtab.shape, const), pl.BlockSpec(msmp.shape, const),
                  pl.BlockSpec((1, D), const), pl.BlockSpec(lbl.shape, const),
                  pl.BlockSpec(wa.shape, const, pipeline_mode=pl.Buffered(1)),
                  pl.BlockSpec(wb.shape, const, pipeline_mode=pl.Buffered(1)),
                  pl.BlockSpec(wlr.shape, const), pl.BlockSpec(w2.shape, const),
                  pl.BlockSpec(bg.shape, const)],
        out_specs=[pl.BlockSpec((TM, w), row) for w, _ in widths] + [pl.BlockSpec((8, LANES), row)],
        out_shape=([jax.ShapeDtypeStruct((n_rows, w), dt) for w, dt in widths]
                   + [jax.ShapeDtypeStruct((n_tiles * 8, LANES), F32)]),
        compiler_params=_cparams(56),
        name="proj",
    )(xp, xs, mtab, msmp, g_mix, lbl, wa, wb, wlr, w2, bg)


def _rec_prep(tri, g, q, k):
    g_hi, g_lo = _split_bf16(g)
    cum = _dot(tri, g_hi) + _dot(tri, g_lo)
    width = cum.shape[1]
    tot = cum[CHUNK - 1:CHUNK, :]
    refs = [jnp.zeros((1, width), F32)] + [cum[j * SUB - 1:j * SUB, :] for j in range(1, NSUB)]
    d = cum - jnp.concatenate([jnp.broadcast_to(b, (SUB, width)) for b in refs], axis=0)
    qn = q.astype(F32) * jnp.exp(d)
    kn = k.astype(F32) * jnp.exp(jnp.minimum(-d, EXP_CLAMP))
    sub = lambda x, j: x[j * SUB:(j + 1) * SUB, :]
    qs = jnp.concatenate([sub(qn, j) * jnp.exp(refs[j]) for j in range(NSUB)],
                         axis=0).astype(BF16)
    kd = jnp.concatenate([sub(kn, j) * jnp.exp(tot - refs[j]) for j in range(NSUB)], axis=0)
    et = jnp.exp(tot)
    zero_rows = lambda n: jnp.zeros((n, width), F32)
    q_to, k_of = [], []
    for j in range(NSUB):
        parts = [zero_rows(j * SUB)] if j else []
        parts += [sub(qn, i) * jnp.exp(refs[i] - refs[j]) if i > j else sub(qn, i)
                  for i in range(j, NSUB)]
        q_to.append(jnp.concatenate(parts, axis=0).astype(BF16))
        parts = ([zero_rows(j * SUB)] if j else []) + [sub(kn, j)]
        if j + 1 < NSUB:
            parts.append(zero_rows(CHUNK - (j + 1) * SUB))
        k_of.append(jnp.concatenate(parts, axis=0).astype(BF16))
    return qs, kd, et, q_to, k_of


def _rec_kernel(rows, has_s0, fp_ref, *refs):
    (qa_ref, ka_ref, va_ref, ra_ref, ga_ref, qb_ref, kb_ref, gb_ref, ib_ref, rb_ref,
     gna_ref, gnb_ref) = refs[:12]
    refs = refs[12:]
    if has_s0:
        s0a_ref, s0b_ref = refs[:2]
        refs = refs[2:]
    oga_ref, ogb_ref, sa_ref, sb_ref, qf, kf, vf, of = refs

    @pl.when(pl.program_id(1) == 0)
    def _():
        if has_s0:
            sa_ref[...] = s0a_ref[...]
            sb_ref[...] = s0b_ref[...]
        else:
            sa_ref[...] = jnp.zeros_like(sa_ref)
            sb_ref[...] = jnp.zeros_like(sb_ref)

    ti = lax.broadcasted_iota(I32, (CHUNK, CHUNK), 0)
    si = lax.broadcasted_iota(I32, (CHUNK, CHUNK), 1)
    causal = ti >= si
    tri = causal.astype(BF16)
    n_chunks = max(rows // CHUNK, 1)
    rows_out = min(rows, CHUNK)

    def load(ref, c):
        if rows >= CHUNK:
            return ref[c * CHUNK:(c + 1) * CHUNK, :]
        x = ref[...]
        return jnp.concatenate([x, jnp.zeros((CHUNK - rows, x.shape[1]), x.dtype)], axis=0)

    mixers = ((GLA_H, GLA_DK, GLA_DV, ga_ref, qa_ref, ka_ref, va_ref, ra_ref, sa_ref, gna_ref, oga_ref),
              (HG_H, HG_DK, HG_DV, gb_ref, qb_ref, kb_ref, ib_ref, rb_ref, sb_ref, gnb_ref, ogb_ref))
    use_frames = fp_ref[pl.program_id(0) * pl.num_programs(1) + pl.program_id(1)] != 0
    pl.when(jnp.logical_not(use_frames))(
        lambda: _rec_factored(tri, causal, load, mixers, n_chunks, rows_out))
    pl.when(use_frames)(lambda: _rec_frames(rows, mixers, qf, kf, vf, of))


def _rec_factored(tri, causal, load, mixers, n_chunks, rows_out):
    units = []
    for c in range(n_chunks):
        for (nh, dk, dv, g_ref, q_ref, k_ref, v_ref, r_ref, s_ref, gn_ref, o_ref) in mixers:
            qs, kd, et, q_to, k_of = _rec_prep(tri, load(g_ref, c), load(q_ref, c), load(k_ref, c))
            for h in range(nh):
                ks = slice(h * dk, (h + 1) * dk)
                units.append(dict(
                    c=c, h=h, dk=dk, dv=dv, vs=slice(h * dv, (h + 1) * dv), s_ref=s_ref,
                    gn_ref=gn_ref, o_ref=o_ref, v_ref=v_ref, r_ref=r_ref,
                    qs=qs[:, ks], kd=kd[:, ks], et=et[:, ks],
                    q_to=jnp.concatenate([x[:, ks] for x in q_to], axis=1),
                    k_of=jnp.concatenate([x[:, ks] for x in k_of], axis=1)))
    for u in units:
        u["a"] = lax.dot_general(u["q_to"], u["k_of"], (((1,), (1,)), ((), ())),
                                 preferred_element_type=F32)
    for u in units:
        u["lhs"] = jnp.concatenate([jnp.where(causal, u["a"], 0.0).astype(BF16),
                                    u["kd"].T.astype(BF16)], axis=0)
    for u in units:
        u["av"] = _dot(u["lhs"], load(u["v_ref"], u["c"])[:, u["vs"]])
    for u in units:
        dk, dv, h = u["dk"], u["dv"], u["h"]
        s = u["s_ref"][0, h]
        o = _dot(u["qs"], s.astype(BF16)) + u["av"][:CHUNK]
        et_col = jnp.broadcast_to(u["et"], (dk, dk)).T
        u["s_ref"][0, h] = s * jnp.tile(et_col, (1, dv // dk)) + u["av"][CHUNK:]
        ms = jnp.mean(o * o, axis=-1, keepdims=True)
        og = (o * lax.rsqrt(ms + EPS) * u["gn_ref"][...]
              * load(u["r_ref"], u["c"])[:, u["vs"]].astype(F32))
        r0 = u["c"] * CHUNK
        u["o_ref"][r0:r0 + rows_out, u["vs"]] = og[:rows_out].astype(BF16)


def _rec_frames(rows, mixers, qf, kf, vf, of):
    sublane = lax.broadcasted_iota(I32, (8, 1), 0)
    of[...] = jnp.zeros_like(of)
    for (nh, dk, dv, g_ref, q_ref, k_ref, v_ref, r_ref, s_ref, gn_ref, o_ref) in mixers:
        qf[:, :nh * dk] = q_ref[...].astype(F32)
        kf[:, :nh * dk] = k_ref[...].astype(F32)
        vf[:, :nh * dv] = v_ref[...].astype(F32)
        for h in range(nh):
            ks = slice(h * dk, (h + 1) * dk)
            vs = slice(h * dv, (h + 1) * dv)

            def frame(t, st, ks=ks, vs=vs, g_ref=g_ref):
                rows8 = pl.ds(pl.multiple_of((t >> 3) << 3, 8), 8)
                is_t = sublane == (t & 7)
                at = lambda ref, cols: jnp.where(is_t, ref[rows8, cols], 0.0)
                kv = lax.dot_general(at(vf, vs).astype(BF16), at(kf, ks).astype(BF16),
                                     (((0,), (0,)), ((), ())),
                                     preferred_element_type=F32)
                g_t = jnp.sum(at(g_ref, ks), axis=0, keepdims=True)
                st = st * jnp.exp(g_t) + kv
                o8 = lax.dot_general(at(qf, ks).astype(BF16), st.astype(BF16),
                                     (((1,), (1,)), ((), ())),
                                     preferred_element_type=F32)
                of[rows8, vs] = jnp.where(is_t, o8, of[rows8, vs])
                return st

            s_ref[0, h] = lax.fori_loop(0, rows, frame, s_ref[0, h].T).T
            o = of[:, vs]
            ms = jnp.mean(o * o, axis=-1, keepdims=True)
            og = o * lax.rsqrt(ms + EPS) * gn_ref[...] * r_ref[:, vs].astype(F32)
            o_ref[:, vs] = og.astype(BF16)


def _rec_call(frame_path, arrs, gna, gnb, n_seq, seq_len, rows, row0, s0=None):
    steps = seq_len // rows
    blk0 = row0 // rows
    row_in = lambda b, t, fp: (blk0 + b * steps + t, 0)
    row_out = lambda b, t, fp: (b * steps + t, 0)
    const = lambda b, t, fp: (0, 0)
    st = lambda b, t, fp: (b, 0, 0, 0)
    in_specs = [pl.BlockSpec((rows, a.shape[1]), row_in) for a in arrs]
    in_specs += [pl.BlockSpec(gna.shape, const), pl.BlockSpec(gnb.shape, const)]
    args = list(arrs) + [gna, gnb]
    if s0 is not None:
        in_specs += [pl.BlockSpec((1, GLA_H, GLA_DK, GLA_DV), st),
                     pl.BlockSpec((1, HG_H, HG_DK, HG_DV), st)]
        args += [s0[0], s0[1]]
    n_rows = n_seq * seq_len
    return pl.pallas_call(
        functools.partial(_rec_kernel, rows, s0 is not None),
        grid_spec=pltpu.PrefetchScalarGridSpec(
            num_scalar_prefetch=1,
            grid=(n_seq, steps),
            in_specs=in_specs,
            out_specs=[pl.BlockSpec((rows, D), row_out), pl.BlockSpec((rows, D), row_out),
                       pl.BlockSpec((1, GLA_H, GLA_DK, GLA_DV), st),
                       pl.BlockSpec((1, HG_H, HG_DK, HG_DV), st)],
            scratch_shapes=[pltpu.VMEM((rows, D), F32)] * 4),
        out_shape=[jax.ShapeDtypeStruct((n_rows, D), BF16), jax.ShapeDtypeStruct((n_rows, D), BF16),
                   jax.ShapeDtypeStruct((n_seq, GLA_H, GLA_DK, GLA_DV), F32),
                   jax.ShapeDtypeStruct((n_seq, HG_H, HG_DK, HG_DV), F32)],
        compiler_params=pltpu.CompilerParams(dimension_semantics=("arbitrary", "arbitrary"),
                                             vmem_limit_bytes=48 * 1024 * 1024),
        name="rec_s0" if s0 is not None else "rec",
    )(frame_path, *args)


def _post_kernel(geom, xp_ref, xs_ref, oap_ref, oas_ref, obp_ref, obs_ref, ua_ref, ub_ref,
                 mtab_ref, msmp_ref, gf_ref, wba_ref, wbb_ref, wo_ref, wrh_ref, wrl_ref, br_ref,
                 x1_ref, h2_ref, lg_ref):
    i = pl.program_id(0)
    is_prompt = i < geom[0] // TM
    mod = _row_mod(i, geom, mtab_ref, msmp_ref)
    groups = [slice(j * POST_ROWS, (j + 1) * POST_ROWS) for j in range(TM // POST_ROWS)]
    pa = [_dot(jnp.where(is_prompt, oap_ref[g, :], oas_ref[g, :]), wba_ref[...]) for g in groups]
    pb = [_dot(jnp.where(is_prompt, obp_ref[g, :], obs_ref[g, :]), wbb_ref[...]) for g in groups]
    merged = [(ua_ref[g, :].astype(F32) * a + ub_ref[g, :].astype(F32) * b).astype(BF16)
              for g, a, b in zip(groups, pa, pb)]
    y = [_dot(m, wo_ref[...]) for m in merged]
    h2s = []
    for g, yg in zip(groups, y):
        x1 = jnp.where(is_prompt, xp_ref[g, :], xs_ref[g, :]) + mod[g, :D] * yg
        x1_ref[g, :] = x1
        ms = jnp.mean(x1 * x1, axis=-1, keepdims=True)
        h2 = x1 * lax.rsqrt(ms + EPS) * gf_ref[...] * (1.0 + mod[g, 2 * D:]) + mod[g, D:2 * D]
        h2_ref[g, :] = h2
        h2s.append(_split_bf16(h2))
    for g, (h_hi, h_lo) in zip(groups, h2s):
        lg_ref[g, :] = (_dot(h_hi, wrh_ref[...]) + _dot(h_lo, wrh_ref[...])
                        + _dot(h_hi, wrl_ref[...]) + br_ref[...])


def _post_call(geom, n_rows, xp, xs, oap, oas, obp, obs, ua, ub, mtab, msmp, g_ffn, wba, wbb, wo,
               wrh, wrl, br):
    n_tiles = n_rows // TM
    last_p = geom[0] // TM - 1
    const = lambda i: (0, 0)
    row = lambda i: (i, 0)
    prow = pl.BlockSpec((TM, D), lambda i: (jnp.minimum(i, last_p), 0))
    full = lambda a: pl.BlockSpec(a.shape, const)
    return pl.pallas_call(
        functools.partial(_post_kernel, geom),
        grid=(n_tiles,),
        in_specs=[prow, full(xs), prow, full(oas), prow, full(obs),
                  pl.BlockSpec((TM, D), row), pl.BlockSpec((TM, D), row),
                  full(mtab), full(msmp), full(g_ffn), full(wba), full(wbb), full(wo),
                  full(wrh), full(wrl), full(br)],
        out_specs=[pl.BlockSpec((TM, D), row), pl.BlockSpec((TM, D), row),
                   pl.BlockSpec((TM, LANES), row)],
        out_shape=[jax.ShapeDtypeStruct((n_rows, D), F32), jax.ShapeDtypeStruct((n_rows, D), F32),
                   jax.ShapeDtypeStruct((n_rows, LANES), F32)],
        compiler_params=_cparams(48),
        name="post",
    )(xp, xs, oap, oas, obp, obs, ua, ub, mtab, msmp, g_ffn, wba, wbb, wo, wrh, wrl, br)


def _multi_hot(te):
    lane = lax.broadcasted_iota(I32, (TM, LANES), 1)
    m = jnp.zeros((TM, LANES), F32)
    for k in range(TOP_K):
        m = m + (lane == te[:, k:k + 1]).astype(F32)
    return m


def _route_kernel(lg_ref, te_ref, tw_ref, cnt_ref):
    logit = lg_ref[...]
    lane = lax.broadcasted_iota(I32, (ROUTE_ROWS, LANES), 1)
    lane_f = lane.astype(F32)
    vals, idxs = [], []
    for _ in range(TOP_K):
        m = jnp.max(logit, axis=-1, keepdims=True)
        idx = jnp.min(jnp.where(logit == m, lane_f, float(LANES)), axis=-1, keepdims=True)
        vals.append(m)
        idxs.append(idx)
        logit = jnp.where(lane_f == idx, -jnp.inf, logit)
    es = [jnp.exp(v - vals[0]) for v in vals]
    den = es[0] + es[1] + es[2] + es[3]
    te = jnp.zeros((ROUTE_ROWS, LANES), F32)
    tw = jnp.zeros((ROUTE_ROWS, LANES), F32)
    for k in range(TOP_K):
        te = jnp.where(lane == k, idxs[k], te)
        tw = jnp.where(lane == k, es[k] / den, tw)
    te = te.astype(I32)
    te_ref[...] = te
    tw_ref[...] = tw
    for t in range(ROUTE_ROWS // TM):
        cnt_ref[t * 8:(t + 1) * 8, :] = jnp.broadcast_to(
            jnp.sum(_multi_hot(te[t * TM:(t + 1) * TM]), axis=0, keepdims=True), (8, LANES))


def _route_call(logits, n_rows):
    n_steps = n_rows // ROUTE_ROWS
    row = lambda i: (i, 0)
    tiles = ROUTE_ROWS // TM
    return pl.pallas_call(
        _route_kernel,
        grid=(n_steps,),
        in_specs=[pl.BlockSpec((ROUTE_ROWS, LANES), row)],
        out_specs=[pl.BlockSpec((ROUTE_ROWS, LANES), row), pl.BlockSpec((ROUTE_ROWS, LANES), row),
                   pl.BlockSpec((tiles * 8, LANES), row)],
        out_shape=[jax.ShapeDtypeStruct((n_rows, LANES), I32),
                   jax.ShapeDtypeStruct((n_rows, LANES), F32),
                   jax.ShapeDtypeStruct((n_rows // TM * 8, LANES), F32)],
        compiler_params=_cparams(32),
        name="route",
    )(logits)


def _wait_rows(rows, wait_n_rows):
    lax.fori_loop(0, rows >> _log2(WAIT_ROWS), lambda j, c: (wait_n_rows(WAIT_ROWS), c)[1], 0)
    lax.fori_loop(0, (rows & (WAIT_ROWS - 1)) >> _log2(RUN),
                  lambda j, c: (wait_n_rows(RUN), c)[1], 0)


def _for_each_run_piece(tab_ref, fn):
    for s, p in enumerate(RUN_PIECES):
        def body(j, c, s=s, p=p):
            fn(pl.multiple_of(tab_ref[0, 0, s * 2 * N_EXP + j], RUN),
               pl.multiple_of(tab_ref[0, 0, (s * 2 + 1) * N_EXP + j], RUN), p)
            return c

        lax.fori_loop(0, tab_ref[0, 0, TAB_COUNT + s], body, 0)


def _piece_table(run, run_src, run_dst):
    ids = jnp.arange(N_EXP, dtype=I32)
    earlier = (ids[None, :] < ids[:, None]).astype(I32)
    src_l, dst_l, cnt_l = [], [], []
    for p in RUN_PIECES:
        has = (run & p) != 0
        off = run - (run & (2 * p - 1))
        slot = jnp.sum(has[:, None, :] * earlier[None, :, :], axis=2)
        put = has[:, :, None] & (slot[:, :, None] == ids[None, None, :])
        src_l.append(jnp.sum(jnp.where(put, (run_src + off)[:, :, None], 0), axis=1))
        dst_l.append(jnp.sum(jnp.where(put, (run_dst + off)[:, :, None], 0), axis=1))
        cnt_l.append(jnp.sum(has.astype(I32), axis=1))
    cols = [a for pair in zip(src_l, dst_l) for a in pair]
    tab = jnp.concatenate(cols + [jnp.stack(cnt_l, axis=1), jnp.sum(run, axis=1, keepdims=True)],
                          axis=1)
    tab = jnp.pad(tab, ((0, 0), (0, TAB_LEN - tab.shape[1])))
    return tab.reshape(run.shape[0], 1, TAB_LEN)


def _dispatch_kernel(pend_ref, tab_ref, tprev_ref, te_ref, gsrc_ref, h2_ref, xs_ref, pos_ref, z_ref,
                     zeros, sem, zsem):
    @pl.when(pl.program_id(0) == 0)
    def _():
        zeros[...] = jnp.zeros_like(zeros)

        def zero_copy(e):
            start = pl.multiple_of(pend_ref[e] - BM, BM)
            return pltpu.make_async_copy(zeros, xs_ref.at[pl.ds(start, BM)], zsem)

        def nonempty(e):
            return pend_ref[e] > (pend_ref[e - 1] if e > 0 else 0)

        for e in range(N_EXP):
            pl.when(nonempty(e))(lambda e=e: zero_copy(e).start())
        for e in range(N_EXP):
            pl.when(nonempty(e))(lambda e=e: zero_copy(e).wait())

        def tail_copy(j):
            return pltpu.make_async_copy(zeros, xs_ref.at[pl.ds(pl.multiple_of(j * BM, BM), BM)],
                                         zsem)

        first, last = pend_ref[N_EXP - 1] // BM, xs_ref.shape[0] // BM
        lax.fori_loop(first, last, lambda j, c: (tail_copy(j).start(), c)[1], 0)
        lax.fori_loop(first, last, lambda j, c: (tail_copy(j).wait(), c)[1], 0)

    te = te_ref[...]
    ri = lax.broadcasted_iota(I32, (TM, TM), 0)
    ci = lax.broadcasted_iota(I32, (TM, TM), 1)
    rank = _dot((ri > ci).astype(BF16), _multi_hot(te).astype(BF16)) + gsrc_ref[0:1, :]
    lane = lax.broadcasted_iota(I32, (TM, LANES), 1)
    pos = jnp.zeros((TM, LANES), F32)
    for k in range(TOP_K):
        p = jnp.sum(jnp.where(lane == te[:, k:k + 1], rank, 0.0), axis=-1, keepdims=True)
        pos = jnp.where(lane == k, p, pos)
    pos_ref[...] = pos.astype(I32)
    pos_t = pos.T.astype(I32)
    row = lax.broadcasted_iota(I32, (SORT_ROWS, TM), 0)
    onehot = jnp.zeros((SORT_ROWS, TM), F32)
    for k in range(TOP_K):
        onehot = jnp.where(row == pos_t[k:k + 1, :], 1.0, onehot)
    i = pl.program_id(0)
    slot = lax.rem(i, 2)
    z_ref[slot] = _dot(onehot.astype(BF16), h2_ref[...].astype(BF16))

    _for_each_run_piece(tab_ref, lambda s, d, p: pltpu.make_async_copy(
        z_ref.at[slot, pl.ds(s, p)], xs_ref.at[pl.ds(d, p)], sem.at[slot]).start())

    def wait_tile(table_ref, which):
        _wait_rows(table_ref[0, 0, TAB_ROWS], lambda n: pltpu.make_async_copy(
            z_ref.at[which, pl.ds(0, n)], xs_ref.at[pl.ds(0, n)], sem.at[which]).wait())

    pl.when(i > 0)(lambda: wait_tile(tprev_ref, 1 - slot))
    pl.when(i == pl.num_programs(0) - 1)(lambda: wait_tile(tab_ref, slot))


def _dispatch_call(pend, tab, te, gsrc, h2, cap):
    n_tiles = tab.shape[0]
    row = lambda i, pe: (i, 0)
    return pl.pallas_call(
        _dispatch_kernel,
        grid_spec=pltpu.PrefetchScalarGridSpec(
            num_scalar_prefetch=1,
            grid=(n_tiles,),
            in_specs=[pl.BlockSpec((1, 1, TAB_LEN), lambda i, pe: (i, 0, 0), memory_space=pltpu.SMEM),
                      pl.BlockSpec((1, 1, TAB_LEN), lambda i, pe: (jnp.maximum(i - 1, 0), 0, 0),
                                   memory_space=pltpu.SMEM),
                      pl.BlockSpec((TM, LANES), row), pl.BlockSpec((8, LANES), row),
                      pl.BlockSpec((TM, D), row)],
            out_specs=[pl.BlockSpec(memory_space=pl.ANY), pl.BlockSpec((TM, LANES), row)],
            scratch_shapes=[pltpu.VMEM((2, SORT_ROWS, D), F32), pltpu.VMEM((BM, D), F32),
                            pltpu.SemaphoreType.DMA((2,)), pltpu.SemaphoreType.DMA]),
        out_shape=[jax.ShapeDtypeStruct((cap, D), F32),
                   jax.ShapeDtypeStruct((n_tiles * TM, LANES), I32)],
        compiler_params=_cparams(40),
        name="dispatch",
    )(pend, tab, tab, te, gsrc, h2)


def _expert_kernel(be_ref, nv_ref, seg_ref, nxt_ref, fill_ref, x_ref, wgu_hbm, bgu_ref, wd_hbm, bd_ref,
                   o_ref, wgu_f, wd_f, wgu_s, wd_s, sem):
    j = pl.program_id(0)
    jc = jnp.minimum(j, nv_ref[0] - 1)
    e = be_ref[jc]
    slot = lax.rem(seg_ref[jc], 2)
    first = (j == 0) | ((j < nv_ref[0]) & (e != be_ref[jnp.maximum(jc - 1, 0)]))

    def fetch(expert, into):
        return (pltpu.make_async_copy(wgu_hbm.at[expert], wgu_f.at[into], sem.at[0, into]),
                pltpu.make_async_copy(wd_hbm.at[expert], wd_f.at[into], sem.at[1, into]))

    @pl.when(j == 0)
    def _():
        for c in fetch(e, slot):
            c.start()

    @pl.when(first)
    def _():
        for c in fetch(e, slot):
            c.wait()

        @pl.when(nxt_ref[jc] >= 0)
        def _():
            for c in fetch(nxt_ref[jc], 1 - slot):
                c.start()

        wgu_s[...] = wgu_f[slot].astype(BF16)
        wd_s[...] = wd_f[slot].astype(BF16)

    def mlp(groups):
        gus = [_dot(x_ref[g, :].astype(BF16), wgu_s[...]) + bgu_ref[0] for g in groups]
        acts = []
        for gu in gus:
            gate = jnp.minimum(gu[:, :D], SWIGLU_LIMIT)
            up = jnp.clip(gu[:, D:], -SWIGLU_LIMIT, SWIGLU_LIMIT)
            acts.append(((up + 1.0) * (gate * _sigmoid(SWIGLU_ALPHA * gate))).astype(BF16))
        for g, act in zip(groups, acts):
            o_ref[g, :] = _dot(act, wd_s[...]) + bd_ref[0]

    lower, upper = slice(0, BM // 2), slice(BM // 2, BM)
    needs_upper = fill_ref[jc] > BM // 2

    @pl.when((j < nv_ref[0]) & needs_upper)
    def _():
        mlp([lower, upper])

    @pl.when((j < nv_ref[0]) & jnp.logical_not(needs_upper))
    def _():
        mlp([lower])
        o_ref[upper, :] = jnp.zeros((BM // 2, D), F32)

    @pl.when(j >= nv_ref[0])
    def _():
        o_ref[...] = jnp.zeros_like(o_ref)


def _expert_call(block_e, n_valid, seg, nxt, fill, xs, wgu, bgu, wd, bd):
    n_blocks = xs.shape[0] // BM
    blk = lambda j, be, nv, sg, nx, fl: (jnp.minimum(j, nv[0] - 1), 0)
    blk_out = lambda j, be, nv, sg, nx, fl: (j, 0)
    exp = lambda j, be, nv, sg, nx, fl: (be[jnp.minimum(j, nv[0] - 1)], 0, 0)
    return pl.pallas_call(
        _expert_kernel,
        grid_spec=pltpu.PrefetchScalarGridSpec(
            num_scalar_prefetch=5,
            grid=(n_blocks,),
            in_specs=[pl.BlockSpec((BM, D), blk),
                      pl.BlockSpec(memory_space=pl.ANY), pl.BlockSpec((1, 1, 2 * D), exp),
                      pl.BlockSpec(memory_space=pl.ANY), pl.BlockSpec((1, 1, D), exp)],
            out_specs=pl.BlockSpec((BM, D), blk_out),
            scratch_shapes=[pltpu.VMEM((2, D, 2 * D), F32), pltpu.VMEM((2, D, D), F32),
                            pltpu.VMEM((D, 2 * D), BF16), pltpu.VMEM((D, D), BF16),
                            pltpu.SemaphoreType.DMA((2, 2))]),
        out_shape=jax.ShapeDtypeStruct(xs.shape, F32),
        compiler_params=_cparams(56),
        name="expert",
    )(block_e, n_valid, seg, nxt, fill, xs, wgu, bgu, wd, bd)


def _combine_kernel(geom, tcur_ref, tnxt_ref, eo_ref, pos_ref, x1_ref, tw_ref, mtab_ref, msmp_ref,
                    gfin_ref, yp_ref, ys_ref, buf, sem):
    i = pl.program_id(0)
    n_prompt_tiles = geom[0] // TM
    slot = lax.rem(i, 2)

    def gather(tab_ref, into):
        _for_each_run_piece(tab_ref, lambda s, d, p: pltpu.make_async_copy(
            eo_ref.at[pl.ds(d, p)], buf.at[into, pl.ds(s, p)], sem.at[into]).start())

    @pl.when(i == 0)
    def _():
        buf[...] = jnp.zeros_like(buf)
        gather(tcur_ref, 0)

    @pl.when(i + 1 < pl.num_programs(0))
    def _():
        gather(tnxt_ref, 1 - slot)

    _wait_rows(tcur_ref[0, 0, TAB_ROWS], lambda n: pltpu.make_async_copy(
        eo_ref.at[pl.ds(0, n)], buf.at[slot, pl.ds(0, n)], sem.at[slot]).wait())
    gt2 = _row_mod(i, geom, mtab_ref, msmp_ref)
    tw = tw_ref[...]
    pos = pos_ref[...]
    col = lax.broadcasted_iota(I32, (TM, SORT_ROWS), 1)
    wsel = jnp.zeros((TM, SORT_ROWS), F32)
    for k in range(TOP_K):
        wsel = jnp.where(col == pos[:, k:k + 1], tw[:, k:k + 1], wsel)
    y = _dot(wsel.astype(BF16), buf[slot].astype(BF16))
    x2 = x1_ref[...] + gt2 * y
    ms = jnp.mean(x2 * x2, axis=-1, keepdims=True)
    out = x2 * lax.rsqrt(ms + EPS) * gfin_ref[...]

    @pl.when(i < n_prompt_tiles)
    def _():
        yp_ref[...] = out

    @pl.when(i >= n_prompt_tiles)
    def _():
        ys_ref[...] = out


def _combine_call(geom, n_rows, tab, eo, pos, x1, tw, mtab, msmp, g_final):
    n_tiles = n_rows // TM
    n_prompt = geom[0]
    last_p = n_prompt // TM - 1
    const = lambda i: (0, 0)
    row = lambda i: (i, 0)
    return pl.pallas_call(
        functools.partial(_combine_kernel, geom),
        grid=(n_tiles,),
        in_specs=[pl.BlockSpec((1, 1, TAB_LEN), lambda i: (i, 0, 0), memory_space=pltpu.SMEM),
                  pl.BlockSpec((1, 1, TAB_LEN), lambda i: (jnp.minimum(i + 1, n_tiles - 1), 0, 0),
                               memory_space=pltpu.SMEM),
                  pl.BlockSpec(memory_space=pl.ANY), pl.BlockSpec((TM, LANES), row),
                  pl.BlockSpec((TM, D), row), pl.BlockSpec((TM, LANES), row),
                  pl.BlockSpec(mtab.shape, const), pl.BlockSpec(msmp.shape, const),
                  pl.BlockSpec((1, D), const)],
        out_specs=[pl.BlockSpec((TM, D), lambda i: (jnp.minimum(i, last_p), 0)),
                   pl.BlockSpec((n_rows - n_prompt, D), const)],
        out_shape=[jax.ShapeDtypeStruct((n_prompt, D), F32),
                   jax.ShapeDtypeStruct((n_rows - n_prompt, D), F32)],
        scratch_shapes=[pltpu.VMEM((2, SORT_ROWS, D), F32), pltpu.SemaphoreType.DMA((2,))],
        compiler_params=_cparams(48),
        name="combine",
    )(tab, tab, eo, pos, x1, tw, mtab, msmp, g_final)


def kernel(x_prompt, x_sample, c_prompt, c_sample, state_gla, state_hgrn, w_ada, b_ada, g_norm_mix,
           g_norm_ffn, w_in, w_gla_gate2, b_gla_gate, g_gla_onorm, hgrn_lb_logits, g_hgrn_onorm,
           w_branch_a, w_branch_b, w_out, w_router, b_router, w_gate_up, b_gate_up, w_down, b_down,
           g_final):
    assert w_ada.shape[0] == 1, "single-layer trunk only"
    bp, lp, _ = x_prompt.shape
    bs, ls, _ = x_sample.shape
    n_p, n_s = bp * lp, bs * ls
    n = n_p + n_s
    assert n_p % TM == 0 and n_s == TM and lp % TM == 0 and lp % REC_ROWS == 0 and ls <= CHUNK
    assert bp + bs <= 32 and REC_ROWS % TM == 0 and TM // SUB == 8
    assert SORT_ROWS >= TM * TOP_K + N_EXP * (RUN - 1) and RUN_PIECES[0] == TM
    assert n % ROUTE_ROWS == 0 and ROUTE_ROWS % TM == 0
    geom = (n_p, lp, ls, bp)
    xp = x_prompt.reshape(n_p, D)
    xs = x_sample.reshape(n_s, D)

    c_all = jnp.zeros((32, D), F32).at[:bp].set(c_prompt).at[bp:bp + bs].set(c_sample)
    mod = _mod_call(c_all, w_ada[0], b_ada[0].reshape(1, 6 * D))
    sh1, sc1, gt1, sh2, sc2, gt2 = [mod[:, j * D:(j + 1) * D] for j in range(6)]

    def table_and_sample_rows(t):
        return t, jnp.repeat(t[bp:bp + bs], ls, axis=0)

    m1 = table_and_sample_rows(jnp.concatenate([sh1, sc1], axis=1))
    m2 = table_and_sample_rows(jnp.concatenate([gt1, sh2, sc2], axis=1))
    m3 = table_and_sample_rows(gt2)

    wi = w_in[0]
    wa = wi[:, :C_QB].astype(BF16)
    wb = wi[:, C_QB + GATE_RANK:].astype(BF16)
    wlr = jnp.pad(wi[:, 3072:3072 + GATE_RANK], ((0, 0), (0, LANES - GATE_RANK))).astype(BF16)
    w2 = jnp.pad(w_gla_gate2[0], ((0, LANES - GATE_RANK), (0, 0))).astype(BF16)
    arrs = _proj_call(geom, n, xp, xs, *m1, g_norm_mix[0].reshape(1, D), hgrn_lb_logits,
                      wa, wb, wlr, w2, b_gla_gate[0].reshape(1, -1))
    qa, ka, va, ra, ga, qb, kb, gb, ib, rb, ua, ub, worst_decay = arrs
    frame_path = (worst_decay[::8, 0] > EXP_CLAMP - 1.0).astype(I32)
    rec_in = (qa, ka, va, ra, ga, qb, kb, gb, ib, rb)

    gna = g_gla_onorm[0].reshape(1, GLA_DV)
    gnb = g_hgrn_onorm[0].reshape(1, HG_DV)
    step_flags = jnp.max(frame_path[:n_p // TM].reshape(-1, REC_ROWS // TM), axis=1)
    oap, obp, sa_p, sb_p = _rec_call(step_flags, rec_in, gna, gnb, bp, lp, REC_ROWS, 0)
    oas, obs, sa_s, sb_s = _rec_call(jnp.broadcast_to(frame_path[n_p // TM], (bs,)), rec_in, gna,
                                     gnb, bs, ls, ls, n_p, s0=(state_gla[0], state_hgrn[0]))

    wr = jnp.pad(w_router[0], ((0, 0), (0, LANES - N_EXP)))
    wr_hi, wr_lo = _split_bf16(wr)
    br = jnp.pad(b_router[0], (0, LANES - N_EXP), constant_values=NEG).reshape(1, LANES)
    x1, h2, logits = _post_call(geom, n, xp, xs, oap, oas, obp, obs, ua, ub, *m2,
                                g_norm_ffn[0].reshape(1, D), w_branch_a[0].astype(BF16),
                                w_branch_b[0].astype(BF16), w_out[0].astype(BF16), wr_hi, wr_lo, br)

    n_tiles = n // TM
    te, tw, cnt = _route_call(logits, n)
    run = (cnt[::8, :N_EXP].astype(I32) + RUN - 1) // RUN * RUN
    counts = jnp.sum(run, axis=0)
    seg_len = (counts + BM - 1) // BM * BM
    pend = jnp.cumsum(seg_len).astype(I32)
    run_dst = (pend - seg_len)[None, :] + jnp.cumsum(run, axis=0) - run
    run_src = jnp.cumsum(run, axis=1) - run
    tab = _piece_table(run, run_src, run_dst)
    gsrc = jnp.repeat(jnp.pad(run_src.astype(F32), ((0, 0), (0, LANES - N_EXP))), 8, axis=0)
    cap = (n * TOP_K + n_tiles * N_EXP * (RUN - 1) + N_EXP * (BM - 1) + BM - 1) // BM * BM
    block_start = jnp.arange(cap // BM, dtype=I32) * BM
    block_e = jnp.minimum(jnp.sum((pend[None, :] <= block_start[:, None]).astype(I32), axis=1),
                          N_EXP - 1)
    n_valid = pend[-1:] // BM
    ids = jnp.arange(N_EXP, dtype=I32)
    nonempty = counts > 0
    later = jnp.where(nonempty[None, :] & (ids[None, :] > ids[:, None]), ids[None, :], N_EXP)
    next_e = jnp.min(later, axis=1)
    next_e = jnp.where(next_e == N_EXP, -1, next_e)
    is_e = block_e[:, None] == ids[None, :]
    seg = jnp.sum(jnp.where(is_e, (jnp.cumsum(nonempty.astype(I32)) - 1)[None, :], 0), axis=1)
    nxt = jnp.sum(jnp.where(is_e, next_e[None, :], 0), axis=1)
    fill = jnp.clip(jnp.sum(jnp.where(is_e, (pend - seg_len + counts)[None, :], 0), axis=1)
                    - block_start, 0, BM)
    xs_sorted, pos = _dispatch_call(pend, tab, te, gsrc, h2, cap)
    eo = _expert_call(block_e, n_valid, seg, nxt, fill, xs_sorted, w_gate_up[0],
                      b_gate_up[0].reshape(N_EXP, 1, -1), w_down[0], b_down[0].reshape(N_EXP, 1, -1))

    yp, ys = _combine_call(geom, n, tab, eo, pos, x1, tw, *m3, g_final.reshape(1, D))
    return (yp.reshape(bp, lp, D), ys.reshape(bs, ls, D),
            sa_p[None], sb_p[None], sa_s[None], sb_s[None])
```

```python
import functools

import jax
import jax.numpy as jnp
from jax import lax
from jax.experimental import pallas as pl
from jax.experimental.pallas import tpu as pltpu

F32 = jnp.float32
BF16 = jnp.bfloat16
I32 = jnp.int32

EPS = 1e-6
D = 1024
GLA_H, GLA_DK, GLA_DV = 4, 128, 256
HG_H, HG_DK, HG_DV = 8, 128, 128
GATE_RANK = 16
GATE_NORM = 16.0
N_EXP = 32
TOP_K = 4
SWIGLU_LIMIT = 7.0
SWIGLU_ALPHA = 1.702

LANES = 128
TM = 256
CHUNK = 128
SUB = 64
NSUB = CHUNK // SUB
REC_ROWS = 2 * CHUNK
POST_ROWS = 128
EXP_CLAMP = 80.0
BM = 512
ROUTE_ROWS = 1280
RUN = 8
RUN_PIECES = (256, 128, 64, 32, 16, 8)
WAIT_ROWS = 128
SORT_ROWS = TM * TOP_K + 256
TAB_COUNT = 2 * N_EXP * len(RUN_PIECES)
TAB_ROWS = TAB_COUNT + len(RUN_PIECES)
TAB_LEN = 512
NEG = -1e30

C_QA, C_KA, C_VA, C_RA, C_QB, C_FB, C_IB, C_RB, C_UA, C_UB, C_END = (
    0, 512, 1024, 2048, 3072, 4096, 5120, 6144, 7168, 8192, 9216)


def _sigmoid(x):
    return 1.0 / (1.0 + jnp.exp(-x))


def _dot(a, b):
    return jnp.dot(a, b, preferred_element_type=F32)


def _split_bf16(x):
    hi = x.astype(BF16)
    lo = (x - hi.astype(F32)).astype(BF16)
    return hi, lo


def _log2(n):
    assert n > 0 and n & (n - 1) == 0, "power of two expected"
    return n.bit_length() - 1


def _cparams(vmem_mb):
    return pltpu.CompilerParams(dimension_semantics=("arbitrary",),
                                vmem_limit_bytes=vmem_mb * 1024 * 1024)


def _mod_kernel(c_ref, w_ref, b_ref, o_ref):
    c = c_ref[...]
    s = (c * _sigmoid(c)).astype(BF16)
    o_ref[...] = _dot(s, w_ref[...].astype(BF16)) + b_ref[...]


def _mod_call(c_all, w_ada, b_ada):
    rows = c_all.shape[0]
    return pl.pallas_call(
        _mod_kernel,
        grid=(6,),
        in_specs=[pl.BlockSpec((rows, D), lambda j: (0, 0)),
                  pl.BlockSpec((D, D), lambda j: (0, j)),
                  pl.BlockSpec((1, D), lambda j: (0, j))],
        out_specs=pl.BlockSpec((rows, D), lambda j: (0, j)),
        out_shape=jax.ShapeDtypeStruct((rows, 6 * D), F32),
        compiler_params=_cparams(32),
        name="mod",
    )(c_all, w_ada, b_ada)


def _row_mod(i, geom, table_ref, sample_ref):
    n_prompt_rows, prompt_len, _, n_prompt_seq = geom
    seq = jnp.minimum((i * TM) >> _log2(prompt_len), n_prompt_seq - 1)
    return jnp.where(i < n_prompt_rows // TM, table_ref[pl.ds(seq, 1), :], sample_ref[...])


def _proj_kernel(geom, xp_ref, xs_ref, mtab_ref, msmp_ref, g_ref, lbl_ref, wa_ref, wb_ref, wlr_ref,
                 w2_ref, bg_ref,
                 qa_ref, ka_ref, va_ref, ra_ref, ga_ref, qb_ref, kb_ref, gb_ref, ib_ref,
                 rb_ref, ua_ref, ub_ref, wd_ref):
    i = pl.program_id(0)
    n_prompt_tiles = geom[0] // TM
    x = jnp.where(i < n_prompt_tiles, xp_ref[...], xs_ref[...])
    mod = _row_mod(i, geom, mtab_ref, msmp_ref)
    sh, sc = mod[:, :D], mod[:, D:]
    ms = jnp.mean(x * x, axis=-1, keepdims=True)
    h = x * lax.rsqrt(ms + EPS) * g_ref[...]
    hb = (h * (1.0 + sc) + sh).astype(BF16)

    def proj(a, b):
        if b <= C_QB:
            return _dot(hb, wa_ref[:, a:b])
        return _dot(hb, wb_ref[:, a - C_QB:b - C_QB])

    cols = (C_QA, C_KA, C_VA, C_RA, C_QB, C_FB, C_IB, C_RB, C_UA, C_UB, C_END)
    z_qa, z_ka, z_va, z_ra, z_qb, z_fb, z_ib, z_rb, z_ua, z_ub = [
        proj(a, b) for a, b in zip(cols[:-1], cols[1:])]
    lr_hi, lr_lo = _split_bf16(_dot(hb, wlr_ref[...]))
    qa_ref[...] = (z_qa * GLA_DK ** -0.5).astype(BF16)
    ka_ref[...] = z_ka.astype(BF16)
    va_ref[...] = z_va.astype(BF16)
    ra_ref[...] = (z_ra * _sigmoid(z_ra)).astype(BF16)
    xg = _dot(lr_hi, w2_ref[...]) + _dot(lr_lo, w2_ref[...]) + bg_ref[...]
    ga = (jnp.minimum(xg, 0.0) - jnp.log1p(jnp.exp(-jnp.abs(xg)))) * (1.0 / GATE_NORM)
    ga_ref[...] = ga
    qb_ref[...] = (z_qb * _sigmoid(z_qb) * HG_DK ** -0.5).astype(BF16)
    lbl = lbl_ref[...]
    e = jnp.exp(lbl - jnp.max(lbl, axis=0, keepdims=True))
    lb = e[0:1, :] / jnp.sum(e, axis=0, keepdims=True)
    kb_ref[...] = ((1.0 - lb) * _sigmoid(-z_fb)).astype(BF16)
    gb = jnp.log(lb + (1.0 - lb) * _sigmoid(z_fb))
    gb_ref[...] = gb
    worst = jnp.zeros((1, 1), F32)
    for g in (ga, gb):
        for j in range(TM // SUB):
            total = jnp.sum(g[j * SUB:(j + 1) * SUB, :], axis=0, keepdims=True)
            worst = jnp.maximum(worst, jnp.max(-total, axis=1, keepdims=True))
    wd_ref[...] = jnp.broadcast_to(worst, (8, LANES))
    ib_ref[...] = z_ib.astype(BF16)
    rb_ref[...] = (z_rb * _sigmoid(z_rb)).astype(BF16)
    ua_ref[...] = _sigmoid(z_ua).astype(BF16)
    ub_ref[...] = _sigmoid(z_ub).astype(BF16)


def _proj_call(geom, n_rows, xp, xs, mtab, msmp, g_mix, lbl, wa, wb, wlr, w2, bg):
    n_tiles = n_rows // TM
    last_p = geom[0] // TM - 1
    const = lambda i: (0, 0)
    row = lambda i: (i, 0)
    widths = [(512, BF16), (512, BF16), (D, BF16), (D, BF16), (512, F32), (D, BF16), (D, BF16),
              (D, F32), (D, BF16), (D, BF16), (D, BF16), (D, BF16)]
    return pl.pallas_call(
        functools.partial(_proj_kernel, geom),
        grid=(n_tiles,),
        in_specs=[pl.BlockSpec((TM, D), lambda i: (jnp.minimum(i, last_p), 0)),
                  pl.BlockSpec(xs.shape, const),
                  pl.BlockSpec(mtab.shape, const), pl.BlockSpec(msmp.shape, const),
                  pl.BlockSpec((1, D), const), pl.BlockSpec(lbl.shape, const),
                  pl.BlockSpec(wa.shape, const, pipeline_mode=pl.Buffered(1)),
                  pl.BlockSpec(wb.shape, const, pipeline_mode=pl.Buffered(1)),
                  pl.BlockSpec(wlr.shape, const), pl.BlockSpec(w2.shape, const),
                  pl.BlockSpec(bg.shape, const)],
        out_specs=[pl.BlockSpec((TM, w), row) for w, _ in widths] + [pl.BlockSpec((8, LANES), row)],
        out_shape=([jax.ShapeDtypeStruct((n_rows, w), dt) for w, dt in widths]
                   + [jax.ShapeDtypeStruct((n_tiles * 8, LANES), F32)]),
        compiler_params=_cparams(56),
        name="proj",
    )(xp, xs, mtab, msmp, g_mix, lbl, wa, wb, wlr, w2, bg)


def _rec_prep(tri, g, q, k):
    g_hi, g_lo = _split_bf16(g)
    cum = _dot(tri, g_hi) + _dot(tri, g_lo)
    width = cum.shape[1]
    tot = cum[CHUNK - 1:CHUNK, :]
    refs = [jnp.zeros((1, width), F32)] + [cum[j * SUB - 1:j * SUB, :] for j in range(1, NSUB)]
    d = cum - jnp.concatenate([jnp.broadcast_to(b, (SUB, width)) for b in refs], axis=0)
    qn = q.astype(F32) * jnp.exp(d)
    kn = k.astype(F32) * jnp.exp(jnp.minimum(-d, EXP_CLAMP))
    sub = lambda x, j: x[j * SUB:(j + 1) * SUB, :]
    qs = jnp.concatenate([sub(qn, j) * jnp.exp(refs[j]) for j in range(NSUB)],
                         axis=0).astype(BF16)
    kd = jnp.concatenate([sub(kn, j) * jnp.exp(tot - refs[j]) for j in range(NSUB)], axis=0)
    et = jnp.exp(tot)
    zero_rows = lambda n: jnp.zeros((n, width), F32)
    q_to, k_of = [], []
    for j in range(NSUB):
        parts = [zero_rows(j * SUB)] if j else []
        parts += [sub(qn, i) * jnp.exp(refs[i] - refs[j]) if i > j else sub(qn, i)
                  for i in range(j, NSUB)]
        q_to.append(jnp.concatenate(parts, axis=0).astype(BF16))
        parts = ([zero_rows(j * SUB)] if j else []) + [sub(kn, j)]
        if j + 1 < NSUB:
            parts.append(zero_rows(CHUNK - (j + 1) * SUB))
        k_of.append(jnp.concatenate(parts, axis=0).astype(BF16))
    return qs, kd, et, q_to, k_of


def _rec_kernel(rows, has_s0, fp_ref, *refs):
    (qa_ref, ka_ref, va_ref, ra_ref, ga_ref, qb_ref, kb_ref, gb_ref, ib_ref, rb_ref,
     gna_ref, gnb_ref) = refs[:12]
    refs = refs[12:]
    if has_s0:
        s0a_ref, s0b_ref = refs[:2]
        refs = refs[2:]
    oga_ref, ogb_ref, sa_ref, sb_ref, qf, kf, vf, of = refs

    @pl.when(pl.program_id(1) == 0)
    def _():
        if has_s0:
            sa_ref[...] = s0a_ref[...]
            sb_ref[...] = s0b_ref[...]
        else:
            sa_ref[...] = jnp.zeros_like(sa_ref)
            sb_ref[...] = jnp.zeros_like(sb_ref)

    ti = lax.broadcasted_iota(I32, (CHUNK, CHUNK), 0)
    si = lax.broadcasted_iota(I32, (CHUNK, CHUNK), 1)
    causal = ti >= si
    tri = causal.astype(BF16)
    n_chunks = max(rows // CHUNK, 1)
    rows_out = min(rows, CHUNK)

    def load(ref, c):
        if rows >= CHUNK:
            return ref[c * CHUNK:(c + 1) * CHUNK, :]
        x = ref[...]
        return jnp.concatenate([x, jnp.zeros((CHUNK - rows, x.shape[1]), x.dtype)], axis=0)

    mixers = ((GLA_H, GLA_DK, GLA_DV, ga_ref, qa_ref, ka_ref, va_ref, ra_ref, sa_ref, gna_ref, oga_ref),
              (HG_H, HG_DK, HG_DV, gb_ref, qb_ref, kb_ref, ib_ref, rb_ref, sb_ref, gnb_ref, ogb_ref))
    use_frames = fp_ref[pl.program_id(0) * pl.num_programs(1) + pl.program_id(1)] != 0
    pl.when(jnp.logical_not(use_frames))(
        lambda: _rec_factored(tri, causal, load, mixers, n_chunks, rows_out))
    pl.when(use_frames)(lambda: _rec_frames(rows, mixers, qf, kf, vf, of))


def _rec_factored(tri, causal, load, mixers, n_chunks, rows_out):
    units = []
    for c in range(n_chunks):
        for (nh, dk, dv, g_ref, q_ref, k_ref, v_ref, r_ref, s_ref, gn_ref, o_ref) in mixers:
            qs, kd, et, q_to, k_of = _rec_prep(tri, load(g_ref, c), load(q_ref, c), load(k_ref, c))
            for h in range(nh):
                ks = slice(h * dk, (h + 1) * dk)
                units.append(dict(
                    c=c, h=h, dk=dk, dv=dv, vs=slice(h * dv, (h + 1) * dv), s_ref=s_ref,
                    gn_ref=gn_ref, o_ref=o_ref, v_ref=v_ref, r_ref=r_ref,
                    qs=qs[:, ks], kd=kd[:, ks], et=et[:, ks],
                    q_to=jnp.concatenate([x[:, ks] for x in q_to], axis=1),
                    k_of=jnp.concatenate([x[:, ks] for x in k_of], axis=1)))
    for u in units:
        u["a"] = lax.dot_general(u["q_to"], u["k_of"], (((1,), (1,)), ((), ())),
                                 preferred_element_type=F32)
    for u in units:
        u["lhs"] = jnp.concatenate([jnp.where(causal, u["a"], 0.0).astype(BF16),
                                    u["kd"].T.astype(BF16)], axis=0)
    for u in units:
        u["av"] = _dot(u["lhs"], load(u["v_ref"], u["c"])[:, u["vs"]])
    for u in units:
        dk, dv, h = u["dk"], u["dv"], u["h"]
        s = u["s_ref"][0, h]
        o = _dot(u["qs"], s.astype(BF16)) + u["av"][:CHUNK]
        et_col = jnp.broadcast_to(u["et"], (dk, dk)).T
        u["s_ref"][0, h] = s * jnp.tile(et_col, (1, dv // dk)) + u["av"][CHUNK:]
        ms = jnp.mean(o * o, axis=-1, keepdims=True)
        og = (o * lax.rsqrt(ms + EPS) * u["gn_ref"][...]
              * load(u["r_ref"], u["c"])[:, u["vs"]].astype(F32))
        r0 = u["c"] * CHUNK
        u["o_ref"][r0:r0 + rows_out, u["vs"]] = og[:rows_out].astype(BF16)


def _rec_frames(rows, mixers, qf, kf, vf, of):
    sublane = lax.broadcasted_iota(I32, (8, 1), 0)
    of[...] = jnp.zeros_like(of)
    for (nh, dk, dv, g_ref, q_ref, k_ref, v_ref, r_ref, s_ref, gn_ref, o_ref) in mixers:
        qf[:, :nh * dk] = q_ref[...].astype(F32)
        kf[:, :nh * dk] = k_ref[...].astype(F32)
        vf[:, :nh * dv] = v_ref[...].astype(F32)
        for h in range(nh):
            ks = slice(h * dk, (h + 1) * dk)
            vs = slice(h * dv, (h + 1) * dv)

            def frame(t, st, ks=ks, vs=vs, g_ref=g_ref):
                rows8 = pl.ds(pl.multiple_of((t >> 3) << 3, 8), 8)
                is_t = sublane == (t & 7)
                at = lambda ref, cols: jnp.where(is_t, ref[rows8, cols], 0.0)
                kv = lax.dot_general(at(vf, vs).astype(BF16), at(kf, ks).astype(BF16),
                                     (((0,), (0,)), ((), ())),
                                     preferred_element_type=F32)
                g_t = jnp.sum(at(g_ref, ks), axis=0, keepdims=True)
                st = st * jnp.exp(g_t) + kv
                o8 = lax.dot_general(at(qf, ks).astype(BF16), st.astype(BF16),
                                     (((1,), (1,)), ((), ())),
                                     preferred_element_type=F32)
                of[rows8, vs] = jnp.where(is_t, o8, of[rows8, vs])
                return st

            s_ref[0, h] = lax.fori_loop(0, rows, frame, s_ref[0, h].T).T
            o = of[:, vs]
            ms = jnp.mean(o * o, axis=-1, keepdims=True)
            og = o * lax.rsqrt(ms + EPS) * gn_ref[...] * r_ref[:, vs].astype(F32)
            o_ref[:, vs] = og.astype(BF16)


def _rec_call(frame_path, arrs, gna, gnb, n_seq, seq_len, rows, row0, s0=None):
    steps = seq_len // rows
    blk0 = row0 // rows
    row_in = lambda b, t, fp: (blk0 + b * steps + t, 0)
    row_out = lambda b, t, fp: (b * steps + t, 0)
    const = lambda b, t, fp: (0, 0)
    st = lambda b, t, fp: (b, 0, 0, 0)
    in_specs = [pl.BlockSpec((rows, a.shape[1]), row_in) for a in arrs]
    in_specs += [pl.BlockSpec(gna.shape, const), pl.BlockSpec(gnb.shape, const)]
    args = list(arrs) + [gna, gnb]
    if s0 is not None:
        in_specs += [pl.BlockSpec((1, GLA_H, GLA_DK, GLA_DV), st),
                     pl.BlockSpec((1, HG_H, HG_DK, HG_DV), st)]
        args += [s0[0], s0[1]]
    n_rows = n_seq * seq_len
    return pl.pallas_call(
        functools.partial(_rec_kernel, rows, s0 is not None),
        grid_spec=pltpu.PrefetchScalarGridSpec(
            num_scalar_prefetch=1,
            grid=(n_seq, steps),
            in_specs=in_specs,
            out_specs=[pl.BlockSpec((rows, D), row_out), pl.BlockSpec((rows, D), row_out),
                       pl.BlockSpec((1, GLA_H, GLA_DK, GLA_DV), st),
                       pl.BlockSpec((1, HG_H, HG_DK, HG_DV), st)],
            scratch_shapes=[pltpu.VMEM((rows, D), F32)] * 4),
        out_shape=[jax.ShapeDtypeStruct((n_rows, D), BF16), jax.ShapeDtypeStruct((n_rows, D), BF16),
                   jax.ShapeDtypeStruct((n_seq, GLA_H, GLA_DK, GLA_DV), F32),
                   jax.ShapeDtypeStruct((n_seq, HG_H, HG_DK, HG_DV), F32)],
        compiler_params=pltpu.CompilerParams(dimension_semantics=("arbitrary", "arbitrary"),
                                             vmem_limit_bytes=48 * 1024 * 1024),
        name="rec_s0" if s0 is not None else "rec",
    )(frame_path, *args)


def _post_kernel(geom, xp_ref, xs_ref, oap_ref, oas_ref, obp_ref, obs_ref, ua_ref, ub_ref,
                 mtab_ref, msmp_ref, gf_ref, wba_ref, wbb_ref, wo_ref, wrh_ref, wrl_ref, br_ref,
                 x1_ref, h2_ref, lg_ref):
    i = pl.program_id(0)
    is_prompt = i < geom[0] // TM
    mod = _row_mod(i, geom, mtab_ref, msmp_ref)
    groups = [slice(j * POST_ROWS, (j + 1) * POST_ROWS) for j in range(TM // POST_ROWS)]
    pa = [_dot(jnp.where(is_prompt, oap_ref[g, :], oas_ref[g, :]), wba_ref[...]) for g in groups]
    pb = [_dot(jnp.where(is_prompt, obp_ref[g, :], obs_ref[g, :]), wbb_ref[...]) for g in groups]
    merged = [(ua_ref[g, :].astype(F32) * a + ub_ref[g, :].astype(F32) * b).astype(BF16)
              for g, a, b in zip(groups, pa, pb)]
    y = [_dot(m, wo_ref[...]) for m in merged]
    h2s = []
    for g, yg in zip(groups, y):
        x1 = jnp.where(is_prompt, xp_ref[g, :], xs_ref[g, :]) + mod[g, :D] * yg
        x1_ref[g, :] = x1
        ms = jnp.mean(x1 * x1, axis=-1, keepdims=True)
        h2 = x1 * lax.rsqrt(ms + EPS) * gf_ref[...] * (1.0 + mod[g, 2 * D:]) + mod[g, D:2 * D]
        h2_ref[g, :] = h2
        h2s.append(_split_bf16(h2))
    for g, (h_hi, h_lo) in zip(groups, h2s):
        lg_ref[g, :] = (_dot(h_hi, wrh_ref[...]) + _dot(h_lo, wrh_ref[...])
                        + _dot(h_hi, wrl_ref[...]) + br_ref[...])


def _post_call(geom, n_rows, xp, xs, oap, oas, obp, obs, ua, ub, mtab, msmp, g_ffn, wba, wbb, wo,
               wrh, wrl, br):
    n_tiles = n_rows // TM
    last_p = geom[0] // TM - 1
    const = lambda i: (0, 0)
    row = lambda i: (i, 0)
    prow = pl.BlockSpec((TM, D), lambda i: (jnp.minimum(i, last_p), 0))
    full = lambda a: pl.BlockSpec(a.shape, const)
    return pl.pallas_call(
        functools.partial(_post_kernel, geom),
        grid=(n_tiles,),
        in_specs=[prow, full(xs), prow, full(oas), prow, full(obs),
                  pl.BlockSpec((TM, D), row), pl.BlockSpec((TM, D), row),
                  full(mtab), full(msmp), full(g_ffn), full(wba), full(wbb), full(wo),
                  full(wrh), full(wrl), full(br)],
        out_specs=[pl.BlockSpec((TM, D), row), pl.BlockSpec((TM, D), row),
                   pl.BlockSpec((TM, LANES), row)],
        out_shape=[jax.ShapeDtypeStruct((n_rows, D), F32), jax.ShapeDtypeStruct((n_rows, D), F32),
                   jax.ShapeDtypeStruct((n_rows, LANES), F32)],
        compiler_params=_cparams(48),
        name="post",
    )(xp, xs, oap, oas, obp, obs, ua, ub, mtab, msmp, g_ffn, wba, wbb, wo, wrh, wrl, br)


def _multi_hot(te):
    lane = lax.broadcasted_iota(I32, (TM, LANES), 1)
    m = jnp.zeros((TM, LANES), F32)
    for k in range(TOP_K):
        m = m + (lane == te[:, k:k + 1]).astype(F32)
    return m


def _route_kernel(lg_ref, te_ref, tw_ref, cnt_ref):
    logit = lg_ref[...]
    lane = lax.broadcasted_iota(I32, (ROUTE_ROWS, LANES), 1)
    lane_f = lane.astype(F32)
    vals, idxs = [], []
    for _ in range(TOP_K):
        m = jnp.max(logit, axis=-1, keepdims=True)
        idx = jnp.min(jnp.where(logit == m, lane_f, float(LANES)), axis=-1, keepdims=True)
        vals.append(m)
        idxs.append(idx)
        logit = jnp.where(lane_f == idx, -jnp.inf, logit)
    es = [jnp.exp(v - vals[0]) for v in vals]
    den = es[0] + es[1] + es[2] + es[3]
    te = jnp.zeros((ROUTE_ROWS, LANES), F32)
    tw = jnp.zeros((ROUTE_ROWS, LANES), F32)
    for k in range(TOP_K):
        te = jnp.where(lane == k, idxs[k], te)
        tw = jnp.where(lane == k, es[k] / den, tw)
    te = te.astype(I32)
    te_ref[...] = te
    tw_ref[...] = tw
    for t in range(ROUTE_ROWS // TM):
        cnt_ref[t * 8:(t + 1) * 8, :] = jnp.broadcast_to(
            jnp.sum(_multi_hot(te[t * TM:(t + 1) * TM]), axis=0, keepdims=True), (8, LANES))


def _route_call(logits, n_rows):
    n_steps = n_rows // ROUTE_ROWS
    row = lambda i: (i, 0)
    tiles = ROUTE_ROWS // TM
    return pl.pallas_call(
        _route_kernel,
        grid=(n_steps,),
        in_specs=[pl.BlockSpec((ROUTE_ROWS, LANES), row)],
        out_specs=[pl.BlockSpec((ROUTE_ROWS, LANES), row), pl.BlockSpec((ROUTE_ROWS, LANES), row),
                   pl.BlockSpec((tiles * 8, LANES), row)],
        out_shape=[jax.ShapeDtypeStruct((n_rows, LANES), I32),
                   jax.ShapeDtypeStruct((n_rows, LANES), F32),
                   jax.ShapeDtypeStruct((n_rows // TM * 8, LANES), F32)],
        compiler_params=_cparams(32),
        name="route",
    )(logits)


def _wait_rows(rows, wait_n_rows):
    lax.fori_loop(0, rows >> _log2(WAIT_ROWS), lambda j, c: (wait_n_rows(WAIT_ROWS), c)[1], 0)
    lax.fori_loop(0, (rows & (WAIT_ROWS - 1)) >> _log2(RUN),
                  lambda j, c: (wait_n_rows(RUN), c)[1], 0)


def _for_each_run_piece(tab_ref, fn):
    for s, p in enumerate(RUN_PIECES):
        def body(j, c, s=s, p=p):
            fn(pl.multiple_of(tab_ref[0, 0, s * 2 * N_EXP + j], RUN),
               pl.multiple_of(tab_ref[0, 0, (s * 2 + 1) * N_EXP + j], RUN), p)
            return c

        lax.fori_loop(0, tab_ref[0, 0, TAB_COUNT + s], body, 0)


def _piece_table(run, run_src, run_dst):
    ids = jnp.arange(N_EXP, dtype=I32)
    earlier = (ids[None, :] < ids[:, None]).astype(I32)
    src_l, dst_l, cnt_l = [], [], []
    for p in RUN_PIECES:
        has = (run & p) != 0
        off = run - (run & (2 * p - 1))
        slot = jnp.sum(has[:, None, :] * earlier[None, :, :], axis=2)
        put = has[:, :, None] & (slot[:, :, None] == ids[None, None, :])
        src_l.append(jnp.sum(jnp.where(put, (run_src + off)[:, :, None], 0), axis=1))
        dst_l.append(jnp.sum(jnp.where(put, (run_dst + off)[:, :, None], 0), axis=1))
        cnt_l.append(jnp.sum(has.astype(I32), axis=1))
    cols = [a for pair in zip(src_l, dst_l) for a in pair]
    tab = jnp.concatenate(cols + [jnp.stack(cnt_l, axis=1), jnp.sum(run, axis=1, keepdims=True)],
                          axis=1)
    tab = jnp.pad(tab, ((0, 0), (0, TAB_LEN - tab.shape[1])))
    return tab.reshape(run.shape[0], 1, TAB_LEN)


def _dispatch_kernel(pend_ref, tab_ref, tprev_ref, te_ref, gsrc_ref, h2_ref, xs_ref, pos_ref, z_ref,
                     zeros, sem, zsem):
    @pl.when(pl.program_id(0) == 0)
    def _():
        zeros[...] = jnp.zeros_like(zeros)

        def zero_copy(e):
            start = pl.multiple_of(pend_ref[e] - BM, BM)
            return pltpu.make_async_copy(zeros, xs_ref.at[pl.ds(start, BM)], zsem)

        def nonempty(e):
            return pend_ref[e] > (pend_ref[e - 1] if e > 0 else 0)

        for e in range(N_EXP):
            pl.when(nonempty(e))(lambda e=e: zero_copy(e).start())
        for e in range(N_EXP):
            pl.when(nonempty(e))(lambda e=e: zero_copy(e).wait())

        def tail_copy(j):
            return pltpu.make_async_copy(zeros, xs_ref.at[pl.ds(pl.multiple_of(j * BM, BM), BM)],
                                         zsem)

        first, last = pend_ref[N_EXP - 1] // BM, xs_ref.shape[0] // BM
        lax.fori_loop(first, last, lambda j, c: (tail_copy(j).start(), c)[1], 0)
        lax.fori_loop(first, last, lambda j, c: (tail_copy(j).wait(), c)[1], 0)

    te = te_ref[...]
    ri = lax.broadcasted_iota(I32, (TM, TM), 0)
    ci = lax.broadcasted_iota(I32, (TM, TM), 1)
    rank = _dot((ri > ci).astype(BF16), _multi_hot(te).astype(BF16)) + gsrc_ref[0:1, :]
    lane = lax.broadcasted_iota(I32, (TM, LANES), 1)
    pos = jnp.zeros((TM, LANES), F32)
    for k in range(TOP_K):
        p = jnp.sum(jnp.where(lane == te[:, k:k + 1], rank, 0.0), axis=-1, keepdims=True)
        pos = jnp.where(lane == k, p, pos)
    pos_ref[...] = pos.astype(I32)
    pos_t = pos.T.astype(I32)
    row = lax.broadcasted_iota(I32, (SORT_ROWS, TM), 0)
    onehot = jnp.zeros((SORT_ROWS, TM), F32)
    for k in range(TOP_K):
        onehot = jnp.where(row == pos_t[k:k + 1, :], 1.0, onehot)
    i = pl.program_id(0)
    slot = lax.rem(i, 2)
    z_ref[slot] = _dot(onehot.astype(BF16), h2_ref[...].astype(BF16))

    _for_each_run_piece(tab_ref, lambda s, d, p: pltpu.make_async_copy(
        z_ref.at[slot, pl.ds(s, p)], xs_ref.at[pl.ds(d, p)], sem.at[slot]).start())

    def wait_tile(table_ref, which):
        _wait_rows(table_ref[0, 0, TAB_ROWS], lambda n: pltpu.make_async_copy(
            z_ref.at[which, pl.ds(0, n)], xs_ref.at[pl.ds(0, n)], sem.at[which]).wait())

    pl.when(i > 0)(lambda: wait_tile(tprev_ref, 1 - slot))
    pl.when(i == pl.num_programs(0) - 1)(lambda: wait_tile(tab_ref, slot))


def _dispatch_call(pend, tab, te, gsrc, h2, cap):
    n_tiles = tab.shape[0]
    row = lambda i, pe: (i, 0)
    return pl.pallas_call(
        _dispatch_kernel,
        grid_spec=pltpu.PrefetchScalarGridSpec(
            num_scalar_prefetch=1,
            grid=(n_tiles,),
            in_specs=[pl.BlockSpec((1, 1, TAB_LEN), lambda i, pe: (i, 0, 0), memory_space=pltpu.SMEM),
                      pl.BlockSpec((1, 1, TAB_LEN), lambda i, pe: (jnp.maximum(i - 1, 0), 0, 0),
                                   memory_space=pltpu.SMEM),
                      pl.BlockSpec((TM, LANES), row), pl.BlockSpec((8, LANES), row),
                      pl.BlockSpec((TM, D), row)],
            out_specs=[pl.BlockSpec(memory_space=pl.ANY), pl.BlockSpec((TM, LANES), row)],
            scratch_shapes=[pltpu.VMEM((2, SORT_ROWS, D), F32), pltpu.VMEM((BM, D), F32),
                            pltpu.SemaphoreType.DMA((2,)), pltpu.SemaphoreType.DMA]),
        out_shape=[jax.ShapeDtypeStruct((cap, D), F32),
                   jax.ShapeDtypeStruct((n_tiles * TM, LANES), I32)],
        compiler_params=_cparams(40),
        name="dispatch",
    )(pend, tab, tab, te, gsrc, h2)


def _expert_kernel(be_ref, nv_ref, seg_ref, nxt_ref, fill_ref, x_ref, wgu_hbm, bgu_ref, wd_hbm, bd_ref,
                   o_ref, wgu_f, wd_f, wgu_s, wd_s, sem):
    j = pl.program_id(0)
    jc = jnp.minimum(j, nv_ref[0] - 1)
    e = be_ref[jc]
    slot = lax.rem(seg_ref[jc], 2)
    first = (j == 0) | ((j < nv_ref[0]) & (e != be_ref[jnp.maximum(jc - 1, 0)]))

    def fetch(expert, into):
        return (pltpu.make_async_copy(wgu_hbm.at[expert], wgu_f.at[into], sem.at[0, into]),
                pltpu.make_async_copy(wd_hbm.at[expert], wd_f.at[into], sem.at[1, into]))

    @pl.when(j == 0)
    def _():
        for c in fetch(e, slot):
            c.start()

    @pl.when(first)
    def _():
        for c in fetch(e, slot):
            c.wait()

        @pl.when(nxt_ref[jc] >= 0)
        def _():
            for c in fetch(nxt_ref[jc], 1 - slot):
                c.start()

        wgu_s[...] = wgu_f[slot].astype(BF16)
        wd_s[...] = wd_f[slot].astype(BF16)

    def mlp(groups):
        gus = [_dot(x_ref[g, :].astype(BF16), wgu_s[...]) + bgu_ref[0] for g in groups]
        acts = []
        for gu in gus:
            gate = jnp.minimum(gu[:, :D], SWIGLU_LIMIT)
            up = jnp.clip(gu[:, D:], -SWIGLU_LIMIT, SWIGLU_LIMIT)
            acts.append(((up + 1.0) * (gate * _sigmoid(SWIGLU_ALPHA * gate))).astype(BF16))
        for g, act in zip(groups, acts):
            o_ref[g, :] = _dot(act, wd_s[...]) + bd_ref[0]

    lower, upper = slice(0, BM // 2), slice(BM // 2, BM)
    needs_upper = fill_ref[jc] > BM // 2

    @pl.when((j < nv_ref[0]) & needs_upper)
    def _():
        mlp([lower, upper])

    @pl.when((j < nv_ref[0]) & jnp.logical_not(needs_upper))
    def _():
        mlp([lower])
        o_ref[upper, :] = jnp.zeros((BM // 2, D), F32)

    @pl.when(j >= nv_ref[0])
    def _():
        o_ref[...] = jnp.zeros_like(o_ref)


def _expert_call(block_e, n_valid, seg, nxt, fill, xs, wgu, bgu, wd, bd):
    n_blocks = xs.shape[0] // BM
    blk = lambda j, be, nv, sg, nx, fl: (jnp.minimum(j, nv[0] - 1), 0)
    blk_out = lambda j, be, nv, sg, nx, fl: (j, 0)
    exp = lambda j, be, nv, sg, nx, fl: (be[jnp.minimum(j, nv[0] - 1)], 0, 0)
    return pl.pallas_call(
        _expert_kernel,
        grid_spec=pltpu.PrefetchScalarGridSpec(
            num_scalar_prefetch=5,
            grid=(n_blocks,),
            in_specs=[pl.BlockSpec((BM, D), blk),
                      pl.BlockSpec(memory_space=pl.ANY), pl.BlockSpec((1, 1, 2 * D), exp),
                      pl.BlockSpec(memory_space=pl.ANY), pl.BlockSpec((1, 1, D), exp)],
            out_specs=pl.BlockSpec((BM, D), blk_out),
            scratch_shapes=[pltpu.VMEM((2, D, 2 * D), F32), pltpu.VMEM((2, D, D), F32),
                            pltpu.VMEM((D, 2 * D), BF16), pltpu.VMEM((D, D), BF16),
                            pltpu.SemaphoreType.DMA((2, 2))]),
        out_shape=jax.ShapeDtypeStruct(xs.shape, F32),
        compiler_params=_cparams(56),
        name="expert",
    )(block_e, n_valid, seg, nxt, fill, xs, wgu, bgu, wd, bd)


def _combine_kernel(geom, tcur_ref, tnxt_ref, eo_ref, pos_ref, x1_ref, tw_ref, mtab_ref, msmp_ref,
                    gfin_ref, yp_ref, ys_ref, buf, sem):
    i = pl.program_id(0)
    n_prompt_tiles = geom[0] // TM
    slot = lax.rem(i, 2)

    def gather(tab_ref, into):
        _for_each_run_piece(tab_ref, lambda s, d, p: pltpu.make_async_copy(
            eo_ref.at[pl.ds(d, p)], buf.at[into, pl.ds(s, p)], sem.at[into]).start())

    @pl.when(i == 0)
    def _():
        buf[...] = jnp.zeros_like(buf)
        gather(tcur_ref, 0)

    @pl.when(i + 1 < pl.num_programs(0))
    def _():
        gather(tnxt_ref, 1 - slot)

    _wait_rows(tcur_ref[0, 0, TAB_ROWS], lambda n: pltpu.make_async_copy(
        eo_ref.at[pl.ds(0, n)], buf.at[slot, pl.ds(0, n)], sem.at[slot]).wait())
    gt2 = _row_mod(i, geom, mtab_ref, msmp_ref)
    tw = tw_ref[...]
    pos = pos_ref[...]
    col = lax.broadcasted_iota(I32, (TM, SORT_ROWS), 1)
    wsel = jnp.zeros((TM, SORT_ROWS), F32)
    for k in range(TOP_K):
        wsel = jnp.where(col == pos[:, k:k + 1], tw[:, k:k + 1], wsel)
    y = _dot(wsel.astype(BF16), buf[slot].astype(BF16))
    x2 = x1_ref[...] + gt2 * y
    ms = jnp.mean(x2 * x2, axis=-1, keepdims=True)
    out = x2 * lax.rsqrt(ms + EPS) * gfin_ref[...]

    @pl.when(i < n_prompt_tiles)
    def _():
        yp_ref[...] = out

    @pl.when(i >= n_prompt_tiles)
    def _():
        ys_ref[...] = out


def _combine_call(geom, n_rows, tab, eo, pos, x1, tw, mtab, msmp, g_final):
    n_tiles = n_rows // TM
    n_prompt = geom[0]
    last_p = n_prompt // TM - 1
    const = lambda i: (0, 0)
    row = lambda i: (i, 0)
    return pl.pallas_call(
        functools.partial(_combine_kernel, geom),
        grid=(n_tiles,),
        in_specs=[pl.BlockSpec((1, 1, TAB_LEN), lambda i: (i, 0, 0), memory_space=pltpu.SMEM),
                  pl.BlockSpec((1, 1, TAB_LEN), lambda i: (jnp.minimum(i + 1, n_tiles - 1), 0, 0),
                               memory_space=pltpu.SMEM),
                  pl.BlockSpec(memory_space=pl.ANY), pl.BlockSpec((TM, LANES), row),
                  pl.BlockSpec((TM, D), row), pl.BlockSpec((TM, LANES), row),
                  pl.BlockSpec(mtab.shape, const), pl.BlockSpec(msmp.shape, const),
                  pl.BlockSpec((1, D), const)],
        out_specs=[pl.BlockSpec((TM, D), lambda i: (jnp.minimum(i, last_p), 0)),
                   pl.BlockSpec((n_rows - n_prompt, D), const)],
        out_shape=[jax.ShapeDtypeStruct((n_prompt, D), F32),
                   jax.ShapeDtypeStruct((n_rows - n_prompt, D), F32)],
        scratch_shapes=[pltpu.VMEM((2, SORT_ROWS, D), F32), pltpu.SemaphoreType.DMA((2,))],
        compiler_params=_cparams(48),
        name="combine",
    )(tab, tab, eo, pos, x1, tw, mtab, msmp, g_final)


def kernel(x_prompt, x_sample, c_prompt, c_sample, state_gla, state_hgrn, w_ada, b_ada, g_norm_mix,
           g_norm_ffn, w_in, w_gla_gate2, b_gla_gate, g_gla_onorm, hgrn_lb_logits, g_hgrn_onorm,
           w_branch_a, w_branch_b, w_out, w_router, b_router, w_gate_up, b_gate_up, w_down, b_down,
           g_final):
    assert w_ada.shape[0] == 1, "single-layer trunk only"
    bp, lp, _ = x_prompt.shape
    bs, ls, _ = x_sample.shape
    n_p, n_s = bp * lp, bs * ls
    n = n_p + n_s
    assert n_p % TM == 0 and n_s == TM and lp % TM == 0 and lp % REC_ROWS == 0 and ls <= CHUNK
    assert bp + bs <= 32 and REC_ROWS % TM == 0 and TM % SUB == 0
    assert SORT_ROWS >= TM * TOP_K + N_EXP * (RUN - 1) and RUN_PIECES[0] == TM
    assert n % ROUTE_ROWS == 0 and ROUTE_ROWS % TM == 0
    geom = (n_p, lp, ls, bp)
    xp = x_prompt.reshape(n_p, D)
    xs = x_sample.reshape(n_s, D)

    c_all = jnp.zeros((32, D), F32).at[:bp].set(c_prompt).at[bp:bp + bs].set(c_sample)
    mod = _mod_call(c_all, w_ada[0], b_ada[0].reshape(1, 6 * D))
    sh1, sc1, gt1, sh2, sc2, gt2 = [mod[:, j * D:(j + 1) * D] for j in range(6)]

    def table_and_sample_rows(t):
        return t, jnp.repeat(t[bp:bp + bs], ls, axis=0)

    m1 = table_and_sample_rows(jnp.concatenate([sh1, sc1], axis=1))
    m2 = table_and_sample_rows(jnp.concatenate([gt1, sh2, sc2], axis=1))
    m3 = table_and_sample_rows(gt2)

    wi = w_in[0]
    wa = wi[:, :C_QB].astype(BF16)
    wb = wi[:, C_QB + GATE_RANK:].astype(BF16)
    wlr = jnp.pad(wi[:, 3072:3072 + GATE_RANK], ((0, 0), (0, LANES - GATE_RANK))).astype(BF16)
    w2 = jnp.pad(w_gla_gate2[0], ((0, LANES - GATE_RANK), (0, 0))).astype(BF16)
    arrs = _proj_call(geom, n, xp, xs, *m1, g_norm_mix[0].reshape(1, D), hgrn_lb_logits,
                      wa, wb, wlr, w2, b_gla_gate[0].reshape(1, -1))
    qa, ka, va, ra, ga, qb, kb, gb, ib, rb, ua, ub, worst_decay = arrs
    frame_path = (worst_decay[::8, 0] > EXP_CLAMP - 1.0).astype(I32)
    rec_in = (qa, ka, va, ra, ga, qb, kb, gb, ib, rb)

    gna = g_gla_onorm[0].reshape(1, GLA_DV)
    gnb = g_hgrn_onorm[0].reshape(1, HG_DV)
    step_flags = jnp.max(frame_path[:n_p // TM].reshape(-1, REC_ROWS // TM), axis=1)
    oap, obp, sa_p, sb_p = _rec_call(step_flags, rec_in, gna, gnb, bp, lp, REC_ROWS, 0)
    oas, obs, sa_s, sb_s = _rec_call(jnp.broadcast_to(frame_path[n_p // TM], (bs,)), rec_in, gna,
                                     gnb, bs, ls, ls, n_p, s0=(state_gla[0], state_hgrn[0]))

    wr = jnp.pad(w_router[0], ((0, 0), (0, LANES - N_EXP)))
    wr_hi, wr_lo = _split_bf16(wr)
    br = jnp.pad(b_router[0], (0, LANES - N_EXP), constant_values=NEG).reshape(1, LANES)
    x1, h2, logits = _post_call(geom, n, xp, xs, oap, oas, obp, obs, ua, ub, *m2,
                                g_norm_ffn[0].reshape(1, D), w_branch_a[0].astype(BF16),
                                w_branch_b[0].astype(BF16), w_out[0].astype(BF16), wr_hi, wr_lo, br)

    n_tiles = n // TM
    te, tw, cnt = _route_call(logits, n)
    run = (cnt[::8, :N_EXP].astype(I32) + RUN - 1) // RUN * RUN
    counts = jnp.sum(run, axis=0)
    seg_len = (counts + BM - 1) // BM * BM
    pend = jnp.cumsum(seg_len).astype(I32)
    run_dst = (pend - seg_len)[None, :] + jnp.cumsum(run, axis=0) - run
    run_src = jnp.cumsum(run, axis=1) - run
    tab = _piece_table(run, run_src, run_dst)
    gsrc = jnp.repeat(jnp.pad(run_src.astype(F32), ((0, 0), (0, LANES - N_EXP))), 8, axis=0)
    cap = (n * TOP_K + n_tiles * N_EXP * (RUN - 1) + N_EXP * (BM - 1) + BM - 1) // BM * BM
    block_start = jnp.arange(cap // BM, dtype=I32) * BM
    block_e = jnp.minimum(jnp.sum((pend[None, :] <= block_start[:, None]).astype(I32), axis=1),
                          N_EXP - 1)
    n_valid = pend[-1:] // BM
    ids = jnp.arange(N_EXP, dtype=I32)
    nonempty = counts > 0
    later = jnp.where(nonempty[None, :] & (ids[None, :] > ids[:, None]), ids[None, :], N_EXP)
    next_e = jnp.min(later, axis=1)
    next_e = jnp.where(next_e == N_EXP, -1, next_e)
    is_e = block_e[:, None] == ids[None, :]
    seg = jnp.sum(jnp.where(is_e, (jnp.cumsum(nonempty.astype(I32)) - 1)[None, :], 0), axis=1)
    nxt = jnp.sum(jnp.where(is_e, next_e[None, :], 0), axis=1)
    fill = jnp.clip(jnp.sum(jnp.where(is_e, (pend - seg_len + counts)[None, :], 0), axis=1)
                    - block_start, 0, BM)
    xs_sorted, pos = _dispatch_call(pend, tab, te, gsrc, h2, cap)
    eo = _expert_call(block_e, n_valid, seg, nxt, fill, xs_sorted, w_gate_up[0],
                      b_gate_up[0].reshape(N_EXP, 1, -1), w_down[0], b_down[0].reshape(N_EXP, 1, -1))

    yp, ys = _combine_call(geom, n, tab, eo, pos, x1, tw, *m3, g_final.reshape(1, D))
    return (yp.reshape(bp, lp, D), ys.reshape(bs, ls, D),
            sa_p[None], sb_p[None], sa_s[None], sb_s[None])
```

```python
import functools

import jax
import jax.numpy as jnp
from jax import lax
from jax.experimental import pallas as pl
from jax.experimental.pallas import tpu as pltpu

F32 = jnp.float32
BF16 = jnp.bfloat16
I32 = jnp.int32

EPS = 1e-6
D = 1024
GLA_H, GLA_DK, GLA_DV = 4, 128, 256
HG_H, HG_DK, HG_DV = 8, 128, 128
GATE_RANK = 16
GATE_NORM = 16.0
N_EXP = 32
TOP_K = 4
SWIGLU_LIMIT = 7.0
SWIGLU_ALPHA = 1.702

LANES = 128
TM = 256
CHUNK = 128
SUB = 64
NSUB = CHUNK // SUB
REC_ROWS = 4 * CHUNK
POST_ROWS = 128
EXP_CLAMP = 80.0
BM = 512
ROUTE_ROWS = 1280
RUN = 8
RUN_PIECES = (256, 128, 64, 32, 16, 8)
WAIT_ROWS = 128
SORT_ROWS = TM * TOP_K + 256
TAB_COUNT = 2 * N_EXP * len(RUN_PIECES)
TAB_ROWS = TAB_COUNT + len(RUN_PIECES)
TAB_LEN = 512
NEG = -1e30

C_QA, C_KA, C_VA, C_RA, C_QB, C_FB, C_IB, C_RB, C_UA, C_UB, C_END = (
    0, 512, 1024, 2048, 3072, 4096, 5120, 6144, 7168, 8192, 9216)


def _sigmoid(x):
    return 1.0 / (1.0 + jnp.exp(-x))


def _dot(a, b):
    return jnp.dot(a, b, preferred_element_type=F32)


def _split_bf16(x):
    hi = x.astype(BF16)
    lo = (x - hi.astype(F32)).astype(BF16)
    return hi, lo


def _log2(n):
    assert n > 0 and n & (n - 1) == 0, "power of two expected"
    return n.bit_length() - 1


def _cparams(vmem_mb):
    return pltpu.CompilerParams(dimension_semantics=("arbitrary",),
                                vmem_limit_bytes=vmem_mb * 1024 * 1024)


def _mod_kernel(c_ref, w_ref, b_ref, o_ref):
    c = c_ref[...]
    s = (c * _sigmoid(c)).astype(BF16)
    o_ref[...] = _dot(s, w_ref[...].astype(BF16)) + b_ref[...]


def _mod_call(c_all, w_ada, b_ada):
    rows = c_all.shape[0]
    return pl.pallas_call(
        _mod_kernel,
        grid=(6,),
        in_specs=[pl.BlockSpec((rows, D), lambda j: (0, 0)),
                  pl.BlockSpec((D, D), lambda j: (0, j)),
                  pl.BlockSpec((1, D), lambda j: (0, j))],
        out_specs=pl.BlockSpec((rows, D), lambda j: (0, j)),
        out_shape=jax.ShapeDtypeStruct((rows, 6 * D), F32),
        compiler_params=_cparams(32),
        name="mod",
    )(c_all, w_ada, b_ada)


def _row_mod(i, geom, table_ref, sample_ref):
    n_prompt_rows, prompt_len, _, n_prompt_seq = geom
    seq = jnp.minimum((i * TM) >> _log2(prompt_len), n_prompt_seq - 1)
    return jnp.where(i < n_prompt_rows // TM, table_ref[pl.ds(seq, 1), :], sample_ref[...])


def _proj_kernel(geom, xp_ref, xs_ref, mtab_ref, msmp_ref, g_ref, lbl_ref, wa_ref, wb_ref, wlr_ref,
                 w2_ref, bg_ref,
                 qa_ref, ka_ref, va_ref, ra_ref, ga_ref, qb_ref, kb_ref, gb_ref, ib_ref,
                 rb_ref, ua_ref, ub_ref, wd_ref):
    i = pl.program_id(0)
    n_prompt_tiles = geom[0] // TM
    x = jnp.where(i < n_prompt_tiles, xp_ref[...], xs_ref[...])
    mod = _row_mod(i, geom, mtab_ref, msmp_ref)
    sh, sc = mod[:, :D], mod[:, D:]
    ms = jnp.mean(x * x, axis=-1, keepdims=True)
    h = x * lax.rsqrt(ms + EPS) * g_ref[...]
    hb = (h * (1.0 + sc) + sh).astype(BF16)

    def proj(a, b):
        if b <= C_QB:
            return _dot(hb, wa_ref[:, a:b])
        return _dot(hb, wb_ref[:, a - C_QB:b - C_QB])

    cols = (C_QA, C_KA, C_VA, C_RA, C_QB, C_FB, C_IB, C_RB, C_UA, C_UB, C_END)
    z_qa, z_ka, z_va, z_ra, z_qb, z_fb, z_ib, z_rb, z_ua, z_ub = [
        proj(a, b) for a, b in zip(cols[:-1], cols[1:])]
    lr_hi, lr_lo = _split_bf16(_dot(hb, wlr_ref[...]))
    qa_ref[...] = (z_qa * GLA_DK ** -0.5).astype(BF16)
    ka_ref[...] = z_ka.astype(BF16)
    va_ref[...] = z_va.astype(BF16)
    ra_ref[...] = (z_ra * _sigmoid(z_ra)).astype(BF16)
    xg = _dot(lr_hi, w2_ref[...]) + _dot(lr_lo, w2_ref[...]) + bg_ref[...]
    ga = (jnp.minimum(xg, 0.0) - jnp.log1p(jnp.exp(-jnp.abs(xg)))) * (1.0 / GATE_NORM)
    ga_ref[...] = ga
    qb_ref[...] = (z_qb * _sigmoid(z_qb) * HG_DK ** -0.5).astype(BF16)
    lbl = lbl_ref[...]
    e = jnp.exp(lbl - jnp.max(lbl, axis=0, keepdims=True))
    lb = e[0:1, :] / jnp.sum(e, axis=0, keepdims=True)
    kb_ref[...] = ((1.0 - lb) * _sigmoid(-z_fb)).astype(BF16)
    gb = jnp.log(lb + (1.0 - lb) * _sigmoid(z_fb))
    gb_ref[...] = gb
    worst = jnp.zeros((1, 1), F32)
    for g in (ga, gb):
        for j in range(TM // SUB):
            total = jnp.sum(g[j * SUB:(j + 1) * SUB, :], axis=0, keepdims=True)
            worst = jnp.maximum(worst, jnp.max(-total, axis=1, keepdims=True))
    wd_ref[...] = jnp.broadcast_to(worst, (8, LANES))
    ib_ref[...] = z_ib.astype(BF16)
    rb_ref[...] = (z_rb * _sigmoid(z_rb)).astype(BF16)
    ua_ref[...] = _sigmoid(z_ua).astype(BF16)
    ub_ref[...] = _sigmoid(z_ub).astype(BF16)


def _proj_call(geom, n_rows, xp, xs, mtab, msmp, g_mix, lbl, wa, wb, wlr, w2, bg):
    n_tiles = n_rows // TM
    last_p = geom[0] // TM - 1
    const = lambda i: (0, 0)
    row = lambda i: (i, 0)
    widths = [(512, BF16), (512, BF16), (D, BF16), (D, BF16), (512, F32), (D, BF16), (D, BF16),
              (D, F32), (D, BF16), (D, BF16), (D, BF16), (D, BF16)]
    return pl.pallas_call(
        functools.partial(_proj_kernel, geom),
        grid=(n_tiles,),
        in_specs=[pl.BlockSpec((TM, D), lambda i: (jnp.minimum(i, last_p), 0)),
                  pl.BlockSpec(xs.shape, const),
                  pl.BlockSpec(mtab.shape, const), pl.BlockSpec(msmp.shape, const),
                  pl.BlockSpec((1, D), const), pl.BlockSpec(lbl.shape, const),
                  pl.BlockSpec(wa.shape, const, pipeline_mode=pl.Buffered(1)),
                  pl.BlockSpec(wb.shape, const, pipeline_mode=pl.Buffered(1)),
                  pl.BlockSpec(wlr.shape, const), pl.BlockSpec(w2.shape, const),
                  pl.BlockSpec(bg.shape, const)],
        out_specs=[pl.BlockSpec((TM, w), row) for w, _ in widths] + [pl.BlockSpec((8, LANES), row)],
        out_shape=([jax.ShapeDtypeStruct((n_rows, w), dt) for w, dt in widths]
                   + [jax.ShapeDtypeStruct((n_tiles * 8, LANES), F32)]),
        compiler_params=_cparams(56),
        name="proj",
    )(xp, xs, mtab, msmp, g_mix, lbl, wa, wb, wlr, w2, bg)


def _rec_prep(tri, g, q, k):
    g_hi, g_lo = _split_bf16(g)
    cum = _dot(tri, g_hi) + _dot(tri, g_lo)
    width = cum.shape[1]
    tot = cum[CHUNK - 1:CHUNK, :]
    refs = [jnp.zeros((1, width), F32)] + [cum[j * SUB - 1:j * SUB, :] for j in range(1, NSUB)]
    d = cum - jnp.concatenate([jnp.broadcast_to(b, (SUB, width)) for b in refs], axis=0)
    qn = q.astype(F32) * jnp.exp(d)
    kn = k.astype(F32) * jnp.exp(jnp.minimum(-d, EXP_CLAMP))
    sub = lambda x, j: x[j * SUB:(j + 1) * SUB, :]
    qs = jnp.concatenate([sub(qn, j) * jnp.exp(refs[j]) for j in range(NSUB)],
                         axis=0).astype(BF16)
    kd = jnp.concatenate([sub(kn, j) * jnp.exp(tot - refs[j]) for j in range(NSUB)], axis=0)
    et = jnp.exp(tot)
    zero_rows = lambda n: jnp.zeros((n, width), F32)
    q_to, k_of = [], []
    for j in range(NSUB):
        parts = [zero_rows(j * SUB)] if j else []
        parts += [sub(qn, i) * jnp.exp(refs[i] - refs[j]) if i > j else sub(qn, i)
                  for i in range(j, NSUB)]
        q_to.append(jnp.concatenate(parts, axis=0).astype(BF16))
        parts = ([zero_rows(j * SUB)] if j else []) + [sub(kn, j)]
        if j + 1 < NSUB:
            parts.append(zero_rows(CHUNK - (j + 1) * SUB))
        k_of.append(jnp.concatenate(parts, axis=0).astype(BF16))
    return qs, kd, et, q_to, k_of


def _rec_kernel(rows, has_s0, fp_ref, *refs):
    (qa_ref, ka_ref, va_ref, ra_ref, ga_ref, qb_ref, kb_ref, gb_ref, ib_ref, rb_ref,
     gna_ref, gnb_ref) = refs[:12]
    refs = refs[12:]
    if has_s0:
        s0a_ref, s0b_ref = refs[:2]
        refs = refs[2:]
    oga_ref, ogb_ref, sa_ref, sb_ref, qf, kf, vf, of = refs

    @pl.when(pl.program_id(1) == 0)
    def _():
        if has_s0:
            sa_ref[...] = s0a_ref[...]
            sb_ref[...] = s0b_ref[...]
        else:
            sa_ref[...] = jnp.zeros_like(sa_ref)
            sb_ref[...] = jnp.zeros_like(sb_ref)

    ti = lax.broadcasted_iota(I32, (CHUNK, CHUNK), 0)
    si = lax.broadcasted_iota(I32, (CHUNK, CHUNK), 1)
    causal = ti >= si
    tri = causal.astype(BF16)
    n_chunks = max(rows // CHUNK, 1)
    rows_out = min(rows, CHUNK)

    def load(ref, c):
        if rows >= CHUNK:
            return ref[c * CHUNK:(c + 1) * CHUNK, :]
        x = ref[...]
        return jnp.concatenate([x, jnp.zeros((CHUNK - rows, x.shape[1]), x.dtype)], axis=0)

    mixers = ((GLA_H, GLA_DK, GLA_DV, ga_ref, qa_ref, ka_ref, va_ref, ra_ref, sa_ref, gna_ref, oga_ref),
              (HG_H, HG_DK, HG_DV, gb_ref, qb_ref, kb_ref, ib_ref, rb_ref, sb_ref, gnb_ref, ogb_ref))
    use_frames = fp_ref[pl.program_id(0) * pl.num_programs(1) + pl.program_id(1)] != 0
    pl.when(jnp.logical_not(use_frames))(
        lambda: _rec_factored(tri, causal, load, mixers, n_chunks, rows_out))
    pl.when(use_frames)(lambda: _rec_frames(rows, mixers, qf, kf, vf, of))


def _rec_factored(tri, causal, load, mixers, n_chunks, rows_out):
    units = []
    for c in range(n_chunks):
        for (nh, dk, dv, g_ref, q_ref, k_ref, v_ref, r_ref, s_ref, gn_ref, o_ref) in mixers:
            qs, kd, et, q_to, k_of = _rec_prep(tri, load(g_ref, c), load(q_ref, c), load(k_ref, c))
            for h in range(nh):
                ks = slice(h * dk, (h + 1) * dk)
                units.append(dict(
                    c=c, h=h, dk=dk, dv=dv, vs=slice(h * dv, (h + 1) * dv), s_ref=s_ref,
                    gn_ref=gn_ref, o_ref=o_ref, v_ref=v_ref, r_ref=r_ref,
                    qs=qs[:, ks], kd=kd[:, ks], et=et[:, ks],
                    q_to=jnp.concatenate([x[:, ks] for x in q_to], axis=1),
                    k_of=jnp.concatenate([x[:, ks] for x in k_of], axis=1)))
    for u in units:
        u["a"] = lax.dot_general(u["q_to"], u["k_of"], (((1,), (1,)), ((), ())),
                                 preferred_element_type=F32)
    for u in units:
        u["lhs"] = jnp.concatenate([jnp.where(causal, u["a"], 0.0).astype(BF16),
                                    u["kd"].T.astype(BF16)], axis=0)
    for u in units:
        u["av"] = _dot(u["lhs"], load(u["v_ref"], u["c"])[:, u["vs"]])
    for u in units:
        dk, dv, h = u["dk"], u["dv"], u["h"]
        s = u["s_ref"][0, h]
        o = _dot(u["qs"], s.astype(BF16)) + u["av"][:CHUNK]
        et_col = jnp.broadcast_to(u["et"], (dk, dk)).T
        u["s_ref"][0, h] = s * jnp.tile(et_col, (1, dv // dk)) + u["av"][CHUNK:]
        ms = jnp.mean(o * o, axis=-1, keepdims=True)
        og = (o * lax.rsqrt(ms + EPS) * u["gn_ref"][...]
              * load(u["r_ref"], u["c"])[:, u["vs"]].astype(F32))
        r0 = u["c"] * CHUNK
        u["o_ref"][r0:r0 + rows_out, u["vs"]] = og[:rows_out].astype(BF16)


def _rec_frames(rows, mixers, qf, kf, vf, of):
    sublane = lax.broadcasted_iota(I32, (8, 1), 0)
    of[...] = jnp.zeros_like(of)
    for (nh, dk, dv, g_ref, q_ref, k_ref, v_ref, r_ref, s_ref, gn_ref, o_ref) in mixers:
        qf[:, :nh * dk] = q_ref[...].astype(F32)
        kf[:, :nh * dk] = k_ref[...].astype(F32)
        vf[:, :nh * dv] = v_ref[...].astype(F32)
        for h in range(nh):
            ks = slice(h * dk, (h + 1) * dk)
            vs = slice(h * dv, (h + 1) * dv)

            def frame(t, st, ks=ks, vs=vs, g_ref=g_ref):
                rows8 = pl.ds(pl.multiple_of((t >> 3) << 3, 8), 8)
                is_t = sublane == (t & 7)
                at = lambda ref, cols: jnp.where(is_t, ref[rows8, cols], 0.0)
                kv = lax.dot_general(at(vf, vs).astype(BF16), at(kf, ks).astype(BF16),
                                     (((0,), (0,)), ((), ())),
                                     preferred_element_type=F32)
                g_t = jnp.sum(at(g_ref, ks), axis=0, keepdims=True)
                st = st * jnp.exp(g_t) + kv
                o8 = lax.dot_general(at(qf, ks).astype(BF16), st.astype(BF16),
                                     (((1,), (1,)), ((), ())),
                                     preferred_element_type=F32)
                of[rows8, vs] = jnp.where(is_t, o8, of[rows8, vs])
                return st

            s_ref[0, h] = lax.fori_loop(0, rows, frame, s_ref[0, h].T).T
            o = of[:, vs]
            ms = jnp.mean(o * o, axis=-1, keepdims=True)
            og = o * lax.rsqrt(ms + EPS) * gn_ref[...] * r_ref[:, vs].astype(F32)
            o_ref[:, vs] = og.astype(BF16)


def _rec_call(frame_path, arrs, gna, gnb, n_seq, seq_len, rows, row0, s0=None):
    steps = seq_len // rows
    blk0 = row0 // rows
    row_in = lambda b, t, fp: (blk0 + b * steps + t, 0)
    row_out = lambda b, t, fp: (b * steps + t, 0)
    const = lambda b, t, fp: (0, 0)
    st = lambda b, t, fp: (b, 0, 0, 0)
    in_specs = [pl.BlockSpec((rows, a.shape[1]), row_in) for a in arrs]
    in_specs += [pl.BlockSpec(gna.shape, const), pl.BlockSpec(gnb.shape, const)]
    args = list(arrs) + [gna, gnb]
    if s0 is not None:
        in_specs += [pl.BlockSpec((1, GLA_H, GLA_DK, GLA_DV), st),
                     pl.BlockSpec((1, HG_H, HG_DK, HG_DV), st)]
        args += [s0[0], s0[1]]
    n_rows = n_seq * seq_len
    return pl.pallas_call(
        functools.partial(_rec_kernel, rows, s0 is not None),
        grid_spec=pltpu.PrefetchScalarGridSpec(
            num_scalar_prefetch=1,
            grid=(n_seq, steps),
            in_specs=in_specs,
            out_specs=[pl.BlockSpec((rows, D), row_out), pl.BlockSpec((rows, D), row_out),
                       pl.BlockSpec((1, GLA_H, GLA_DK, GLA_DV), st),
                       pl.BlockSpec((1, HG_H, HG_DK, HG_DV), st)],
            scratch_shapes=[pltpu.VMEM((rows, D), F32)] * 4),
        out_shape=[jax.ShapeDtypeStruct((n_rows, D), BF16), jax.ShapeDtypeStruct((n_rows, D), BF16),
                   jax.ShapeDtypeStruct((n_seq, GLA_H, GLA_DK, GLA_DV), F32),
                   jax.ShapeDtypeStruct((n_seq, HG_H, HG_DK, HG_DV), F32)],
        compiler_params=pltpu.CompilerParams(dimension_semantics=("arbitrary", "arbitrary"),
                                             vmem_limit_bytes=48 * 1024 * 1024),
        name="rec_s0" if s0 is not None else "rec",
    )(frame_path, *args)


def _post_kernel(geom, xp_ref, xs_ref, oap_ref, oas_ref, obp_ref, obs_ref, ua_ref, ub_ref,
                 mtab_ref, msmp_ref, gf_ref, wba_ref, wbb_ref, wo_ref, wrh_ref, wrl_ref, br_ref,
                 x1_ref, h2_ref, lg_ref):
    i = pl.program_id(0)
    is_prompt = i < geom[0] // TM
    mod = _row_mod(i, geom, mtab_ref, msmp_ref)
    groups = [slice(j * POST_ROWS, (j + 1) * POST_ROWS) for j in range(TM // POST_ROWS)]
    pa = [_dot(jnp.where(is_prompt, oap_ref[g, :], oas_ref[g, :]), wba_ref[...]) for g in groups]
    pb = [_dot(jnp.where(is_prompt, obp_ref[g, :], obs_ref[g, :]), wbb_ref[...]) for g in groups]
    merged = [(ua_ref[g, :].astype(F32) * a + ub_ref[g, :].astype(F32) * b).astype(BF16)
              for g, a, b in zip(groups, pa, pb)]
    y = [_dot(m, wo_ref[...]) for m in merged]
    h2s = []
    for g, yg in zip(groups, y):
        x1 = jnp.where(is_prompt, xp_ref[g, :], xs_ref[g, :]) + mod[g, :D] * yg
        x1_ref[g, :] = x1
        ms = jnp.mean(x1 * x1, axis=-1, keepdims=True)
        h2 = x1 * lax.rsqrt(ms + EPS) * gf_ref[...] * (1.0 + mod[g, 2 * D:]) + mod[g, D:2 * D]
        h2_ref[g, :] = h2
        h2s.append(_split_bf16(h2))
    for g, (h_hi, h_lo) in zip(groups, h2s):
        lg_ref[g, :] = (_dot(h_hi, wrh_ref[...]) + _dot(h_lo, wrh_ref[...])
                        + _dot(h_hi, wrl_ref[...]) + br_ref[...])


def _post_call(geom, n_rows, xp, xs, oap, oas, obp, obs, ua, ub, mtab, msmp, g_ffn, wba, wbb, wo,
               wrh, wrl, br):
    n_tiles = n_rows // TM
    last_p = geom[0] // TM - 1
    const = lambda i: (0, 0)
    row = lambda i: (i, 0)
    prow = pl.BlockSpec((TM, D), lambda i: (jnp.minimum(i, last_p), 0))
    full = lambda a: pl.BlockSpec(a.shape, const)
    return pl.pallas_call(
        functools.partial(_post_kernel, geom),
        grid=(n_tiles,),
        in_specs=[prow, full(xs), prow, full(oas), prow, full(obs),
                  pl.BlockSpec((TM, D), row), pl.BlockSpec((TM, D), row),
                  full(mtab), full(msmp), full(g_ffn), full(wba), full(wbb), full(wo),
                  full(wrh), full(wrl), full(br)],
        out_specs=[pl.BlockSpec((TM, D), row), pl.BlockSpec((TM, D), row),
                   pl.BlockSpec((TM, LANES), row)],
        out_shape=[jax.ShapeDtypeStruct((n_rows, D), F32), jax.ShapeDtypeStruct((n_rows, D), F32),
                   jax.ShapeDtypeStruct((n_rows, LANES), F32)],
        compiler_params=_cparams(48),
        name="post",
    )(xp, xs, oap, oas, obp, obs, ua, ub, mtab, msmp, g_ffn, wba, wbb, wo, wrh, wrl, br)


def _multi_hot(te):
    lane = lax.broadcasted_iota(I32, (TM, LANES), 1)
    m = jnp.zeros((TM, LANES), F32)
    for k in range(TOP_K):
        m = m + (lane == te[:, k:k + 1]).astype(F32)
    return m


def _route_kernel(lg_ref, te_ref, tw_ref, cnt_ref):
    logit = lg_ref[...]
    lane = lax.broadcasted_iota(I32, (ROUTE_ROWS, LANES), 1)
    lane_f = lane.astype(F32)
    vals, idxs = [], []
    for _ in range(TOP_K):
        m = jnp.max(logit, axis=-1, keepdims=True)
        idx = jnp.min(jnp.where(logit == m, lane_f, float(LANES)), axis=-1, keepdims=True)
        vals.append(m)
        idxs.append(idx)
        logit = jnp.where(lane_f == idx, -jnp.inf, logit)
    es = [jnp.exp(v - vals[0]) for v in vals]
    den = es[0] + es[1] + es[2] + es[3]
    te = jnp.zeros((ROUTE_ROWS, LANES), F32)
    tw = jnp.zeros((ROUTE_ROWS, LANES), F32)
    for k in range(TOP_K):
        te = jnp.where(lane == k, idxs[k], te)
        tw = jnp.where(lane == k, es[k] / den, tw)
    te = te.astype(I32)
    te_ref[...] = te
    tw_ref[...] = tw
    for t in range(ROUTE_ROWS // TM):
        cnt_ref[t * 8:(t + 1) * 8, :] = jnp.broadcast_to(
            jnp.sum(_multi_hot(te[t * TM:(t + 1) * TM]), axis=0, keepdims=True), (8, LANES))


def _route_call(logits, n_rows):
    n_steps = n_rows // ROUTE_ROWS
    row = lambda i: (i, 0)
    tiles = ROUTE_ROWS // TM
    return pl.pallas_call(
        _route_kernel,
        grid=(n_steps,),
        in_specs=[pl.BlockSpec((ROUTE_ROWS, LANES), row)],
        out_specs=[pl.BlockSpec((ROUTE_ROWS, LANES), row), pl.BlockSpec((ROUTE_ROWS, LANES), row),
                   pl.BlockSpec((tiles * 8, LANES), row)],
        out_shape=[jax.ShapeDtypeStruct((n_rows, LANES), I32),
                   jax.ShapeDtypeStruct((n_rows, LANES), F32),
                   jax.ShapeDtypeStruct((n_rows // TM * 8, LANES), F32)],
        compiler_params=_cparams(32),
        name="route",
    )(logits)


def _wait_rows(rows, wait_n_rows):
    lax.fori_loop(0, rows >> _log2(WAIT_ROWS), lambda j, c: (wait_n_rows(WAIT_ROWS), c)[1], 0)
    lax.fori_loop(0, (rows & (WAIT_ROWS - 1)) >> _log2(RUN),
                  lambda j, c: (wait_n_rows(RUN), c)[1], 0)


def _for_each_run_piece(tab_ref, fn):
    for s, p in enumerate(RUN_PIECES):
        def body(j, c, s=s, p=p):
            fn(pl.multiple_of(tab_ref[0, 0, s * 2 * N_EXP + j], RUN),
               pl.multiple_of(tab_ref[0, 0, (s * 2 + 1) * N_EXP + j], RUN), p)
            return c

        lax.fori_loop(0, tab_ref[0, 0, TAB_COUNT + s], body, 0)


def _piece_table(run, run_src, run_dst):
    ids = jnp.arange(N_EXP, dtype=I32)
    earlier = (ids[None, :] < ids[:, None]).astype(I32)
    src_l, dst_l, cnt_l = [], [], []
    for p in RUN_PIECES:
        has = (run & p) != 0
        off = run - (run & (2 * p - 1))
        slot = jnp.sum(has[:, None, :] * earlier[None, :, :], axis=2)
        put = has[:, :, None] & (slot[:, :, None] == ids[None, None, :])
        src_l.append(jnp.sum(jnp.where(put, (run_src + off)[:, :, None], 0), axis=1))
        dst_l.append(jnp.sum(jnp.where(put, (run_dst + off)[:, :, None], 0), axis=1))
        cnt_l.append(jnp.sum(has.astype(I32), axis=1))
    cols = [a for pair in zip(src_l, dst_l) for a in pair]
    tab = jnp.concatenate(cols + [jnp.stack(cnt_l, axis=1), jnp.sum(run, axis=1, keepdims=True)],
                          axis=1)
    tab = jnp.pad(tab, ((0, 0), (0, TAB_LEN - tab.shape[1])))
    return tab.reshape(run.shape[0], 1, TAB_LEN)


def _dispatch_kernel(pend_ref, tab_ref, tprev_ref, te_ref, gsrc_ref, h2_ref, xs_ref, pos_ref, z_ref,
                     zeros, sem, zsem):
    @pl.when(pl.program_id(0) == 0)
    def _():
        zeros[...] = jnp.zeros_like(zeros)

        def zero_copy(e):
            start = pl.multiple_of(pend_ref[e] - BM, BM)
            return pltpu.make_async_copy(zeros, xs_ref.at[pl.ds(start, BM)], zsem)

        def nonempty(e):
            return pend_ref[e] > (pend_ref[e - 1] if e > 0 else 0)

        for e in range(N_EXP):
            pl.when(nonempty(e))(lambda e=e: zero_copy(e).start())
        for e in range(N_EXP):
            pl.when(nonempty(e))(lambda e=e: zero_copy(e).wait())

        def tail_copy(j):
            return pltpu.make_async_copy(zeros, xs_ref.at[pl.ds(pl.multiple_of(j * BM, BM), BM)],
                                         zsem)

        first, last = pend_ref[N_EXP - 1] // BM, xs_ref.shape[0] // BM
        lax.fori_loop(first, last, lambda j, c: (tail_copy(j).start(), c)[1], 0)
        lax.fori_loop(first, last, lambda j, c: (tail_copy(j).wait(), c)[1], 0)

    te = te_ref[...]
    ri = lax.broadcasted_iota(I32, (TM, TM), 0)
    ci = lax.broadcasted_iota(I32, (TM, TM), 1)
    rank = _dot((ri > ci).astype(BF16), _multi_hot(te).astype(BF16)) + gsrc_ref[0:1, :]
    lane = lax.broadcasted_iota(I32, (TM, LANES), 1)
    pos = jnp.zeros((TM, LANES), F32)
    for k in range(TOP_K):
        p = jnp.sum(jnp.where(lane == te[:, k:k + 1], rank, 0.0), axis=-1, keepdims=True)
        pos = jnp.where(lane == k, p, pos)
    pos_ref[...] = pos.astype(I32)
    pos_t = pos.T.astype(I32)
    row = lax.broadcasted_iota(I32, (SORT_ROWS, TM), 0)
    onehot = jnp.zeros((SORT_ROWS, TM), F32)
    for k in range(TOP_K):
        onehot = jnp.where(row == pos_t[k:k + 1, :], 1.0, onehot)
    i = pl.program_id(0)
    slot = lax.rem(i, 2)
    z_ref[slot] = _dot(onehot.astype(BF16), h2_ref[...].astype(BF16))

    _for_each_run_piece(tab_ref, lambda s, d, p: pltpu.make_async_copy(
        z_ref.at[slot, pl.ds(s, p)], xs_ref.at[pl.ds(d, p)], sem.at[slot]).start())

    def wait_tile(table_ref, which):
        _wait_rows(table_ref[0, 0, TAB_ROWS], lambda n: pltpu.make_async_copy(
            z_ref.at[which, pl.ds(0, n)], xs_ref.at[pl.ds(0, n)], sem.at[which]).wait())

    pl.when(i > 0)(lambda: wait_tile(tprev_ref, 1 - slot))
    pl.when(i == pl.num_programs(0) - 1)(lambda: wait_tile(tab_ref, slot))


def _dispatch_call(pend, tab, te, gsrc, h2, cap):
    n_tiles = tab.shape[0]
    row = lambda i, pe: (i, 0)
    return pl.pallas_call(
        _dispatch_kernel,
        grid_spec=pltpu.PrefetchScalarGridSpec(
            num_scalar_prefetch=1,
            grid=(n_tiles,),
            in_specs=[pl.BlockSpec((1, 1, TAB_LEN), lambda i, pe: (i, 0, 0), memory_space=pltpu.SMEM),
                      pl.BlockSpec((1, 1, TAB_LEN), lambda i, pe: (jnp.maximum(i - 1, 0), 0, 0),
                                   memory_space=pltpu.SMEM),
                      pl.BlockSpec((TM, LANES), row), pl.BlockSpec((8, LANES), row),
                      pl.BlockSpec((TM, D), row)],
            out_specs=[pl.BlockSpec(memory_space=pl.ANY), pl.BlockSpec((TM, LANES), row)],
            scratch_shapes=[pltpu.VMEM((2, SORT_ROWS, D), F32), pltpu.VMEM((BM, D), F32),
                            pltpu.SemaphoreType.DMA((2,)), pltpu.SemaphoreType.DMA]),
        out_shape=[jax.ShapeDtypeStruct((cap, D), F32),
                   jax.ShapeDtypeStruct((n_tiles * TM, LANES), I32)],
        compiler_params=_cparams(40),
        name="dispatch",
    )(pend, tab, tab, te, gsrc, h2)


def _expert_kernel(be_ref, nv_ref, seg_ref, nxt_ref, fill_ref, x_ref, wgu_hbm, bgu_ref, wd_hbm, bd_ref,
                   o_ref, wgu_f, wd_f, wgu_s, wd_s, sem):
    j = pl.program_id(0)
    jc = jnp.minimum(j, nv_ref[0] - 1)
    e = be_ref[jc]
    slot = lax.rem(seg_ref[jc], 2)
    first = (j == 0) | ((j < nv_ref[0]) & (e != be_ref[jnp.maximum(jc - 1, 0)]))

    def fetch(expert, into):
        return (pltpu.make_async_copy(wgu_hbm.at[expert], wgu_f.at[into], sem.at[0, into]),
                pltpu.make_async_copy(wd_hbm.at[expert], wd_f.at[into], sem.at[1, into]))

    @pl.when(j == 0)
    def _():
        for c in fetch(e, slot):
            c.start()

    @pl.when(first)
    def _():
        for c in fetch(e, slot):
            c.wait()

        @pl.when(nxt_ref[jc] >= 0)
        def _():
            for c in fetch(nxt_ref[jc], 1 - slot):
                c.start()

        wgu_s[...] = wgu_f[slot].astype(BF16)
        wd_s[...] = wd_f[slot].astype(BF16)

    def mlp(groups):
        gus = [_dot(x_ref[g, :].astype(BF16), wgu_s[...]) + bgu_ref[0] for g in groups]
        acts = []
        for gu in gus:
            gate = jnp.minimum(gu[:, :D], SWIGLU_LIMIT)
            up = jnp.clip(gu[:, D:], -SWIGLU_LIMIT, SWIGLU_LIMIT)
            acts.append(((up + 1.0) * (gate * _sigmoid(SWIGLU_ALPHA * gate))).astype(BF16))
        for g, act in zip(groups, acts):
            o_ref[g, :] = _dot(act, wd_s[...]) + bd_ref[0]

    lower, upper = slice(0, BM // 2), slice(BM // 2, BM)
    needs_upper = fill_ref[jc] > BM // 2

    @pl.when((j < nv_ref[0]) & needs_upper)
    def _():
        mlp([lower, upper])

    @pl.when((j < nv_ref[0]) & jnp.logical_not(needs_upper))
    def _():
        mlp([lower])
        o_ref[upper, :] = jnp.zeros((BM // 2, D), F32)

    @pl.when(j >= nv_ref[0])
    def _():
        o_ref[...] = jnp.zeros_like(o_ref)


def _expert_call(block_e, n_valid, seg, nxt, fill, xs, wgu, bgu, wd, bd):
    n_blocks = xs.shape[0] // BM
    blk = lambda j, be, nv, sg, nx, fl: (jnp.minimum(j, nv[0] - 1), 0)
    blk_out = lambda j, be, nv, sg, nx, fl: (j, 0)
    exp = lambda j, be, nv, sg, nx, fl: (be[jnp.minimum(j, nv[0] - 1)], 0, 0)
    return pl.pallas_call(
        _expert_kernel,
        grid_spec=pltpu.PrefetchScalarGridSpec(
            num_scalar_prefetch=5,
            grid=(n_blocks,),
            in_specs=[pl.BlockSpec((BM, D), blk),
                      pl.BlockSpec(memory_space=pl.ANY), pl.BlockSpec((1, 1, 2 * D), exp),
                      pl.BlockSpec(memory_space=pl.ANY), pl.BlockSpec((1, 1, D), exp)],
            out_specs=pl.BlockSpec((BM, D), blk_out),
            scratch_shapes=[pltpu.VMEM((2, D, 2 * D), F32), pltpu.VMEM((2, D, D), F32),
                            pltpu.VMEM((D, 2 * D), BF16), pltpu.VMEM((D, D), BF16),
                            pltpu.SemaphoreType.DMA((2, 2))]),
        out_shape=jax.ShapeDtypeStruct(xs.shape, F32),
        compiler_params=_cparams(56),
        name="expert",
    )(block_e, n_valid, seg, nxt, fill, xs, wgu, bgu, wd, bd)


def _combine_kernel(geom, tcur_ref, tnxt_ref, eo_ref, pos_ref, x1_ref, tw_ref, mtab_ref, msmp_ref,
                    gfin_ref, yp_ref, ys_ref, buf, sem):
    i = pl.program_id(0)
    n_prompt_tiles = geom[0] // TM
    slot = lax.rem(i, 2)

    def gather(tab_ref, into):
        _for_each_run_piece(tab_ref, lambda s, d, p: pltpu.make_async_copy(
            eo_ref.at[pl.ds(d, p)], buf.at[into, pl.ds(s, p)], sem.at[into]).start())

    @pl.when(i == 0)
    def _():
        buf[...] = jnp.zeros_like(buf)
        gather(tcur_ref, 0)

    @pl.when(i + 1 < pl.num_programs(0))
    def _():
        gather(tnxt_ref, 1 - slot)

    _wait_rows(tcur_ref[0, 0, TAB_ROWS], lambda n: pltpu.make_async_copy(
        eo_ref.at[pl.ds(0, n)], buf.at[slot, pl.ds(0, n)], sem.at[slot]).wait())
    gt2 = _row_mod(i, geom, mtab_ref, msmp_ref)
    tw = tw_ref[...]
    pos = pos_ref[...]
    col = lax.broadcasted_iota(I32, (TM, SORT_ROWS), 1)
    wsel = jnp.zeros((TM, SORT_ROWS), F32)
    for k in range(TOP_K):
        wsel = jnp.where(col == pos[:, k:k + 1], tw[:, k:k + 1], wsel)
    y = _dot(wsel.astype(BF16), buf[slot].astype(BF16))
    x2 = x1_ref[...] + gt2 * y
    ms = jnp.mean(x2 * x2, axis=-1, keepdims=True)
    out = x2 * lax.rsqrt(ms + EPS) * gfin_ref[...]

    @pl.when(i < n_prompt_tiles)
    def _():
        yp_ref[...] = out

    @pl.when(i >= n_prompt_tiles)
    def _():
        ys_ref[...] = out


def _combine_call(geom, n_rows, tab, eo, pos, x1, tw, mtab, msmp, g_final):
    n_tiles = n_rows // TM
    n_prompt = geom[0]
    last_p = n_prompt // TM - 1
    const = lambda i: (0, 0)
    row = lambda i: (i, 0)
    return pl.pallas_call(
        functools.partial(_combine_kernel, geom),
        grid=(n_tiles,),
        in_specs=[pl.BlockSpec((1, 1, TAB_LEN), lambda i: (i, 0, 0), memory_space=pltpu.SMEM),
                  pl.BlockSpec((1, 1, TAB_LEN), lambda i: (jnp.minimum(i + 1, n_tiles - 1), 0, 0),
                               memory_space=pltpu.SMEM),
                  pl.BlockSpec(memory_space=pl.ANY), pl.BlockSpec((TM, LANES), row),
                  pl.BlockSpec((TM, D), row), pl.BlockSpec((TM, LANES), row),
                  pl.BlockSpec(mtab.shape, const), pl.BlockSpec(msmp.shape, const),
                  pl.BlockSpec((1, D), const)],
        out_specs=[pl.BlockSpec((TM, D), lambda i: (jnp.minimum(i, last_p), 0)),
                   pl.BlockSpec((n_rows - n_prompt, D), const)],
        out_shape=[jax.ShapeDtypeStruct((n_prompt, D), F32),
                   jax.ShapeDtypeStruct((n_rows - n_prompt, D), F32)],
        scratch_shapes=[pltpu.VMEM((2, SORT_ROWS, D), F32), pltpu.SemaphoreType.DMA((2,))],
        compiler_params=_cparams(48),
        name="combine",
    )(tab, tab, eo, pos, x1, tw, mtab, msmp, g_final)


def kernel(x_prompt, x_sample, c_prompt, c_sample, state_gla, state_hgrn, w_ada, b_ada, g_norm_mix,
           g_norm_ffn, w_in, w_gla_gate2, b_gla_gate, g_gla_onorm, hgrn_lb_logits, g_hgrn_onorm,
           w_branch_a, w_branch_b, w_out, w_router, b_router, w_gate_up, b_gate_up, w_down, b_down,
           g_final):
    assert w_ada.shape[0] == 1, "single-layer trunk only"
    bp, lp, _ = x_prompt.shape
    bs, ls, _ = x_sample.shape
    n_p, n_s = bp * lp, bs * ls
    n = n_p + n_s
    assert n_p % TM == 0 and n_s == TM and lp % TM == 0 and lp % REC_ROWS == 0 and ls <= CHUNK
    assert bp + bs <= 32 and REC_ROWS % TM == 0 and TM % SUB == 0
    assert SORT_ROWS >= TM * TOP_K + N_EXP * (RUN - 1) and RUN_PIECES[0] == TM
    assert n % ROUTE_ROWS == 0 and ROUTE_ROWS % TM == 0
    geom = (n_p, lp, ls, bp)
    xp = x_prompt.reshape(n_p, D)
    xs = x_sample.reshape(n_s, D)

    c_all = jnp.zeros((32, D), F32).at[:bp].set(c_prompt).at[bp:bp + bs].set(c_sample)
    mod = _mod_call(c_all, w_ada[0], b_ada[0].reshape(1, 6 * D))
    sh1, sc1, gt1, sh2, sc2, gt2 = [mod[:, j * D:(j + 1) * D] for j in range(6)]

    def table_and_sample_rows(t):
        return t, jnp.repeat(t[bp:bp + bs], ls, axis=0)

    m1 = table_and_sample_rows(jnp.concatenate([sh1, sc1], axis=1))
    m2 = table_and_sample_rows(jnp.concatenate([gt1, sh2, sc2], axis=1))
    m3 = table_and_sample_rows(gt2)

    wi = w_in[0]
    wa = wi[:, :C_QB].astype(BF16)
    wb = wi[:, C_QB + GATE_RANK:].astype(BF16)
    wlr = jnp.pad(wi[:, 3072:3072 + GATE_RANK], ((0, 0), (0, LANES - GATE_RANK))).astype(BF16)
    w2 = jnp.pad(w_gla_gate2[0], ((0, LANES - GATE_RANK), (0, 0))).astype(BF16)
    arrs = _proj_call(geom, n, xp, xs, *m1, g_norm_mix[0].reshape(1, D), hgrn_lb_logits,
                      wa, wb, wlr, w2, b_gla_gate[0].reshape(1, -1))
    qa, ka, va, ra, ga, qb, kb, gb, ib, rb, ua, ub, worst_decay = arrs
    frame_path = (worst_decay[::8, 0] > EXP_CLAMP - 1.0).astype(I32)
    rec_in = (qa, ka, va, ra, ga, qb, kb, gb, ib, rb)

    gna = g_gla_onorm[0].reshape(1, GLA_DV)
    gnb = g_hgrn_onorm[0].reshape(1, HG_DV)
    step_flags = jnp.max(frame_path[:n_p // TM].reshape(-1, REC_ROWS // TM), axis=1)
    oap, obp, sa_p, sb_p = _rec_call(step_flags, rec_in, gna, gnb, bp, lp, REC_ROWS, 0)
    oas, obs, sa_s, sb_s = _rec_call(jnp.broadcast_to(frame_path[n_p // TM], (bs,)), rec_in, gna,
                                     gnb, bs, ls, ls, n_p, s0=(state_gla[0], state_hgrn[0]))

    wr = jnp.pad(w_router[0], ((0, 0), (0, LANES - N_EXP)))
    wr_hi, wr_lo = _split_bf16(wr)
    br = jnp.pad(b_router[0], (0, LANES - N_EXP), constant_values=NEG).reshape(1, LANES)
    x1, h2, logits = _post_call(geom, n, xp, xs, oap, oas, obp, obs, ua, ub, *m2,
                                g_norm_ffn[0].reshape(1, D), w_branch_a[0].astype(BF16),
                                w_branch_b[0].astype(BF16), w_out[0].astype(BF16), wr_hi, wr_lo, br)

    n_tiles = n // TM
    te, tw, cnt = _route_call(logits, n)
    run = (cnt[::8, :N_EXP].astype(I32) + RUN - 1) // RUN * RUN
    counts = jnp.sum(run, axis=0)
    seg_len = (counts + BM - 1) // BM * BM
    pend = jnp.cumsum(seg_len).astype(I32)
    run_dst = (pend - seg_len)[None, :] + jnp.cumsum(run, axis=0) - run
    run_src = jnp.cumsum(run, axis=1) - run
    tab = _piece_table(run, run_src, run_dst)
    gsrc = jnp.repeat(jnp.pad(run_src.astype(F32), ((0, 0), (0, LANES - N_EXP))), 8, axis=0)
    cap = (n * TOP_K + n_tiles * N_EXP * (RUN - 1) + N_EXP * (BM - 1) + BM - 1) // BM * BM
    block_start = jnp.arange(cap // BM, dtype=I32) * BM
    block_e = jnp.minimum(jnp.sum((pend[None, :] <= block_start[:, None]).astype(I32), axis=1),
                          N_EXP - 1)
    n_valid = pend[-1:] // BM
    ids = jnp.arange(N_EXP, dtype=I32)
    nonempty = counts > 0
    later = jnp.where(nonempty[None, :] & (ids[None, :] > ids[:, None]), ids[None, :], N_EXP)
    next_e = jnp.min(later, axis=1)
    next_e = jnp.where(next_e == N_EXP, -1, next_e)
    is_e = block_e[:, None] == ids[None, :]
    seg = jnp.sum(jnp.where(is_e, (jnp.cumsum(nonempty.astype(I32)) - 1)[None, :], 0), axis=1)
    nxt = jnp.sum(jnp.where(is_e, next_e[None, :], 0), axis=1)
    fill = jnp.clip(jnp.sum(jnp.where(is_e, (pend - seg_len + counts)[None, :], 0), axis=1)
                    - block_start, 0, BM)
    xs_sorted, pos = _dispatch_call(pend, tab, te, gsrc, h2, cap)
    eo = _expert_call(block_e, n_valid, seg, nxt, fill, xs_sorted, w_gate_up[0],
                      b_gate_up[0].reshape(N_EXP, 1, -1), w_down[0], b_down[0].reshape(N_EXP, 1, -1))

    yp, ys = _combine_call(geom, n, tab, eo, pos, x1, tw, *m3, g_final.reshape(1, D))
    return (yp.reshape(bp, lp, D), ys.reshape(bs, ls, D),
            sa_p[None], sb_p[None], sa_s[None], sb_s[None])
```
